```python
import jax, jax.numpy as jnp
from jax import lax
import numpy as np

D_MODEL = 4096
BATCH = 4
SEQ = 2048
DEPTH = 2
DEC_BATCH = 128
DEC_SEQ = 4
PAST_LEN = 16384
PAGE_SIZE = 128

N_EVEN = (DEPTH + 1) // 2
N_ODD = DEPTH // 2
A_HEADS = 8
A_DV = D_MODEL // 2 // A_HEADS
A_DK = A_DV // 2
A_WIDTH = A_HEADS * A_DV
A_QK = A_HEADS * A_DK
A_CHUNK = 64
B_WIDTH = D_MODEL // 2
B_BLOCKS = 16
B_BLOCK = B_WIDTH // B_BLOCKS
B_CONV = 4
LRU_C = 8.0
C_WIDTH = D_MODEL
C_CONV = 3
N_MEM = 256
X_HEADS = 4
X_HEAD_DIM = D_MODEL // X_HEADS
D_FF = 4 * D_MODEL
EPS = 1e-6
EVEN_IN = 2 * A_QK + 2 * A_WIDTH + 2 * A_HEADS + 2 * B_WIDTH
EVEN_SPLITS = (A_QK, 2 * A_QK, 2 * A_QK + A_WIDTH, 2 * A_QK + 2 * A_WIDTH, 2 * A_QK + 2 * A_WIDTH + A_HEADS, 2 * A_QK + 2 * A_WIDTH + 2 * A_HEADS, 2 * A_QK + 2 * A_WIDTH + 2 * A_HEADS + B_WIDTH)
ODD_IN = 3 * C_WIDTH

kernel_name = 'mlstm_rglru_shortconv_memxattn_step'


def rmsnorm(x, g):
    xf = x.astype(jnp.float32)
    y = xf * lax.rsqrt(jnp.mean(xf * xf, axis=-1, keepdims=True) + EPS)
    return (y * g.astype(jnp.float32)).astype(x.dtype)


def causal_dwconv(u, buf, w, b=None):
    width, s = w.shape[0], u.shape[1]
    full = jnp.concatenate([buf.astype(u.dtype), u], axis=1)
    y = full[:, 0:s] * w[0]
    for j in range(1, width):
        y = y + full[:, j:j + s] * w[j]
    if b is not None:
        y = y + b
    return y, full[:, s:]


def mlstm_chunkwise(q, k, v, i_pre, f_pre, c0, n0, m0):
    f32 = jnp.float32
    bsz, s = q.shape[0], q.shape[1]
    L = A_CHUNK if s % A_CHUNK == 0 else s
    nc = s // L

    def to_chunks(t):
        t = t.astype(f32).reshape((bsz, nc, L) + t.shape[2:])
        return jnp.swapaxes(jnp.moveaxis(t, 1, 0), 2, 3)

    qc = to_chunks(q) * (A_DK ** -0.5)
    kc = to_chunks(k)
    vc = to_chunks(v)
    ic = to_chunks(i_pre)
    lfc = jax.nn.log_sigmoid(to_chunks(f_pre))
    causal = jnp.tril(jnp.ones((L, L), dtype=bool))

    def step(carry, inp):
        cm, nv, mv = carry
        qj, kj, vj, ij, lf = inp
        b = jnp.cumsum(lf, axis=-1)
        dmat = jnp.where(causal, b[..., :, None] - b[..., None, :] + ij[..., None, :], -jnp.inf)
        g = b + mv[..., None]
        mt = jnp.maximum(g, jnp.max(dmat, axis=-1))
        w_inter = jnp.exp(g - mt)
        sc = jnp.einsum('bhld,bhsd->bhls', qj, kj) * jnp.exp(dmat - mt[..., None])
        num = w_inter[..., None] * jnp.einsum('bhld,bhde->bhle', qj, cm) + jnp.einsum('bhls,bhse->bhle', sc, vj)
        den = w_inter * jnp.einsum('bhld,bhd->bhl', qj, nv) + jnp.sum(sc, axis=-1)
        h = num / jnp.maximum(jnp.abs(den), jnp.exp(-mt))[..., None]
        m_new = mt[..., -1]
        decay = jnp.exp(b[..., -1] + mv - m_new)
        kw = kj * jnp.exp(b[..., -1:] - b + ij - m_new[..., None])[..., None]
        c_new = decay[..., None, None] * cm + jnp.einsum('bhsd,bhse->bhde', kw, vj)
        n_new = decay[..., None] * nv + jnp.sum(kw, axis=2)
        return (c_new, n_new, m_new), h

    (cf, nf, mf), h = lax.scan(step, (c0.astype(f32), n0.astype(f32), m0.astype(f32)), (qc, kc, vc, ic, lfc))
    h = jnp.swapaxes(jnp.moveaxis(h, 0, 1), 2, 3).reshape(bsz, s, A_HEADS, A_DV)
    return h, cf, nf, mf


def rglru(xc, h0, wa, ba, wi, bi, lam):
    f32 = jnp.float32
    bsz, s, _ = xc.shape
    xf = xc.astype(f32)
    xblk = xf.reshape(bsz, s, B_BLOCKS, B_BLOCK)
    r = jax.nn.sigmoid(jnp.einsum('bsnc,ncd->bsnd', xblk, wa.astype(f32)).reshape(bsz, s, B_WIDTH) + ba)
    ig = jax.nn.sigmoid(jnp.einsum('bsnc,ncd->bsnd', xblk, wi.astype(f32)).reshape(bsz, s, B_WIDTH) + bi)
    log_a = LRU_C * r * jax.nn.log_sigmoid(lam.astype(f32))
    a = jnp.exp(log_a)
    u = jnp.sqrt(-jnp.expm1(2.0 * log_a)) * (ig * xf)
    u = u.at[:, 0].add(a[:, 0] * h0.astype(f32))

    def comb(left, right):
        a1, b1 = left
        a2, b2 = right
        return a1 * a2, a2 * b1 + b2

    _, h = lax.associative_scan(comb, (a, u), axis=1)
    return h, h[:, -1]


def even_mixer(u, w, i, c0, n0, m0, h0, cb0):
    f32 = jnp.float32
    bsz, s, _ = u.shape
    q, k, v, o, ig, fg, xb, gb = jnp.split(u @ w['w_in_e'][i], EVEN_SPLITS, axis=-1)
    ig = ig.astype(f32) + w['b_if'][i, :A_HEADS]
    fg = fg.astype(f32) + w['b_if'][i, A_HEADS:]
    h, cf, nf, mf = mlstm_chunkwise(q.reshape(bsz, s, A_HEADS, A_DK), k.reshape(bsz, s, A_HEADS, A_DK), v.reshape(bsz, s, A_HEADS, A_DV), ig, fg, c0, n0, m0)
    h = h * lax.rsqrt(jnp.mean(h * h, axis=-1, keepdims=True) + EPS)
    h = h.reshape(bsz, s, A_WIDTH) * w['mh_norm'][i].astype(f32)
    ya = jax.nn.sigmoid(o.astype(f32)) * h
    xc, cb = causal_dwconv(xb, cb0, w['conv_b_w'][i], w['conv_b_b'][i])
    hl, h_last = rglru(xc, h0, w['lru_wa'][i], w['lru_ba'][i], w['lru_wi'][i], w['lru_bi'][i], w['lru_lam'][i])
    yb = hl * jax.nn.gelu(gb.astype(f32))
    y = jnp.concatenate([ya, yb], axis=-1).astype(u.dtype) @ w['w_out_e'][i]
    return y, (cf, nf, mf, h_last, cb)


def odd_mixer(u, w, i, sb0):
    bg, cg, hx = jnp.split(u @ w['w_in_o'][i], 3, axis=-1)
    z, sb = causal_dwconv(cg * hx, sb0, w['conv_c_w'][i])
    return (bg * z) @ w['w_out_o'][i], sb


def cross_attn(u, mk, mv, wq, wo):
    bsz, s, _ = u.shape
    q = (u @ wq).reshape(bsz, s, X_HEADS, X_HEAD_DIM)
    sc = jnp.einsum('bqhd,bkhd->bhqk', q, mk).astype(jnp.float32) * (X_HEAD_DIM ** -0.5)
    p = jax.nn.softmax(sc, axis=-1).astype(u.dtype)
    o = jnp.einsum('bhqk,bkhd->bqhd', p, mv).reshape(bsz, s, D_MODEL)
    return o @ wo


def mlp(u, w1, w2):
    return jnp.square(jax.nn.relu(u @ w1)) @ w2


def trunk(x, mem_k, mem_v, c0, n0, m0, h0, cb0, sb0, w):
    outs = {'c': [], 'n': [], 'm': [], 'h': [], 'cb': [], 'sb': []}
    ie = 0
    io = 0
    for l in range(DEPTH):
        u = rmsnorm(x, w['norm_mix'][l])
        if l % 2 == 0:
            y, st = even_mixer(u, w, ie, c0[ie], n0[ie], m0[ie], h0[ie], cb0[ie])
            for name, val in zip(('c', 'n', 'm', 'h', 'cb'), st):
                outs[name].append(val.astype(x.dtype))
            ie += 1
        else:
            y, sb = odd_mixer(u, w, io, sb0[io])
            outs['sb'].append(sb.astype(x.dtype))
            io += 1
        x = x + y
        x = x + cross_attn(rmsnorm(x, w['norm_x'][l]), mem_k[l], mem_v[l], w['w_xq'][l], w['w_xo'][l])
        x = x + mlp(rmsnorm(x, w['norm_ff'][l]), w['w_ff1'][l], w['w_ff2'][l])
    y = rmsnorm(x, w['norm_final'])
    return y, [jnp.stack(outs[name]) for name in ('c', 'n', 'm', 'h', 'cb', 'sb')]


def setup_inputs(seed: int = 0) -> dict:
    key = jax.random.key(seed)
    ks = iter(jax.random.split(key, 48))
    f32 = jnp.float32

    def nrm(shape, scale):
        return jax.random.normal(next(ks), shape, f32) * scale

    def gain(shape):
        return 1.0 + 0.02 * jax.random.normal(next(ks), shape, f32)

    lam_u = jax.random.uniform(next(ks), (N_EVEN, B_WIDTH), f32, 0.9, 0.999)
    lam_s = lam_u ** (1.0 / LRU_C)
    b_if = jnp.concatenate([nrm((N_EVEN, A_HEADS), 0.1), 3.0 + nrm((N_EVEN, A_HEADS), 0.5)], axis=-1)
    return {
        'x_prompt': nrm((BATCH, SEQ, D_MODEL), 1.0),
        'x_sample': nrm((DEC_BATCH, DEC_SEQ, D_MODEL), 1.0),
        'mem_prompt': nrm((BATCH, N_MEM, D_MODEL), 1.0),
        'cache_mem_k': nrm((DEPTH, DEC_BATCH, N_MEM, X_HEADS, X_HEAD_DIM), 1.0),
        'cache_mem_v': nrm((DEPTH, DEC_BATCH, N_MEM, X_HEADS, X_HEAD_DIM), 1.0),
        'state_mlstm_c': nrm((N_EVEN, DEC_BATCH, A_HEADS, A_DK, A_DV), 1.0),
        'state_mlstm_n': nrm((N_EVEN, DEC_BATCH, A_HEADS, A_DK), 1.0),
        'state_mlstm_m': nrm((N_EVEN, DEC_BATCH, A_HEADS), 1.0),
        'state_lru_h': nrm((N_EVEN, DEC_BATCH, B_WIDTH), 0.5),
        'state_lru_conv': nrm((N_EVEN, DEC_BATCH, B_CONV - 1, B_WIDTH), 1.0),
        'state_sconv': nrm((N_ODD, DEC_BATCH, C_CONV - 1, C_WIDTH), 1.0),
        'norm_mix': gain((DEPTH, D_MODEL)),
        'norm_x': gain((DEPTH, D_MODEL)),
        'norm_ff': gain((DEPTH, D_MODEL)),
        'norm_final': gain((D_MODEL,)),
        'w_in_e': nrm((N_EVEN, D_MODEL, EVEN_IN), D_MODEL ** -0.5),
        'b_if': b_if,
        'mh_norm': gain((N_EVEN, A_WIDTH)),
        'conv_b_w': nrm((N_EVEN, B_CONV, B_WIDTH), B_CONV ** -0.5),
        'conv_b_b': nrm((N_EVEN, B_WIDTH), 0.01),
        'lru_wa': nrm((N_EVEN, B_BLOCKS, B_BLOCK, B_BLOCK), B_BLOCK ** -0.5),
        'lru_ba': nrm((N_EVEN, B_WIDTH), 0.01),
        'lru_wi': nrm((N_EVEN, B_BLOCKS, B_BLOCK, B_BLOCK), B_BLOCK ** -0.5),
        'lru_bi': nrm((N_EVEN, B_WIDTH), 0.01),
        'lru_lam': jnp.log(lam_s) - jnp.log1p(-lam_s),
        'w_out_e': nrm((N_EVEN, A_WIDTH + B_WIDTH, D_MODEL), (A_WIDTH + B_WIDTH) ** -0.5),
        'w_in_o': nrm((N_ODD, D_MODEL, ODD_IN), D_MODEL ** -0.5),
        'conv_c_w': nrm((N_ODD, C_CONV, C_WIDTH), C_CONV ** -0.5),
        'w_out_o': nrm((N_ODD, C_WIDTH, D_MODEL), C_WIDTH ** -0.5),
        'w_xq': nrm((DEPTH, D_MODEL, D_MODEL), D_MODEL ** -0.5),
        'w_xk': nrm((DEPTH, D_MODEL, D_MODEL), D_MODEL ** -0.5),
        'w_xv': nrm((DEPTH, D_MODEL, D_MODEL), D_MODEL ** -0.5),
        'w_xo': nrm((DEPTH, D_MODEL, D_MODEL), D_MODEL ** -0.5),
        'w_ff1': nrm((DEPTH, D_MODEL, D_FF), D_MODEL ** -0.5),
        'w_ff2': nrm((DEPTH, D_FF, D_MODEL), D_FF ** -0.5),
    }


def reference(x_prompt, x_sample, mem_prompt, cache_mem_k, cache_mem_v, state_mlstm_c, state_mlstm_n, state_mlstm_m, state_lru_h, state_lru_conv, state_sconv, norm_mix, norm_x, norm_ff, norm_final, w_in_e, b_if, mh_norm, conv_b_w, conv_b_b, lru_wa, lru_ba, lru_wi, lru_bi, lru_lam, w_out_e, w_in_o, conv_c_w, w_out_o, w_xq, w_xk, w_xv, w_xo, w_ff1, w_ff2):
    w = dict(norm_mix=norm_mix, norm_x=norm_x, norm_ff=norm_ff, norm_final=norm_final, w_in_e=w_in_e, b_if=b_if, mh_norm=mh_norm, conv_b_w=conv_b_w, conv_b_b=conv_b_b, lru_wa=lru_wa, lru_ba=lru_ba, lru_wi=lru_wi, lru_bi=lru_bi, lru_lam=lru_lam, w_out_e=w_out_e, w_in_o=w_in_o, conv_c_w=conv_c_w, w_out_o=w_out_o, w_xq=w_xq, w_xo=w_xo, w_ff1=w_ff1, w_ff2=w_ff2)
    bp = x_prompt.shape[0]
    dt = x_prompt.dtype
    p_mem_k = jnp.stack([(mem_prompt @ w_xk[l]).reshape(bp, N_MEM, X_HEADS, X_HEAD_DIM) for l in range(DEPTH)])
    p_mem_v = jnp.stack([(mem_prompt @ w_xv[l]).reshape(bp, N_MEM, X_HEADS, X_HEAD_DIM) for l in range(DEPTH)])
    z_c = jnp.zeros((N_EVEN, bp, A_HEADS, A_DK, A_DV), dt)
    z_n = jnp.zeros((N_EVEN, bp, A_HEADS, A_DK), dt)
    z_m = jnp.zeros((N_EVEN, bp, A_HEADS), dt)
    z_h = jnp.zeros((N_EVEN, bp, B_WIDTH), dt)
    z_cb = jnp.zeros((N_EVEN, bp, B_CONV - 1, B_WIDTH), dt)
    z_sb = jnp.zeros((N_ODD, bp, C_CONV - 1, C_WIDTH), dt)
    y_prompt, (p_c, p_n, p_m, p_h, p_cb, p_sb) = trunk(x_prompt, p_mem_k, p_mem_v, z_c, z_n, z_m, z_h, z_cb, z_sb, w)
    y_sample, (s_c, s_n, s_m, s_h, s_cb, s_sb) = trunk(x_sample, cache_mem_k, cache_mem_v, state_mlstm_c, state_mlstm_n, state_mlstm_m, state_lru_h, state_lru_conv, state_sconv, w)
    return (y_prompt, y_sample, p_c, p_n, p_m, p_h, p_cb, p_sb, p_mem_k, p_mem_v, s_c, s_n, s_m, s_h, s_cb, s_sb)
```

```python
import functools

import jax
import jax.numpy as jnp
from jax import lax
from jax.experimental import pallas as pl
from jax.experimental.pallas import tpu as pltpu

F32 = jnp.float32
BF16 = jnp.bfloat16
EPS = 1e-6
LRU_C = 8.0

V7X_LANES = 128
V7X_SUBLANES = 8
V7X_VMEM_BUDGET_BYTES = 56 * 1024 * 1024

MLSTM_CHUNK = 128
SCAN_CHUNK = 256


def _cparams(*semantics):
    return pltpu.CompilerParams(dimension_semantics=semantics, vmem_limit_bytes=V7X_VMEM_BUDGET_BYTES)


def _log_sigmoid(x):
    return jnp.minimum(x, 0.0) - jnp.log1p(jnp.exp(-jnp.abs(x)))


def _sigmoid(x):
    return 1.0 / (1.0 + jnp.exp(-x))


def _gelu_tanh(x):
    c = 0.7978845608028654
    return 0.5 * x * (1.0 + jnp.tanh(c * (x + 0.044715 * (x * x * x))))


def _expm1(x):
    u = jnp.exp(x)
    um1 = u - 1.0
    safe = jnp.where(um1 == 0.0, 1.0, jnp.log(u))
    return jnp.where(um1 == 0.0, x, jnp.where(um1 == -1.0, -1.0, um1 * x / safe))


def _split3(x):
    hi = x.astype(BF16)
    r1 = x - hi.astype(F32)
    mid = r1.astype(BF16)
    lo = (r1 - mid.astype(F32)).astype(BF16)
    return hi, mid, lo


def _dot01(mask_bf16, x):
    hi, mid, lo = _split3(x)
    acc = jnp.dot(mask_bf16, lo, preferred_element_type=F32)
    acc = acc + jnp.dot(mask_bf16, mid, preferred_element_type=F32)
    return acc + jnp.dot(mask_bf16, hi, preferred_element_type=F32)


def _rmsnorm_body(x_ref, g_ref, o_ref):
    x = x_ref[...]
    inv = lax.rsqrt(jnp.mean(x * x, axis=-1, keepdims=True) + EPS)
    o_ref[...] = ((x * inv) * g_ref[...]).astype(o_ref.dtype)


def rmsnorm(x, g, out_dtype=BF16, rows=256):
    t, d = x.shape
    return pl.pallas_call(
        _rmsnorm_body,
        grid=(t // rows,),
        in_specs=[pl.BlockSpec((rows, d), lambda i: (i, 0)), pl.BlockSpec((1, d), lambda i: (0, 0))],
        out_specs=pl.BlockSpec((rows, d), lambda i: (i, 0)),
        out_shape=jax.ShapeDtypeStruct((t, d), out_dtype),
        compiler_params=_cparams("parallel"),
        name="rmsnorm",
    )(x, g.reshape(1, d))


def _mm_body(*refs, n_in, has_res, act):
    o_ref = refs[-1]
    acc = jnp.dot(refs[0][...], refs[n_in][...], preferred_element_type=F32)
    for k in range(1, n_in):
        acc = acc + jnp.dot(refs[k][...], refs[n_in + k][...], preferred_element_type=F32)
    if act == "relu2":
        r = jnp.maximum(acc, 0.0)
        acc = r * r
    if has_res:
        acc = refs[2 * n_in][...] + acc
    o_ref[...] = acc.astype(o_ref.dtype)


def matmul(xs, ws, *, res=None, act=None, out_dtype=F32, tm, tn, name):
    m = xs[0].shape[0]
    n = ws[0].shape[1]
    n_in = len(xs)
    in_specs = [pl.BlockSpec((tm, x.shape[1]), lambda j, i: (i, 0)) for x in xs]
    in_specs += [pl.BlockSpec((w.shape[0], tn), lambda j, i: (0, j)) for w in ws]
    args = list(xs) + list(ws)
    if res is not None:
        in_specs.append(pl.BlockSpec((tm, tn), lambda j, i: (i, j)))
        args.append(res)
    return pl.pallas_call(
        functools.partial(_mm_body, n_in=n_in, has_res=res is not None, act=act),
        grid=(n // tn, m // tm),
        in_specs=in_specs,
        out_specs=pl.BlockSpec((tm, tn), lambda j, i: (i, j)),
        out_shape=jax.ShapeDtypeStruct((m, n), out_dtype),
        compiler_params=_cparams("parallel", "parallel"),
        name=name,
    )(*args)


def _mm_acc_body(x_ref, w_ref, res_ref, o_ref, acc_ref):
    k = pl.program_id(2)

    @pl.when(k == 0)
    def _():
        acc_ref[...] = res_ref[...]

    acc_ref[...] += jnp.dot(x_ref[...], w_ref[...], preferred_element_type=F32)

    @pl.when(k == pl.num_programs(2) - 1)
    def _():
        o_ref[...] = acc_ref[...]


def matmul_ksplit(x, w, res, *, tm, tn, tk, name):
    m, kk = x.shape
    n = w.shape[1]
    return pl.pallas_call(
        _mm_acc_body,
        grid=(m // tm, n // tn, kk // tk),
        in_specs=[
            pl.BlockSpec((tm, tk), lambda i, j, k: (i, k)),
            pl.BlockSpec((tk, tn), lambda i, j, k: (k, j)),
            pl.BlockSpec((tm, tn), lambda i, j, k: (i, j)),
        ],
        out_specs=pl.BlockSpec((tm, tn), lambda i, j, k: (i, j)),
        out_shape=jax.ShapeDtypeStruct((m, n), F32),
        scratch_shapes=[pltpu.VMEM((tm, tn), F32)],
        compiler_params=_cparams("parallel", "parallel", "arbitrary"),
        name=name,
    )(x, w, res)


def _gates_body(u_ref, w_ref, b_ref, gc_ref, gr_ref, *, tm, n_prompt_tiles, seg_sample):
    L = MLSTM_CHUNK
    g = jnp.dot(u_ref[...], w_ref[...], preferred_element_type=F32) + b_ref[...]
    lane = lax.broadcasted_iota(jnp.int32, (tm, V7X_LANES), 1)
    val = jnp.where(lane < 8, g, _log_sigmoid(g))
    row = lax.broadcasted_iota(jnp.int32, (L, L), 0)
    col = lax.broadcasted_iota(jnp.int32, (L, L), 1)
    is_sample = pl.program_id(0) >= n_prompt_tiles
    tril = jnp.where(col <= row, 1.0, 0.0)
    tril_seg = jnp.where((col <= row) & ((row // seg_sample) == (col // seg_sample)), 1.0, 0.0)
    mask = jnp.where(is_sample, tril_seg, tril).astype(BF16)
    lane_l = lax.broadcasted_iota(jnp.int32, (L, V7X_LANES), 1)
    for s in range(tm // L):
        x = val[s * L:(s + 1) * L]
        out = jnp.where(lane_l < 8, x, _dot01(mask, x))
        gr_ref[:, s * L:(s + 1) * L] = out.T[:16]
        for c in range(16):
            gc_ref[c, s * L:(s + 1) * L, :] = jnp.broadcast_to(out[:, c:c + 1], (L, V7X_LANES))


def mlstm_gates(u, w_if, b_if, *, n_prompt_rows, seg_sample, tm=512):
    t, d = u.shape
    return pl.pallas_call(
        functools.partial(_gates_body, tm=tm, n_prompt_tiles=n_prompt_rows // tm, seg_sample=seg_sample),
        grid=(t // tm,),
        in_specs=[
            pl.BlockSpec((tm, d), lambda i: (i, 0)),
            pl.BlockSpec((d, V7X_LANES), lambda i: (0, 0)),
            pl.BlockSpec((1, V7X_LANES), lambda i: (0, 0)),
        ],
        out_specs=[
            pl.BlockSpec((16, tm, V7X_LANES), lambda i: (0, i, 0)),
            pl.BlockSpec((16, tm), lambda i: (0, i)),
        ],
        out_shape=[
            jax.ShapeDtypeStruct((16, t, V7X_LANES), F32),
            jax.ShapeDtypeStruct((16, t), F32),
        ],
        compiler_params=_cparams("parallel"),
        name="mlstm_gates",
    )(u, w_if, b_if)


def _head_out(hh, o, mh):
    hn = hh * lax.rsqrt(jnp.mean(hh * hh, axis=-1, keepdims=True) + EPS)
    return _sigmoid(o) * (hn * mh)


def _mlstm_prompt_body(q_ref, k_ref, v_ref, o_ref, ic_ref, bc_ref, ir_ref, br_ref, mh_ref,
                       ya_ref, c_ref, n_ref, m_ref, c_scr, n_scr, m_scr, *, n_heads, seq, scale):
    L = MLSTM_CHUNK
    h = pl.program_id(1)
    c_scr[...] = jnp.zeros_like(c_scr)
    n_scr[...] = jnp.zeros_like(n_scr)
    m_scr[...] = jnp.zeros_like(m_scr)
    row = lax.broadcasted_iota(jnp.int32, (L, L), 0)
    col = lax.broadcasted_iota(jnp.int32, (L, L), 1)
    causal = col <= row

    def chunk(ci, carry):
        r0 = pl.multiple_of(ci * L, L)
        q = q_ref[pl.ds(r0, L), :] * scale
        k = k_ref[pl.ds(r0, L), :]
        v = v_ref[pl.ds(r0, L), :]
        bc = bc_ref[0, pl.ds(r0, L), :]
        ic = ic_ref[0, pl.ds(r0, L), :]
        i_row = ir_ref[0, :, pl.ds(r0, L)]
        b_row = br_ref[0, :, pl.ds(r0, L)]
        m_prev = m_scr[...]
        c_prev = c_scr[...]
        n_prev = n_scr[...]
        dmat = jnp.where(causal, bc - b_row + i_row, -jnp.inf)
        g = bc + m_prev
        mt = jnp.maximum(g, jnp.max(dmat, axis=-1, keepdims=True))
        w_inter = jnp.exp(g - mt)
        qb = q.astype(BF16)
        kb = k.astype(BF16)
        vb = v.astype(BF16)
        sc = lax.dot_general(qb, kb, (((1,), (1,)), ((), ())), preferred_element_type=F32) * jnp.exp(dmat - mt)
        inter = jnp.dot(qb, c_prev.astype(BF16), preferred_element_type=F32)
        num = jnp.concatenate([w_inter, w_inter], axis=1) * inter + jnp.dot(sc.astype(BF16), vb, preferred_element_type=F32)
        qn = jnp.sum(q * n_prev, axis=-1, keepdims=True)
        den = w_inter[:, :1] * qn + jnp.sum(sc, axis=-1, keepdims=True)
        hh = num / jnp.maximum(jnp.abs(den), jnp.exp(-mt[:, :1]))
        ya_ref[pl.ds(r0, L), :] = _head_out(hh, o_ref[pl.ds(r0, L), :], mh_ref[...]).astype(ya_ref.dtype)
        m_new = mt[L - 1:L, :1]
        b_last = b_row[:, L - 1:L]
        decay = jnp.exp(b_last + m_prev - m_new)
        kw = k * jnp.exp(b_last - bc + ic - m_new)
        c_scr[...] = decay * c_prev + lax.dot_general(kw.astype(BF16), vb, (((0,), (0,)), ((), ())), preferred_element_type=F32)
        n_scr[...] = decay * n_prev + jnp.sum(kw, axis=0, keepdims=True)
        m_scr[...] = m_new
        return carry

    lax.fori_loop(0, seq // L, chunk, 0)
    c_ref[0, 0] = c_scr[...]
    n_ref[0, 0] = n_scr[...]
    m_ref[0, 0] = m_scr[...]


def mlstm_prompt(proj, gc, gr, mh_norm, *, batch, seq, n_heads, dk, dv, col_q, col_k, col_v, col_o):
    L = MLSTM_CHUNK
    assert dk == L and dv == 2 * L
    rows = batch * seq
    return pl.pallas_call(
        functools.partial(_mlstm_prompt_body, n_heads=n_heads, seq=seq, scale=dk ** -0.5),
        grid=(batch, n_heads),
        in_specs=[
            pl.BlockSpec((seq, dk), lambda b, h: (b, col_q // dk + h)),
            pl.BlockSpec((seq, dk), lambda b, h: (b, col_k // dk + h)),
            pl.BlockSpec((seq, dv), lambda b, h: (b, col_v // dv + h)),
            pl.BlockSpec((seq, dv), lambda b, h: (b, col_o // dv + h)),
            pl.BlockSpec((1, seq, V7X_LANES), lambda b, h: (h, b, 0)),
            pl.BlockSpec((1, seq, V7X_LANES), lambda b, h: (h + n_heads, b, 0)),
            pl.BlockSpec((1, 1, seq), lambda b, h: (h, 0, b)),
            pl.BlockSpec((1, 1, seq), lambda b, h: (h + n_heads, 0, b)),
            pl.BlockSpec((1, dv), lambda b, h: (0, h)),
        ],
        out_specs=[
            pl.BlockSpec((seq, dv), lambda b, h: (b, h)),
            pl.BlockSpec((1, 1, dk, dv), lambda b, h: (b, h, 0, 0)),
            pl.BlockSpec((1, 1, 1, dk), lambda b, h: (b, h, 0, 0)),
            pl.BlockSpec((1, 1, 1, 1), lambda b, h: (b, h, 0, 0)),
        ],
        out_shape=[
            jax.ShapeDtypeStruct((rows, n_heads * dv), BF16),
            jax.ShapeDtypeStruct((batch, n_heads, dk, dv), F32),
            jax.ShapeDtypeStruct((batch, n_heads, 1, dk), F32),
            jax.ShapeDtypeStruct((batch, n_heads, 1, 1), F32),
        ],
        scratch_shapes=[pltpu.VMEM((dk, dv), F32), pltpu.VMEM((1, dk), F32), pltpu.VMEM((1, 1), F32)],
        compiler_params=_cparams("parallel", "parallel"),
        name="mlstm_prompt",
    )(proj, proj, proj, proj, gc, gc, gr, gr, mh_norm)


def _mlstm_sample_body(q_ref, k_ref, v_ref, o_ref, ic_ref, bc_ref, ir_ref, br_ref, mh_ref, c0_ref, n0_ref, m0_ref,
                       ya_ref, c_ref, n_ref, m_ref, inter_scr, *, n_heads, seg, scale):
    L = MLSTM_CHUNK
    nb = L // seg
    h = pl.program_id(1)
    row = lax.broadcasted_iota(jnp.int32, (L, L), 0)
    col = lax.broadcasted_iota(jnp.int32, (L, L), 1)
    same = (row // seg) == (col // seg)
    causal = (col <= row) & same
    last = col == (row // seg) * seg + (seg - 1)
    q = q_ref[...] * scale
    k = k_ref[...]
    v = v_ref[...]
    bc = bc_ref[0]
    ic = ic_ref[0]
    i_row = ir_ref[0]
    b_row = br_ref[0]
    m_prev = m0_ref[0]
    n_prev = n0_ref[0]
    dmat = jnp.where(causal, bc - b_row + i_row, -jnp.inf)
    g = bc + m_prev
    mt = jnp.maximum(g, jnp.max(dmat, axis=-1, keepdims=True))
    w_inter = jnp.exp(g - mt)
    qb = q.astype(BF16)
    kb = k.astype(BF16)
    vb = v.astype(BF16)
    sc = lax.dot_general(qb, kb, (((1,), (1,)), ((), ())), preferred_element_type=F32) * jnp.exp(dmat - mt)

    per_group = V7X_SUBLANES // seg
    sub = lax.broadcasted_iota(jnp.int32, (V7X_SUBLANES, 2 * L), 0)
    for gi in range(L // V7X_SUBLANES):
        q8 = qb[gi * V7X_SUBLANES:(gi + 1) * V7X_SUBLANES]
        acc = jnp.zeros((V7X_SUBLANES, 2 * L), F32)
        for j in range(per_group):
            r = jnp.dot(q8, c0_ref[0, gi * per_group + j, 0].astype(BF16), preferred_element_type=F32)
            acc = jnp.where((sub // seg) == j, r, acc)
        inter_scr[gi * V7X_SUBLANES:(gi + 1) * V7X_SUBLANES, :] = acc
    inter = inter_scr[...]

    num = jnp.concatenate([w_inter, w_inter], axis=1) * inter + jnp.dot(sc.astype(BF16), vb, preferred_element_type=F32)
    qn = jnp.sum(q * n_prev, axis=-1, keepdims=True)
    den = w_inter[:, :1] * qn + jnp.sum(sc, axis=-1, keepdims=True)
    hh = num / jnp.maximum(jnp.abs(den), jnp.exp(-mt[:, :1]))
    ya_ref[...] = _head_out(hh, o_ref[...], mh_ref[...]).astype(ya_ref.dtype)

    b_last = jnp.sum(jnp.where(last, jnp.broadcast_to(b_row, (L, L)), 0.0), axis=-1, keepdims=True)
    mt_row = mt.T
    m_new = jnp.sum(jnp.where(last, mt_row, 0.0), axis=-1, keepdims=True)
    decay = jnp.exp(b_last + m_prev - m_new)
    kw = k * jnp.exp(b_last - bc + ic - m_new)
    seg_ones = jnp.where(same, 1.0, 0.0).astype(BF16)
    n_ref[0] = decay * n_prev + _dot01(seg_ones, kw)
    m_ref[0] = jnp.broadcast_to(m_new, (L, L))
    rowk = lax.broadcasted_iota(jnp.int32, (L, L), 0)
    for bi in range(nb):
        kw_b = jnp.where((rowk // seg) == bi, kw, 0.0).astype(BF16)
        upd = lax.dot_general(kw_b, vb, (((0,), (0,)), ((), ())), preferred_element_type=F32)
        d_b = decay[bi * seg:bi * seg + 1, :1]
        c_ref[0, bi, 0] = d_b * c0_ref[0, bi, 0] + upd


def mlstm_sample(proj, gc, gr, mh_norm, c0, n0_tok, m0_tok, *, row0, n_seq, seg, n_heads, dk, dv,
                 col_q, col_k, col_v, col_o):
    L = MLSTM_CHUNK
    assert dk == L and dv == 2 * L and V7X_SUBLANES % seg == 0 and row0 % L == 0
    rows = n_seq * seg
    nb = L // seg
    t0 = row0 // L
    return pl.pallas_call(
        functools.partial(_mlstm_sample_body, n_heads=n_heads, seg=seg, scale=dk ** -0.5),
        grid=(rows // L, n_heads),
        in_specs=[
            pl.BlockSpec((L, dk), lambda j, h: (t0 + j, col_q // dk + h)),
            pl.BlockSpec((L, dk), lambda j, h: (t0 + j, col_k // dk + h)),
            pl.BlockSpec((L, dv), lambda j, h: (t0 + j, col_v // dv + h)),
            pl.BlockSpec((L, dv), lambda j, h: (t0 + j, col_o // dv + h)),
            pl.BlockSpec((1, L, V7X_LANES), lambda j, h: (h, t0 + j, 0)),
            pl.BlockSpec((1, L, V7X_LANES), lambda j, h: (h + n_heads, t0 + j, 0)),
            pl.BlockSpec((1, 1, L), lambda j, h: (h, 0, t0 + j)),
            pl.BlockSpec((1, 1, L), lambda j, h: (h + n_heads, 0, t0 + j)),
            pl.BlockSpec((1, dv), lambda j, h: (0, h)),
            pl.BlockSpec((1, nb, 1, dk, dv), lambda j, h: (0, j, h, 0, 0)),
            pl.BlockSpec((1, L, dk), lambda j, h: (h, j, 0)),
            pl.BlockSpec((1, L, L), lambda j, h: (h, j, 0)),
        ],
        out_specs=[
            pl.BlockSpec((L, dv), lambda j, h: (j, h)),
            pl.BlockSpec((1, nb, 1, dk, dv), lambda j, h: (0, j, h, 0, 0)),
            pl.BlockSpec((1, L, dk), lambda j, h: (h, j, 0)),
            pl.BlockSpec((1, L, L), lambda j, h: (h, j, 0)),
        ],
        out_shape=[
            jax.ShapeDtypeStruct((rows, n_heads * dv), BF16),
            jax.ShapeDtypeStruct((1, n_seq, n_heads, dk, dv), F32),
            jax.ShapeDtypeStruct((n_heads, rows, dk), F32),
            jax.ShapeDtypeStruct((n_heads, rows, L), F32),
        ],
        scratch_shapes=[pltpu.VMEM((L, dv), F32)],
        compiler_params=_cparams("parallel", "parallel"),
        name="mlstm_sample",
    )(proj, proj, proj, proj, gc, gc, gr, gr, mh_norm, c0, n0_tok, m0_tok)


def _lru_gates(xc, wa_ref, wi_ref, ba, bi, lam):
    nblk = wa_ref.shape[0]
    blk = wa_ref.shape[1]
    rs, igs = [], []
    for n in range(nblk):
        xb = xc[:, n * blk:(n + 1) * blk].astype(BF16)
        rs.append(jnp.dot(xb, wa_ref[n], preferred_element_type=F32))
        igs.append(jnp.dot(xb, wi_ref[n], preferred_element_type=F32))
    r = _sigmoid(jnp.concatenate(rs, axis=1) + ba)
    ig = _sigmoid(jnp.concatenate(igs, axis=1) + bi)
    log_a = LRU_C * r * _log_sigmoid(lam)
    a = jnp.exp(log_a)
    u = jnp.sqrt(-_expm1(2.0 * log_a)) * (ig * xc)
    return a, u


def _lru_prompt_body(x_ref, g_ref, halo_ref, h0_ref, cw_ref, cb_ref, wa_ref, wi_ref, ba_ref, bi_ref, lam_ref,
                     y_ref, hl_ref, h_scr, *, seq):
    Lc = SCAN_CHUNK
    cbw = x_ref.shape[1]
    h_scr[...] = h0_ref[0]
    row = lax.broadcasted_iota(jnp.int32, (Lc, cbw), 0)
    cw = cw_ref[...]
    width = cw.shape[0]

    def chunk(ci, carry):
        r0 = pl.multiple_of(ci * Lc, Lc)
        x = x_ref[pl.ds(r0, Lc), :]
        prev = x_ref[pl.ds(pl.multiple_of(jnp.maximum(r0 - V7X_SUBLANES, 0), V7X_SUBLANES), V7X_SUBLANES), :]
        prev = jnp.where(ci == 0, halo_ref[0], prev)
        xfull = jnp.concatenate([prev, x], axis=0)
        xc = None
        for j in range(width - 1, 0, -1):
            term = pltpu.roll(xfull, j, 0)[V7X_SUBLANES:] * cw[width - 1 - j:width - j]
            xc = term if xc is None else xc + term
        xc = xc + x * cw[width - 1:width] + cb_ref[...]
        a, u = _lru_gates(xc, wa_ref, wi_ref, ba_ref[...], bi_ref[...], lam_ref[...])
        d = 1
        while d < Lc:
            ok = row >= d
            a_sh = jnp.where(ok, pltpu.roll(a, d, 0), 1.0)
            u_sh = jnp.where(ok, pltpu.roll(u, d, 0), 0.0)
            u = a * u_sh + u
            a = a * a_sh
            d *= 2
        hs = u + a * h_scr[...]
        h_scr[...] = hs[Lc - 1:Lc]
        y_ref[pl.ds(r0, Lc), :] = (hs * _gelu_tanh(g_ref[pl.ds(r0, Lc), :])).astype(y_ref.dtype)
        return carry

    lax.fori_loop(0, seq // Lc, chunk, 0)
    hl_ref[0] = h_scr[...]


def lru_prompt(proj, halo, h0, cw, cb, wa, wi, ba, bi, lam, *, batch, seq, width, col_x, col_g, cbw=256):
    blk = wa.shape[1]
    nper = cbw // blk
    vec = lambda b, c: (0, c)
    return pl.pallas_call(
        functools.partial(_lru_prompt_body, seq=seq),
        grid=(batch, width // cbw),
        in_specs=[
            pl.BlockSpec((seq, cbw), lambda b, c: (b, col_x // cbw + c)),
            pl.BlockSpec((seq, cbw), lambda b, c: (b, col_g // cbw + c)),
            pl.BlockSpec((1, V7X_SUBLANES, cbw), lambda b, c: (b, 0, c)),
            pl.BlockSpec((1, 1, cbw), lambda b, c: (b, 0, c)),
            pl.BlockSpec((cw.shape[0], cbw), vec),
            pl.BlockSpec((1, cbw), vec),
            pl.BlockSpec((nper, blk, blk), lambda b, c: (c, 0, 0)),
            pl.BlockSpec((nper, blk, blk), lambda b, c: (c, 0, 0)),
            pl.BlockSpec((1, cbw), vec),
            pl.BlockSpec((1, cbw), vec),
            pl.BlockSpec((1, cbw), vec),
        ],
        out_specs=[
            pl.BlockSpec((seq, cbw), lambda b, c: (b, c)),
            pl.BlockSpec((1, 1, cbw), lambda b, c: (b, 0, c)),
        ],
        out_shape=[
            jax.ShapeDtypeStruct((batch * seq, width), BF16),
            jax.ShapeDtypeStruct((batch, 1, width), F32),
        ],
        scratch_shapes=[pltpu.VMEM((1, cbw), F32)],
        compiler_params=_cparams("parallel", "parallel"),
        name="lru_prompt",
    )(proj, proj, halo, h0, cw, cb, wa, wi, ba, bi, lam)


def _lru_sample_body(x_ref, g_ref, st_ref, h0_ref, cw_ref, cb_ref, wa_ref, wi_ref, ba_ref, bi_ref, lam_ref,
                     y_ref, hl_ref, *, seg):
    cw = cw_ref[...]
    width = cw.shape[0]
    hist = [st_ref[j] for j in range(width - 1)]
    h = h0_ref[...]
    for t in range(seg):
        x = x_ref[t]
        taps = hist + [x]
        xc = taps[0] * cw[0:1]
        for j in range(1, width):
            xc = xc + taps[j] * cw[j:j + 1]
        xc = xc + cb_ref[...]
        a, u = _lru_gates(xc, wa_ref, wi_ref, ba_ref[...], bi_ref[...], lam_ref[...])
        h = a * h + u
        y_ref[t] = (h * _gelu_tanh(g_ref[t])).astype(y_ref.dtype)
        hist = hist[1:] + [x]
    hl_ref[...] = h


def lru_sample(x_t, g_t, st, h0, cw, cb, wa, wi, ba, bi, lam, *, cbw=256):
    seg, n_seq, width = x_t.shape
    blk = wa.shape[1]
    nper = cbw // blk
    vec = lambda c: (0, c)
    return pl.pallas_call(
        functools.partial(_lru_sample_body, seg=seg),
        grid=(width // cbw,),
        in_specs=[
            pl.BlockSpec((seg, n_seq, cbw), lambda c: (0, 0, c)),
            pl.BlockSpec((seg, n_seq, cbw), lambda c: (0, 0, c)),
            pl.BlockSpec((st.shape[0], n_seq, cbw), lambda c: (0, 0, c)),
            pl.BlockSpec((n_seq, cbw), vec),
            pl.BlockSpec((cw.shape[0], cbw), vec),
            pl.BlockSpec((1, cbw), vec),
            pl.BlockSpec((nper, blk, blk), lambda c: (c, 0, 0)),
            pl.BlockSpec((nper, blk, blk), lambda c: (c, 0, 0)),
            pl.BlockSpec((1, cbw), vec),
            pl.BlockSpec((1, cbw), vec),
            pl.BlockSpec((1, cbw), vec),
        ],
        out_specs=[
            pl.BlockSpec((seg, n_seq, cbw), lambda c: (0, 0, c)),
            pl.BlockSpec((n_seq, cbw), vec),
        ],
        out_shape=[
            jax.ShapeDtypeStruct((seg, n_seq, width), BF16),
            jax.ShapeDtypeStruct((n_seq, width), F32),
        ],
        compiler_params=_cparams("parallel"),
        name="lru_sample",
    )(x_t, g_t, st, h0, cw, cb, wa, wi, ba, bi, lam)


def _sconv_prompt_body(bg_ref, cg_ref, hx_ref, halo_ref, cw_ref, y_ref, tail_ref, *, seq):
    Lc = SCAN_CHUNK
    cw = cw_ref[...]
    width = cw.shape[0]

    def chunk(ci, carry):
        r0 = pl.multiple_of(ci * Lc, Lc)
        p = cg_ref[pl.ds(r0, Lc), :] * hx_ref[pl.ds(r0, Lc), :]
        rp = pl.multiple_of(jnp.maximum(r0 - V7X_SUBLANES, 0), V7X_SUBLANES)
        prev = cg_ref[pl.ds(rp, V7X_SUBLANES), :] * hx_ref[pl.ds(rp, V7X_SUBLANES), :]
        prev = jnp.where(ci == 0, halo_ref[0], prev)
        pfull = jnp.concatenate([prev, p], axis=0)
        z = None
        for j in range(width - 1, 0, -1):
            term = pltpu.roll(pfull, j, 0)[V7X_SUBLANES:] * cw[width - 1 - j:width - j]
            z = term if z is None else z + term
        z = z + p * cw[width - 1:width]
        y_ref[pl.ds(r0, Lc), :] = (bg_ref[pl.ds(r0, Lc), :] * z).astype(y_ref.dtype)
        return carry

    lax.fori_loop(0, seq // Lc, chunk, 0)
    rt = seq - V7X_SUBLANES
    tail_ref[0] = cg_ref[pl.ds(rt, V7X_SUBLANES), :] * hx_ref[pl.ds(rt, V7X_SUBLANES), :]


def sconv_prompt(proj, halo, cw, *, batch, seq, width, cbw=512):
    nb = width // cbw
    return pl.pallas_call(
        functools.partial(_sconv_prompt_body, seq=seq),
        grid=(batch, nb),
        in_specs=[
            pl.BlockSpec((seq, cbw), lambda b, c: (b, c)),
            pl.BlockSpec((seq, cbw), lambda b, c: (b, nb + c)),
            pl.BlockSpec((seq, cbw), lambda b, c: (b, 2 * nb + c)),
            pl.BlockSpec((1, V7X_SUBLANES, cbw), lambda b, c: (b, 0, c)),
            pl.BlockSpec((cw.shape[0], cbw), lambda b, c: (0, c)),
        ],
        out_specs=[
            pl.BlockSpec((seq, cbw), lambda b, c: (b, c)),
            pl.BlockSpec((1, V7X_SUBLANES, cbw), lambda b, c: (b, 0, c)),
        ],
        out_shape=[
            jax.ShapeDtypeStruct((batch * seq, width), BF16),
            jax.ShapeDtypeStruct((batch, V7X_SUBLANES, width), F32),
        ],
        compiler_params=_cparams("parallel", "parallel"),
        name="sconv_prompt",
    )(proj, proj, proj, halo, cw)


def _sconv_sample_body(bg_ref, cg_ref, hx_ref, st_ref, cw_ref, y_ref, ps_ref, *, seg):
    cw = cw_ref[...]
    width = cw.shape[0]
    hist = [st_ref[j] for j in range(width - 1)]
    for t in range(seg):
        p = cg_ref[t] * hx_ref[t]
        ps_ref[t] = p
        taps = hist + [p]
        z = taps[0] * cw[0:1]
        for j in range(1, width):
            z = z + taps[j] * cw[j:j + 1]
        y_ref[t] = (bg_ref[t] * z).astype(y_ref.dtype)
        hist = hist[1:] + [p]


def sconv_sample(proj_t, st, cw, *, width, cbw=512):
    seg, n_seq, _ = proj_t.shape
    nb = width // cbw
    return pl.pallas_call(
        functools.partial(_sconv_sample_body, seg=seg),
        grid=(nb,),
        in_specs=[
            pl.BlockSpec((seg, n_seq, cbw), lambda c: (0, 0, c)),
            pl.BlockSpec((seg, n_seq, cbw), lambda c: (0, 0, nb + c)),
            pl.BlockSpec((seg, n_seq, cbw), lambda c: (0, 0, 2 * nb + c)),
            pl.BlockSpec((st.shape[0], n_seq, cbw), lambda c: (0, 0, c)),
            pl.BlockSpec((cw.shape[0], cbw), lambda c: (0, c)),
        ],
        out_specs=[
            pl.BlockSpec((seg, n_seq, cbw), lambda c: (0, 0, c)),
            pl.BlockSpec((seg, n_seq, cbw), lambda c: (0, 0, c)),
        ],
        out_shape=[
            jax.ShapeDtypeStruct((seg, n_seq, width), BF16),
            jax.ShapeDtypeStruct((seg, n_seq, width), F32),
        ],
        compiler_params=_cparams("parallel"),
        name="sconv_sample",
    )(proj_t, proj_t, proj_t, st, cw)


def _softmax_rows(s):
    e = jnp.exp(s - jnp.max(s, axis=-1, keepdims=True))
    return e / jnp.sum(e, axis=-1, keepdims=True)


def _xattn_prompt_body(q_ref, k_ref, v_ref, o_ref, *, scale):
    s = lax.dot_general(q_ref[...], k_ref[...].astype(BF16), (((1,), (1,)), ((), ())), preferred_element_type=F32) * scale
    p = _softmax_rows(s).astype(BF16)
    o_ref[...] = jnp.dot(p, v_ref[...].astype(BF16), preferred_element_type=F32).astype(o_ref.dtype)


def xattn_prompt(q, mk, mv, *, batch, seq, n_mem, n_heads, hd, tq=1024):
    nq = seq // tq
    return pl.pallas_call(
        functools.partial(_xattn_prompt_body, scale=hd ** -0.5),
        grid=(batch, n_heads, nq),
        in_specs=[
            pl.BlockSpec((tq, hd), lambda b, h, i: (b * nq + i, h)),
            pl.BlockSpec((n_mem, hd), lambda b, h, i: (b, h)),
            pl.BlockSpec((n_mem, hd), lambda b, h, i: (b, h)),
        ],
        out_specs=pl.BlockSpec((tq, hd), lambda b, h, i: (b * nq + i, h)),
        out_shape=jax.ShapeDtypeStruct((batch * seq, n_heads * hd), BF16),
        compiler_params=_cparams("parallel", "parallel", "parallel"),
        name="xattn_prompt",
    )(q, mk, mv)


def _xattn_sample_body(q_ref, k_ref, v_ref, o_ref, *, n_heads, hd, scale):
    for h in range(n_heads):
        cs = slice(h * hd, (h + 1) * hd)
        s = lax.dot_general(q_ref[0, :, cs], k_ref[0, :, cs].astype(BF16), (((1,), (1,)), ((), ())),
                            preferred_element_type=F32) * scale
        p = _softmax_rows(s).astype(BF16)
        o_ref[0, :, cs] = jnp.dot(p, v_ref[0, :, cs].astype(BF16), preferred_element_type=F32).astype(o_ref.dtype)


def xattn_sample(q3, mk, mv, *, n_heads, hd):
    n_seq, seg, d = q3.shape
    n_mem = mk.shape[1]
    return pl.pallas_call(
        functools.partial(_xattn_sample_body, n_heads=n_heads, hd=hd, scale=hd ** -0.5),
        grid=(n_seq,),
        in_specs=[
            pl.BlockSpec((1, seg, d), lambda b: (b, 0, 0)),
            pl.BlockSpec((1, n_mem, d), lambda b: (b, 0, 0)),
            pl.BlockSpec((1, n_mem, d), lambda b: (b, 0, 0)),
        ],
        out_specs=pl.BlockSpec((1, seg, d), lambda b: (b, 0, 0)),
        out_shape=jax.ShapeDtypeStruct((n_seq, seg, d), BF16),
        compiler_params=_cparams("parallel"),
        name="xattn_sample",
    )(q3, mk, mv)


def kernel(x_prompt, x_sample, mem_prompt, cache_mem_k, cache_mem_v, state_mlstm_c, state_mlstm_n, state_mlstm_m, state_lru_h, state_lru_conv, state_sconv, norm_mix, norm_x, norm_ff, norm_final, w_in_e, b_if, mh_norm, conv_b_w, conv_b_b, lru_wa, lru_ba, lru_wi, lru_bi, lru_lam, w_out_e, w_in_o, conv_c_w, w_out_o, w_xq, w_xk, w_xv, w_xo, w_ff1, w_ff2):
    bp, sp, d = x_prompt.shape
    bs, ss, _ = x_sample.shape
    n_mem = mem_prompt.shape[1]
    depth, _, _, xh, xhd = cache_mem_k.shape
    _, _, ah, dk, dv = state_mlstm_c.shape
    bw = state_lru_h.shape[-1]
    bconv = state_lru_conv.shape[2] + 1
    cconv = state_sconv.shape[2] + 1
    cwid = state_sconv.shape[-1]
    aw = ah * dv
    aqk = ah * dk
    tp = bp * sp
    ts = bs * ss
    t = tp + ts
    TM = 544
    TN = 1024
    assert t % TM == 0 and tp % 512 == 0 and ts == 512

    x = jnp.concatenate([x_prompt.reshape(tp, d), x_sample.reshape(ts, d)], axis=0)
    mem_bf = mem_prompt.reshape(bp * n_mem, d).astype(BF16)

    outs = {}
    ie = io = 0
    p_mem_k, p_mem_v = [], []
    for l in range(depth):
        u = rmsnorm(x, norm_mix[l])
        if l % 2 == 0:
            w = w_in_e[ie]
            o_qkvo = 2 * aqk + 2 * aw
            o_if = o_qkvo + 2 * ah
            w_main = jnp.concatenate([w[:, :o_qkvo], w[:, o_if:]], axis=1).astype(BF16)
            w_if = jnp.pad(w[:, o_qkvo:o_if], ((0, 0), (0, V7X_LANES - 2 * ah))).astype(BF16)
            bias_if = jnp.pad(b_if[ie], (0, V7X_LANES - 2 * ah)).reshape(1, V7X_LANES)
            col_q, col_k, col_v, col_o = 0, aqk, 2 * aqk, 2 * aqk + aw
            col_x, col_g = o_qkvo, o_qkvo + bw
            proj = matmul([u], [w_main], tm=TM, tn=TN, name="in_proj_even")
            gc, gr = mlstm_gates(u, w_if, bias_if, n_prompt_rows=tp, seg_sample=ss)
            gr = gr.reshape(16, 1, t)
            mh = mh_norm[ie].reshape(1, aw)
            mkw = dict(n_heads=ah, dk=dk, dv=dv, col_q=col_q, col_k=col_k, col_v=col_v, col_o=col_o)
            ya_p, pc, pn, pm = mlstm_prompt(proj, gc, gr, mh, batch=bp, seq=sp, **mkw)
            n0_tok = jnp.repeat(jnp.transpose(state_mlstm_n[ie], (1, 0, 2)), ss, axis=1)
            m0_tok = jnp.broadcast_to(jnp.repeat(state_mlstm_m[ie].T, ss, axis=1)[:, :, None], (ah, ts, MLSTM_CHUNK))
            ya_s, sc_, sn_tok, sm_tok = mlstm_sample(proj, gc, gr, mh, state_mlstm_c[ie:ie + 1], n0_tok, m0_tok,
                                                     row0=tp, n_seq=bs, seg=ss, **mkw)
            outs.setdefault("pc", []).append(pc.reshape(bp, ah, dk, dv))
            outs.setdefault("pn", []).append(pn.reshape(bp, ah, dk))
            outs.setdefault("pm", []).append(pm.reshape(bp, ah))
            outs.setdefault("sc", []).append(sc_[0])
            outs.setdefault("sn", []).append(jnp.transpose(sn_tok[:, ss - 1::ss, :], (1, 0, 2)))
            outs.setdefault("sm", []).append(sm_tok[:, ss - 1::ss, 0].T)

            cw = conv_b_w[ie]
            cbias = conv_b_b[ie].reshape(1, bw)
            wa = lru_wa[ie].astype(BF16)
            wi = lru_wi[ie].astype(BF16)
            ba = lru_ba[ie].reshape(1, bw)
            bi = lru_bi[ie].reshape(1, bw)
            lam = lru_lam[ie].reshape(1, bw)
            halo_p = jnp.zeros((bp, V7X_SUBLANES, bw), F32)
            h0_p = jnp.zeros((bp, 1, bw), F32)
            yb_p, ph = lru_prompt(proj, halo_p, h0_p, cw, cbias, wa, wi, ba, bi, lam,
                                  batch=bp, seq=sp, width=bw, col_x=col_x, col_g=col_g)
            st_s = jnp.transpose(state_lru_conv[ie], (1, 0, 2))
            xb_p = proj[:tp, col_x:col_x + bw].reshape(bp, sp, bw)
            xb_s = proj[tp:, col_x:col_x + bw].reshape(bs, ss, bw)
            gb_s = proj[tp:, col_g:col_g + bw].reshape(bs, ss, bw)
            yb_s, sh = lru_sample(jnp.transpose(xb_s, (1, 0, 2)), jnp.transpose(gb_s, (1, 0, 2)), st_s,
                                  state_lru_h[ie], cw, cbias, wa, wi, ba, bi, lam)
            yb_s = jnp.transpose(yb_s, (1, 0, 2)).reshape(ts, bw)
            outs.setdefault("ph", []).append(ph.reshape(bp, bw))
            outs.setdefault("pcb", []).append(xb_p[:, sp - (bconv - 1):])
            outs.setdefault("sh", []).append(sh)
            outs.setdefault("scb", []).append(jnp.concatenate([state_lru_conv[ie], xb_s], axis=1)[:, ss:])

            ya = jnp.concatenate([ya_p, ya_s], axis=0)
            yb = jnp.concatenate([yb_p, yb_s], axis=0)
            wo = w_out_e[ie].astype(BF16)
            x = matmul([ya, yb], [wo[:aw], wo[aw:]], res=x, tm=TM, tn=TN, name="out_proj_even")
            ie += 1
        else:
            proj = matmul([u], [w_in_o[io].astype(BF16)], tm=TM, tn=TN, name="in_proj_odd")
            cw = conv_c_w[io]
            halo_p = jnp.zeros((bp, V7X_SUBLANES, cwid), F32)
            y_p, tail = sconv_prompt(proj, halo_p, cw, batch=bp, seq=sp, width=cwid)
            st_s = jnp.transpose(state_sconv[io], (1, 0, 2))
            proj_t = jnp.transpose(proj[tp:].reshape(bs, ss, 3 * cwid), (1, 0, 2))
            y_s, ps = sconv_sample(proj_t, st_s, cw, width=cwid)
            y_s = jnp.transpose(y_s, (1, 0, 2)).reshape(ts, cwid)
            outs.setdefault("psb", []).append(tail[:, V7X_SUBLANES - (cconv - 1):])
            ps_b = jnp.transpose(ps, (1, 0, 2))
            outs.setdefault("ssb", []).append(jnp.concatenate([state_sconv[io], ps_b], axis=1)[:, ss:])
            y = jnp.concatenate([y_p, y_s], axis=0)
            x = matmul([y], [w_out_o[io].astype(BF16)], res=x, tm=TM, tn=TN, name="out_proj_odd")
            io += 1

        u = rmsnorm(x, norm_x[l])
        q = matmul([u], [w_xq[l].astype(BF16)], out_dtype=BF16, tm=TM, tn=TN, name="xattn_q")
        mk = matmul([mem_bf], [w_xk[l].astype(BF16)], tm=512, tn=TN, name="mem_k")
        mv = matmul([mem_bf], [w_xv[l].astype(BF16)], tm=512, tn=TN, name="mem_v")
        p_mem_k.append(mk.reshape(bp, n_mem, xh, xhd))
        p_mem_v.append(mv.reshape(bp, n_mem, xh, xhd))
        o_p = xattn_prompt(q, mk, mv, batch=bp, seq=sp, n_mem=n_mem, n_heads=xh, hd=xhd)
        o_s = xattn_sample(q[tp:].reshape(bs, ss, d), cache_mem_k[l].reshape(bs, n_mem, d),
                           cache_mem_v[l].reshape(bs, n_mem, d), n_heads=xh, hd=xhd)
        o = jnp.concatenate([o_p, o_s.reshape(ts, d)], axis=0)
        x = matmul([o], [w_xo[l].astype(BF16)], res=x, tm=TM, tn=TN, name="xattn_o")

        u = rmsnorm(x, norm_ff[l])
        hmid = matmul([u], [w_ff1[l].astype(BF16)], act="relu2", out_dtype=BF16, tm=TM, tn=TN, name="ff1")
        x = matmul_ksplit(hmid, w_ff2[l].astype(BF16), x, tm=TM, tn=TN, tk=2048, name="ff2")

    y = rmsnorm(x, norm_final, out_dtype=F32)
    dt = x_prompt.dtype
    st = lambda name: jnp.stack(outs[name]).astype(dt)
    return (y[:tp].reshape(bp, sp, d), y[tp:].reshape(bs, ss, d),
            st("pc"), st("pn"), st("pm"), st("ph"), st("pcb"), st("psb"),
            jnp.stack(p_mem_k), jnp.stack(p_mem_v),
            st("sc"), st("sn"), st("sm"), st("sh"), st("scb"), st("ssb"))
```

```python
import functools

import jax
import jax.numpy as jnp
from jax import lax
from jax.experimental import pallas as pl
from jax.experimental.pallas import tpu as pltpu

F32 = jnp.float32
BF16 = jnp.bfloat16
EPS = 1e-6
LRU_C = 8.0

V7X_LANES = 128
V7X_SUBLANES = 8
V7X_VMEM_BUDGET_BYTES = 56 * 1024 * 1024

MLSTM_CHUNK = 128
SCAN_CHUNK = 256


def _cparams(*semantics):
    return pltpu.CompilerParams(dimension_semantics=semantics, vmem_limit_bytes=V7X_VMEM_BUDGET_BYTES)


def _log_sigmoid(x):
    return jnp.minimum(x, 0.0) - jnp.log1p(jnp.exp(-jnp.abs(x)))


def _sigmoid(x):
    return 1.0 / (1.0 + jnp.exp(-x))


def _gelu_tanh(x):
    c = 0.7978845608028654
    return 0.5 * x * (1.0 + jnp.tanh(c * (x + 0.044715 * (x * x * x))))


def _expm1(x):
    u = jnp.exp(x)
    um1 = u - 1.0
    safe = jnp.where(um1 == 0.0, 1.0, jnp.log(u))
    return jnp.where(um1 == 0.0, x, jnp.where(um1 == -1.0, -1.0, um1 * x / safe))


def _split3(x):
    hi = x.astype(BF16)
    r1 = x - hi.astype(F32)
    mid = r1.astype(BF16)
    lo = (r1 - mid.astype(F32)).astype(BF16)
    return hi, mid, lo


def _dot01(mask_bf16, x):
    hi, mid, lo = _split3(x)
    acc = jnp.dot(mask_bf16, lo, preferred_element_type=F32)
    acc = acc + jnp.dot(mask_bf16, mid, preferred_element_type=F32)
    return acc + jnp.dot(mask_bf16, hi, preferred_element_type=F32)


def _rmsnorm_body(x_ref, g_ref, o_ref):
    x = x_ref[...]
    inv = lax.rsqrt(jnp.mean(x * x, axis=-1, keepdims=True) + EPS)
    o_ref[...] = ((x * inv) * g_ref[...]).astype(o_ref.dtype)


def rmsnorm(x, g, out_dtype=BF16, rows=256, row0=0, nrows=None):
    d = x.shape[1]
    nrows = x.shape[0] if nrows is None else nrows
    b0 = row0 // rows
    return pl.pallas_call(
        _rmsnorm_body,
        grid=(nrows // rows,),
        in_specs=[pl.BlockSpec((rows, d), lambda i: (b0 + i, 0)), pl.BlockSpec((1, d), lambda i: (0, 0))],
        out_specs=pl.BlockSpec((rows, d), lambda i: (i, 0)),
        out_shape=jax.ShapeDtypeStruct((nrows, d), out_dtype),
        compiler_params=_cparams("parallel"),
        name="rmsnorm",
    )(x, g.reshape(1, d))


def _mm_body(*refs, n_in, has_res, act):
    o_ref = refs[-1]
    acc = jnp.dot(refs[0][...], refs[n_in][...], preferred_element_type=F32)
    for k in range(1, n_in):
        acc = acc + jnp.dot(refs[k][...], refs[n_in + k][...], preferred_element_type=F32)
    if act == "relu2":
        r = jnp.maximum(acc, 0.0)
        acc = r * r
    if has_res:
        acc = refs[2 * n_in][...] + acc
    o_ref[...] = acc.astype(o_ref.dtype)


def matmul(xs, ws, *, res=None, act=None, out_dtype=F32, tm, tn, name):
    m = xs[0].shape[0]
    n = (ws[0][0] if isinstance(ws[0], tuple) else ws[0]).shape[-1]
    n_in = len(xs)
    in_specs = [pl.BlockSpec((tm, x.shape[1]), lambda j, i: (i, 0)) for x in xs]
    args = list(xs)
    for x, w in zip(xs, ws):
        if isinstance(w, tuple):
            w, layer, kb = w
            in_specs.append(pl.BlockSpec((None, x.shape[1], tn), lambda j, i, layer=layer, kb=kb: (layer, kb, j)))
        else:
            in_specs.append(pl.BlockSpec((w.shape[0], tn), lambda j, i: (0, j)))
        args.append(w)
    if res is not None:
        in_specs.append(pl.BlockSpec((tm, tn), lambda j, i: (i, j)))
        args.append(res)
    return pl.pallas_call(
        functools.partial(_mm_body, n_in=n_in, has_res=res is not None, act=act),
        grid=(n // tn, m // tm),
        in_specs=in_specs,
        out_specs=pl.BlockSpec((tm, tn), lambda j, i: (i, j)),
        out_shape=jax.ShapeDtypeStruct((m, n), out_dtype),
        compiler_params=_cparams("parallel", "parallel"),
        name=name,
    )(*args)


def _mm_acc_body(x_ref, w_ref, res_ref, o_ref, acc_ref):
    k = pl.program_id(2)

    @pl.when(k == 0)
    def _():
        acc_ref[...] = res_ref[...]

    acc_ref[...] += jnp.dot(x_ref[...], w_ref[...], preferred_element_type=F32)

    @pl.when(k == pl.num_programs(2) - 1)
    def _():
        o_ref[...] = acc_ref[...]


def matmul_ksplit(x, w, layer, res, *, tm, tn, tk, name):
    m, kk = x.shape
    n = w.shape[-1]
    return pl.pallas_call(
        _mm_acc_body,
        grid=(m // tm, n // tn, kk // tk),
        in_specs=[
            pl.BlockSpec((tm, tk), lambda i, j, k: (i, k)),
            pl.BlockSpec((None, tk, tn), lambda i, j, k: (layer, k, j)),
            pl.BlockSpec((tm, tn), lambda i, j, k: (i, j)),
        ],
        out_specs=pl.BlockSpec((tm, tn), lambda i, j, k: (i, j)),
        out_shape=jax.ShapeDtypeStruct((m, n), F32),
        scratch_shapes=[pltpu.VMEM((tm, tn), F32)],
        compiler_params=_cparams("parallel", "parallel", "arbitrary"),
        name=name,
    )(x, w, res)


def _gates_body(u_ref, w_ref, b_ref, gc_ref, gr_ref, *, tm, n_prompt_tiles, seg_sample):
    L = MLSTM_CHUNK
    g = jnp.dot(u_ref[...], w_ref[...], preferred_element_type=F32) + b_ref[...]
    lane = lax.broadcasted_iota(jnp.int32, (tm, V7X_LANES), 1)
    val = jnp.where(lane < 8, g, _log_sigmoid(g))
    row = lax.broadcasted_iota(jnp.int32, (L, L), 0)
    col = lax.broadcasted_iota(jnp.int32, (L, L), 1)
    is_sample = pl.program_id(0) >= n_prompt_tiles
    tril = jnp.where(col <= row, 1.0, 0.0)
    tril_seg = jnp.where((col <= row) & ((row // seg_sample) == (col // seg_sample)), 1.0, 0.0)
    mask = jnp.where(is_sample, tril_seg, tril).astype(BF16)
    lane_l = lax.broadcasted_iota(jnp.int32, (L, V7X_LANES), 1)
    for s in range(tm // L):
        x = val[s * L:(s + 1) * L]
        out = jnp.where(lane_l < 8, x, _dot01(mask, x))
        gr_ref[:, s * L:(s + 1) * L] = out.T[:16]
        for c in range(16):
            gc_ref[c, s * L:(s + 1) * L, :] = jnp.broadcast_to(out[:, c:c + 1], (L, V7X_LANES))


def mlstm_gates(u, w_if, b_if, *, n_prompt_rows, seg_sample, tm=512):
    t, d = u.shape
    return pl.pallas_call(
        functools.partial(_gates_body, tm=tm, n_prompt_tiles=n_prompt_rows // tm, seg_sample=seg_sample),
        grid=(t // tm,),
        in_specs=[
            pl.BlockSpec((tm, d), lambda i: (i, 0)),
            pl.BlockSpec((d, V7X_LANES), lambda i: (0, 0)),
            pl.BlockSpec((1, V7X_LANES), lambda i: (0, 0)),
        ],
        out_specs=[
            pl.BlockSpec((16, tm, V7X_LANES), lambda i: (0, i, 0)),
            pl.BlockSpec((16, tm), lambda i: (0, i)),
        ],
        out_shape=[
            jax.ShapeDtypeStruct((16, t, V7X_LANES), F32),
            jax.ShapeDtypeStruct((16, t), F32),
        ],
        compiler_params=_cparams("parallel"),
        name="mlstm_gates",
    )(u, w_if, b_if)


def _head_out(hh, o, mh):
    hn = hh * lax.rsqrt(jnp.mean(hh * hh, axis=-1, keepdims=True) + EPS)
    return _sigmoid(o) * (hn * mh)


def _mlstm_prompt_body(q_ref, k_ref, v_ref, o_ref, ic_ref, bc_ref, ir_ref, br_ref, mh_ref,
                       ya_ref, c_ref, n_ref, m_ref, c_scr, n_scr, m_scr, *, n_heads, seq, scale):
    L = MLSTM_CHUNK
    c_scr[...] = jnp.zeros_like(c_scr)
    n_scr[...] = jnp.zeros_like(n_scr)
    m_scr[...] = jnp.zeros_like(m_scr)
    row = lax.broadcasted_iota(jnp.int32, (L, L), 0)
    col = lax.broadcasted_iota(jnp.int32, (L, L), 1)
    causal = col <= row

    def chunk(ci, carry):
        r0 = pl.multiple_of(ci * L, L)
        q = q_ref[pl.ds(r0, L), :] * scale
        k = k_ref[pl.ds(r0, L), :]
        v = v_ref[pl.ds(r0, L), :]
        bc = bc_ref[0, pl.ds(r0, L), :]
        ic = ic_ref[0, pl.ds(r0, L), :]
        i_row = ir_ref[0, :, pl.ds(r0, L)]
        b_row = br_ref[0, :, pl.ds(r0, L)]
        m_prev = m_scr[...]
        c_prev = c_scr[...]
        n_prev = n_scr[...]
        dmat = jnp.where(causal, bc - b_row + i_row, -jnp.inf)
        g = bc + m_prev
        mt = jnp.maximum(g, jnp.max(dmat, axis=-1, keepdims=True))
        w_inter = jnp.exp(g - mt)
        qb = q.astype(BF16)
        kb = k.astype(BF16)
        vb = v.astype(BF16)
        sc = lax.dot_general(qb, kb, (((1,), (1,)), ((), ())), preferred_element_type=F32) * jnp.exp(dmat - mt)
        inter = jnp.dot(qb, c_prev.astype(BF16), preferred_element_type=F32)
        num = jnp.concatenate([w_inter, w_inter], axis=1) * inter + jnp.dot(sc.astype(BF16), vb, preferred_element_type=F32)
        qn = jnp.sum(q * n_prev, axis=-1, keepdims=True)
        den = w_inter[:, :1] * qn + jnp.sum(sc, axis=-1, keepdims=True)
        hh = num / jnp.maximum(jnp.abs(den), jnp.exp(-mt[:, :1]))
        ya_ref[pl.ds(r0, L), :] = _head_out(hh, o_ref[pl.ds(r0, L), :], mh_ref[...]).astype(ya_ref.dtype)
        m_new = mt[L - 1:L, :1]
        b_last = b_row[:, L - 1:L]
        decay = jnp.exp(b_last + m_prev - m_new)
        kw = k * jnp.exp(b_last - bc + ic - m_new)
        c_scr[...] = decay * c_prev + lax.dot_general(kw.astype(BF16), vb, (((0,), (0,)), ((), ())), preferred_element_type=F32)
        n_scr[...] = decay * n_prev + jnp.sum(kw, axis=0, keepdims=True)
        m_scr[...] = m_new
        return carry

    lax.fori_loop(0, seq // L, chunk, 0)
    c_ref[0, 0] = c_scr[...]
    n_ref[0, 0] = n_scr[...]
    m_ref[0, 0] = m_scr[...]


def mlstm_prompt(proj, gc, gr, mh_norm, *, batch, seq, n_heads, dk, dv, col_q, col_k, col_v, col_o):
    L = MLSTM_CHUNK
    assert dk == L and dv == 2 * L
    rows = batch * seq
    return pl.pallas_call(
        functools.partial(_mlstm_prompt_body, n_heads=n_heads, seq=seq, scale=dk ** -0.5),
        grid=(batch, n_heads),
        in_specs=[
            pl.BlockSpec((seq, dk), lambda b, h: (b, col_q // dk + h)),
            pl.BlockSpec((seq, dk), lambda b, h: (b, col_k // dk + h)),
            pl.BlockSpec((seq, dv), lambda b, h: (b, col_v // dv + h)),
            pl.BlockSpec((seq, dv), lambda b, h: (b, col_o // dv + h)),
            pl.BlockSpec((1, seq, V7X_LANES), lambda b, h: (h, b, 0)),
            pl.BlockSpec((1, seq, V7X_LANES), lambda b, h: (h + n_heads, b, 0)),
            pl.BlockSpec((1, 1, seq), lambda b, h: (h, 0, b)),
            pl.BlockSpec((1, 1, seq), lambda b, h: (h + n_heads, 0, b)),
            pl.BlockSpec((1, dv), lambda b, h: (0, h)),
        ],
        out_specs=[
            pl.BlockSpec((seq, dv), lambda b, h: (b, h)),
            pl.BlockSpec((1, 1, dk, dv), lambda b, h: (b, h, 0, 0)),
            pl.BlockSpec((1, 1, 1, dk), lambda b, h: (b, h, 0, 0)),
            pl.BlockSpec((1, 1, 1, 1), lambda b, h: (b, h, 0, 0)),
        ],
        out_shape=[
            jax.ShapeDtypeStruct((rows, n_heads * dv), BF16),
            jax.ShapeDtypeStruct((batch, n_heads, dk, dv), F32),
            jax.ShapeDtypeStruct((batch, n_heads, 1, dk), F32),
            jax.ShapeDtypeStruct((batch, n_heads, 1, 1), F32),
        ],
        scratch_shapes=[pltpu.VMEM((dk, dv), F32), pltpu.VMEM((1, dk), F32), pltpu.VMEM((1, 1), F32)],
        compiler_params=_cparams("parallel", "parallel"),
        name="mlstm_prompt",
    )(proj, proj, proj, proj, gc, gc, gr, gr, mh_norm)


def _mlstm_sample_body(q_ref, k_ref, v_ref, o_ref, ic_ref, bc_ref, ir_ref, br_ref, mh_ref, c0_ref, n0_ref, m0_ref,
                       ya_ref, c_ref, n_ref, m_ref, inter_scr, *, n_heads, seg, scale):
    L = MLSTM_CHUNK
    nb = L // seg
    row = lax.broadcasted_iota(jnp.int32, (L, L), 0)
    col = lax.broadcasted_iota(jnp.int32, (L, L), 1)
    same = (row // seg) == (col // seg)
    causal = (col <= row) & same
    last = col == (row // seg) * seg + (seg - 1)
    q = q_ref[...] * scale
    k = k_ref[...]
    v = v_ref[...]
    bc = bc_ref[0]
    ic = ic_ref[0]
    i_row = ir_ref[0]
    b_row = br_ref[0]
    m_prev = m0_ref[0]
    n_prev = n0_ref[0]
    dmat = jnp.where(causal, bc - b_row + i_row, -jnp.inf)
    g = bc + m_prev
    mt = jnp.maximum(g, jnp.max(dmat, axis=-1, keepdims=True))
    w_inter = jnp.exp(g - mt)
    qb = q.astype(BF16)
    kb = k.astype(BF16)
    vb = v.astype(BF16)
    sc = lax.dot_general(qb, kb, (((1,), (1,)), ((), ())), preferred_element_type=F32) * jnp.exp(dmat - mt)

    per_group = V7X_SUBLANES // seg
    sub = lax.broadcasted_iota(jnp.int32, (V7X_SUBLANES, 2 * L), 0)
    for gi in range(L // V7X_SUBLANES):
        q8 = qb[gi * V7X_SUBLANES:(gi + 1) * V7X_SUBLANES]
        acc = jnp.zeros((V7X_SUBLANES, 2 * L), F32)
        for j in range(per_group):
            r = jnp.dot(q8, c0_ref[0, gi * per_group + j, 0].astype(BF16), preferred_element_type=F32)
            acc = jnp.where((sub // seg) == j, r, acc)
        inter_scr[gi * V7X_SUBLANES:(gi + 1) * V7X_SUBLANES, :] = acc
    inter = inter_scr[...]

    num = jnp.concatenate([w_inter, w_inter], axis=1) * inter + jnp.dot(sc.astype(BF16), vb, preferred_element_type=F32)
    qn = jnp.sum(q * n_prev, axis=-1, keepdims=True)
    den = w_inter[:, :1] * qn + jnp.sum(sc, axis=-1, keepdims=True)
    hh = num / jnp.maximum(jnp.abs(den), jnp.exp(-mt[:, :1]))
    ya_ref[...] = _head_out(hh, o_ref[...], mh_ref[...]).astype(ya_ref.dtype)

    b_last = jnp.sum(jnp.where(last, jnp.broadcast_to(b_row, (L, L)), 0.0), axis=-1, keepdims=True)
    mt_row = mt.T
    m_new = jnp.sum(jnp.where(last, mt_row, 0.0), axis=-1, keepdims=True)
    decay = jnp.exp(b_last + m_prev - m_new)
    kw = k * jnp.exp(b_last - bc + ic - m_new)
    seg_ones = jnp.where(same, 1.0, 0.0).astype(BF16)
    n_ref[0] = decay * n_prev + _dot01(seg_ones, kw)
    m_ref[0] = jnp.broadcast_to(m_new, (L, L))
    rowk = lax.broadcasted_iota(jnp.int32, (L, L), 0)
    for bi in range(nb):
        kw_b = jnp.where((rowk // seg) == bi, kw, 0.0).astype(BF16)
        upd = lax.dot_general(kw_b, vb, (((0,), (0,)), ((), ())), preferred_element_type=F32)
        d_b = decay[bi * seg:bi * seg + 1, :1]
        c_ref[0, bi, 0] = d_b * c0_ref[0, bi, 0] + upd


def mlstm_sample(proj, gc, gr, mh_norm, c0, n0_tok, m0_tok, *, row0, n_seq, seg, n_heads, dk, dv,
                 col_q, col_k, col_v, col_o):
    L = MLSTM_CHUNK
    assert dk == L and dv == 2 * L and V7X_SUBLANES % seg == 0 and row0 % L == 0
    rows = n_seq * seg
    nb = L // seg
    t0 = row0 // L
    return pl.pallas_call(
        functools.partial(_mlstm_sample_body, n_heads=n_heads, seg=seg, scale=dk ** -0.5),
        grid=(rows // L, n_heads),
        in_specs=[
            pl.BlockSpec((L, dk), lambda j, h: (t0 + j, col_q // dk + h)),
            pl.BlockSpec((L, dk), lambda j, h: (t0 + j, col_k // dk + h)),
            pl.BlockSpec((L, dv), lambda j, h: (t0 + j, col_v // dv + h)),
            pl.BlockSpec((L, dv), lambda j, h: (t0 + j, col_o // dv + h)),
            pl.BlockSpec((1, L, V7X_LANES), lambda j, h: (h, t0 + j, 0)),
            pl.BlockSpec((1, L, V7X_LANES), lambda j, h: (h + n_heads, t0 + j, 0)),
            pl.BlockSpec((1, 1, L), lambda j, h: (h, 0, t0 + j)),
            pl.BlockSpec((1, 1, L), lambda j, h: (h + n_heads, 0, t0 + j)),
            pl.BlockSpec((1, dv), lambda j, h: (0, h)),
            pl.BlockSpec((1, nb, 1, dk, dv), lambda j, h: (0, j, h, 0, 0)),
            pl.BlockSpec((1, L, dk), lambda j, h: (h, j, 0)),
            pl.BlockSpec((1, L, L), lambda j, h: (h, j, 0)),
        ],
        out_specs=[
            pl.BlockSpec((L, dv), lambda j, h: (j, h)),
            pl.BlockSpec((1, nb, 1, dk, dv), lambda j, h: (0, j, h, 0, 0)),
            pl.BlockSpec((1, L, dk), lambda j, h: (h, j, 0)),
            pl.BlockSpec((1, L, L), lambda j, h: (h, j, 0)),
        ],
        out_shape=[
            jax.ShapeDtypeStruct((rows, n_heads * dv), BF16),
            jax.ShapeDtypeStruct((1, n_seq, n_heads, dk, dv), F32),
            jax.ShapeDtypeStruct((n_heads, rows, dk), F32),
            jax.ShapeDtypeStruct((n_heads, rows, L), F32),
        ],
        scratch_shapes=[pltpu.VMEM((L, dv), F32)],
        compiler_params=_cparams("parallel", "parallel"),
        name="mlstm_sample",
    )(proj, proj, proj, proj, gc, gc, gr, gr, mh_norm, c0, n0_tok, m0_tok)


def _lru_gates(xc, wa_ref, wi_ref, ba, bi, lam):
    nblk = wa_ref.shape[0]
    blk = wa_ref.shape[1]
    rs, igs = [], []
    for n in range(nblk):
        xb = xc[:, n * blk:(n + 1) * blk].astype(BF16)
        rs.append(jnp.dot(xb, wa_ref[n], preferred_element_type=F32))
        igs.append(jnp.dot(xb, wi_ref[n], preferred_element_type=F32))
    r = _sigmoid(jnp.concatenate(rs, axis=1) + ba)
    ig = _sigmoid(jnp.concatenate(igs, axis=1) + bi)
    log_a = LRU_C * r * _log_sigmoid(lam)
    a = jnp.exp(log_a)
    u = jnp.sqrt(-_expm1(2.0 * log_a)) * (ig * xc)
    return a, u


def _lru_prompt_body(x_ref, g_ref, halo_ref, h0_ref, cw_ref, cb_ref, wa_ref, wi_ref, ba_ref, bi_ref, lam_ref,
                     y_ref, hl_ref, h_scr, *, seq):
    Lc = SCAN_CHUNK
    cbw = x_ref.shape[1]
    h_scr[...] = h0_ref[0]
    row = lax.broadcasted_iota(jnp.int32, (Lc, cbw), 0)
    cw = cw_ref[...]
    width = cw.shape[0]

    def chunk(ci, carry):
        r0 = pl.multiple_of(ci * Lc, Lc)
        x = x_ref[pl.ds(r0, Lc), :]
        prev = x_ref[pl.ds(pl.multiple_of(jnp.maximum(r0 - V7X_SUBLANES, 0), V7X_SUBLANES), V7X_SUBLANES), :]
        prev = jnp.where(ci == 0, halo_ref[0], prev)
        xfull = jnp.concatenate([prev, x], axis=0)
        xc = None
        for j in range(width - 1, 0, -1):
            term = pltpu.roll(xfull, j, 0)[V7X_SUBLANES:] * cw[width - 1 - j:width - j]
            xc = term if xc is None else xc + term
        xc = xc + x * cw[width - 1:width] + cb_ref[...]
        a, u = _lru_gates(xc, wa_ref, wi_ref, ba_ref[...], bi_ref[...], lam_ref[...])
        d = 1
        while d < Lc:
            ok = row >= d
            a_sh = jnp.where(ok, pltpu.roll(a, d, 0), 1.0)
            u_sh = jnp.where(ok, pltpu.roll(u, d, 0), 0.0)
            u = a * u_sh + u
            a = a * a_sh
            d *= 2
        hs = u + a * h_scr[...]
        h_scr[...] = hs[Lc - 1:Lc]
        y_ref[pl.ds(r0, Lc), :] = (hs * _gelu_tanh(g_ref[pl.ds(r0, Lc), :])).astype(y_ref.dtype)
        return carry

    lax.fori_loop(0, seq // Lc, chunk, 0)
    hl_ref[0] = h_scr[...]


def lru_prompt(proj, halo, h0, cw, cb, wa, wi, ba, bi, lam, *, batch, seq, width, col_x, col_g, cbw=256):
    blk = wa.shape[1]
    nper = cbw // blk
    vec = lambda b, c: (0, c)
    return pl.pallas_call(
        functools.partial(_lru_prompt_body, seq=seq),
        grid=(batch, width // cbw),
        in_specs=[
            pl.BlockSpec((seq, cbw), lambda b, c: (b, col_x // cbw + c)),
            pl.BlockSpec((seq, cbw), lambda b, c: (b, col_g // cbw + c)),
            pl.BlockSpec((1, V7X_SUBLANES, cbw), lambda b, c: (b, 0, c)),
            pl.BlockSpec((1, 1, cbw), lambda b, c: (b, 0, c)),
            pl.BlockSpec((cw.shape[0], cbw), vec),
            pl.BlockSpec((1, cbw), vec),
            pl.BlockSpec((nper, blk, blk), lambda b, c: (c, 0, 0)),
            pl.BlockSpec((nper, blk, blk), lambda b, c: (c, 0, 0)),
            pl.BlockSpec((1, cbw), vec),
            pl.BlockSpec((1, cbw), vec),
            pl.BlockSpec((1, cbw), vec),
        ],
        out_specs=[
            pl.BlockSpec((seq, cbw), lambda b, c: (b, c)),
            pl.BlockSpec((1, 1, cbw), lambda b, c: (b, 0, c)),
        ],
        out_shape=[
            jax.ShapeDtypeStruct((batch * seq, width), BF16),
            jax.ShapeDtypeStruct((batch, 1, width), F32),
        ],
        scratch_shapes=[pltpu.VMEM((1, cbw), F32)],
        compiler_params=_cparams("parallel", "parallel"),
        name="lru_prompt",
    )(proj, proj, halo, h0, cw, cb, wa, wi, ba, bi, lam)


def _lru_sample_body(x_ref, g_ref, st_ref, h0_ref, cw_ref, cb_ref, wa_ref, wi_ref, ba_ref, bi_ref, lam_ref,
                     y_ref, hl_ref, *, seg):
    cw = cw_ref[...]
    width = cw.shape[0]
    hist = [st_ref[j] for j in range(width - 1)]
    h = h0_ref[...]
    for t in range(seg):
        x = x_ref[t]
        taps = hist + [x]
        xc = taps[0] * cw[0:1]
        for j in range(1, width):
            xc = xc + taps[j] * cw[j:j + 1]
        xc = xc + cb_ref[...]
        a, u = _lru_gates(xc, wa_ref, wi_ref, ba_ref[...], bi_ref[...], lam_ref[...])
        h = a * h + u
        y_ref[t] = (h * _gelu_tanh(g_ref[t])).astype(y_ref.dtype)
        hist = hist[1:] + [x]
    hl_ref[...] = h


def lru_sample(x_t, g_t, st, h0, cw, cb, wa, wi, ba, bi, lam, *, cbw=256):
    seg, n_seq, width = x_t.shape
    blk = wa.shape[1]
    nper = cbw // blk
    vec = lambda c: (0, c)
    return pl.pallas_call(
        functools.partial(_lru_sample_body, seg=seg),
        grid=(width // cbw,),
        in_specs=[
            pl.BlockSpec((seg, n_seq, cbw), lambda c: (0, 0, c)),
            pl.BlockSpec((seg, n_seq, cbw), lambda c: (0, 0, c)),
            pl.BlockSpec((st.shape[0], n_seq, cbw), lambda c: (0, 0, c)),
            pl.BlockSpec((n_seq, cbw), vec),
            pl.BlockSpec((cw.shape[0], cbw), vec),
            pl.BlockSpec((1, cbw), vec),
            pl.BlockSpec((nper, blk, blk), lambda c: (c, 0, 0)),
            pl.BlockSpec((nper, blk, blk), lambda c: (c, 0, 0)),
            pl.BlockSpec((1, cbw), vec),
            pl.BlockSpec((1, cbw), vec),
            pl.BlockSpec((1, cbw), vec),
        ],
        out_specs=[
            pl.BlockSpec((seg, n_seq, cbw), lambda c: (0, 0, c)),
            pl.BlockSpec((n_seq, cbw), vec),
        ],
        out_shape=[
            jax.ShapeDtypeStruct((seg, n_seq, width), BF16),
            jax.ShapeDtypeStruct((n_seq, width), F32),
        ],
        compiler_params=_cparams("parallel"),
        name="lru_sample",
    )(x_t, g_t, st, h0, cw, cb, wa, wi, ba, bi, lam)


def _sconv_prompt_body(bg_ref, cg_ref, hx_ref, halo_ref, cw_ref, y_ref, tail_ref, *, seq):
    Lc = SCAN_CHUNK
    cw = cw_ref[...]
    width = cw.shape[0]

    def chunk(ci, carry):
        r0 = pl.multiple_of(ci * Lc, Lc)
        p = cg_ref[pl.ds(r0, Lc), :] * hx_ref[pl.ds(r0, Lc), :]
        rp = pl.multiple_of(jnp.maximum(r0 - V7X_SUBLANES, 0), V7X_SUBLANES)
        prev = cg_ref[pl.ds(rp, V7X_SUBLANES), :] * hx_ref[pl.ds(rp, V7X_SUBLANES), :]
        prev = jnp.where(ci == 0, halo_ref[0], prev)
        pfull = jnp.concatenate([prev, p], axis=0)
        z = None
        for j in range(width - 1, 0, -1):
            term = pltpu.roll(pfull, j, 0)[V7X_SUBLANES:] * cw[width - 1 - j:width - j]
            z = term if z is None else z + term
        z = z + p * cw[width - 1:width]
        y_ref[pl.ds(r0, Lc), :] = (bg_ref[pl.ds(r0, Lc), :] * z).astype(y_ref.dtype)
        return carry

    lax.fori_loop(0, seq // Lc, chunk, 0)
    rt = seq - V7X_SUBLANES
    tail_ref[0] = cg_ref[pl.ds(rt, V7X_SUBLANES), :] * hx_ref[pl.ds(rt, V7X_SUBLANES), :]


def sconv_prompt(proj, halo, cw, *, batch, seq, width, cbw=512):
    nb = width // cbw
    return pl.pallas_call(
        functools.partial(_sconv_prompt_body, seq=seq),
        grid=(batch, nb),
        in_specs=[
            pl.BlockSpec((seq, cbw), lambda b, c: (b, c)),
            pl.BlockSpec((seq, cbw), lambda b, c: (b, nb + c)),
            pl.BlockSpec((seq, cbw), lambda b, c: (b, 2 * nb + c)),
            pl.BlockSpec((1, V7X_SUBLANES, cbw), lambda b, c: (b, 0, c)),
            pl.BlockSpec((cw.shape[0], cbw), lambda b, c: (0, c)),
        ],
        out_specs=[
            pl.BlockSpec((seq, cbw), lambda b, c: (b, c)),
            pl.BlockSpec((1, V7X_SUBLANES, cbw), lambda b, c: (b, 0, c)),
        ],
        out_shape=[
            jax.ShapeDtypeStruct((batch * seq, width), BF16),
            jax.ShapeDtypeStruct((batch, V7X_SUBLANES, width), F32),
        ],
        compiler_params=_cparams("parallel", "parallel"),
        name="sconv_prompt",
    )(proj, proj, proj, halo, cw)


def _sconv_sample_body(bg_ref, cg_ref, hx_ref, st_ref, cw_ref, y_ref, ps_ref, *, seg):
    cw = cw_ref[...]
    width = cw.shape[0]
    hist = [st_ref[j] for j in range(width - 1)]
    for t in range(seg):
        p = cg_ref[t] * hx_ref[t]
        ps_ref[t] = p
        taps = hist + [p]
        z = taps[0] * cw[0:1]
        for j in range(1, width):
            z = z + taps[j] * cw[j:j + 1]
        y_ref[t] = (bg_ref[t] * z).astype(y_ref.dtype)
        hist = hist[1:] + [p]


def sconv_sample(proj_t, st, cw, *, width, cbw=512):
    seg, n_seq, _ = proj_t.shape
    nb = width // cbw
    return pl.pallas_call(
        functools.partial(_sconv_sample_body, seg=seg),
        grid=(nb,),
        in_specs=[
            pl.BlockSpec((seg, n_seq, cbw), lambda c: (0, 0, c)),
            pl.BlockSpec((seg, n_seq, cbw), lambda c: (0, 0, nb + c)),
            pl.BlockSpec((seg, n_seq, cbw), lambda c: (0, 0, 2 * nb + c)),
            pl.BlockSpec((st.shape[0], n_seq, cbw), lambda c: (0, 0, c)),
            pl.BlockSpec((cw.shape[0], cbw), lambda c: (0, c)),
        ],
        out_specs=[
            pl.BlockSpec((seg, n_seq, cbw), lambda c: (0, 0, c)),
            pl.BlockSpec((seg, n_seq, cbw), lambda c: (0, 0, c)),
        ],
        out_shape=[
            jax.ShapeDtypeStruct((seg, n_seq, width), BF16),
            jax.ShapeDtypeStruct((seg, n_seq, width), F32),
        ],
        compiler_params=_cparams("parallel"),
        name="sconv_sample",
    )(proj_t, proj_t, proj_t, st, cw)


def _softmax_rows(s):
    e = jnp.exp(s - jnp.max(s, axis=-1, keepdims=True))
    return e / jnp.sum(e, axis=-1, keepdims=True)


def _xattn_prompt_body(q_ref, k_ref, v_ref, o_ref, *, scale):
    s = lax.dot_general(q_ref[...], k_ref[...].astype(BF16), (((1,), (1,)), ((), ())), preferred_element_type=F32) * scale
    p = _softmax_rows(s).astype(BF16)
    o_ref[...] = jnp.dot(p, v_ref[...].astype(BF16), preferred_element_type=F32).astype(o_ref.dtype)


def xattn_prompt(q, mk, mv, *, batch, seq, n_mem, n_heads, hd, tq=1024):
    nq = seq // tq
    return pl.pallas_call(
        functools.partial(_xattn_prompt_body, scale=hd ** -0.5),
        grid=(batch, n_heads, nq),
        in_specs=[
            pl.BlockSpec((tq, hd), lambda b, h, i: (b * nq + i, h)),
            pl.BlockSpec((n_mem, hd), lambda b, h, i: (b, h)),
            pl.BlockSpec((n_mem, hd), lambda b, h, i: (b, h)),
        ],
        out_specs=pl.BlockSpec((tq, hd), lambda b, h, i: (b * nq + i, h)),
        out_shape=jax.ShapeDtypeStruct((batch * seq, n_heads * hd), BF16),
        compiler_params=_cparams("parallel", "parallel", "parallel"),
        name="xattn_prompt",
    )(q, mk, mv)


def _xattn_sample_body(q_ref, k_ref, v_ref, o_ref, *, n_heads, scale):
    n_mem, hd = k_ref.shape[2], k_ref.shape[4]
    k2 = k_ref[0, 0].reshape(n_mem * n_heads, hd).astype(BF16)
    v2 = v_ref[0, 0].reshape(n_mem * n_heads, hd).astype(BF16)
    s = lax.dot_general(q_ref[0], k2, (((1,), (1,)), ((), ())), preferred_element_type=F32) * scale
    row = lax.broadcasted_iota(jnp.int32, s.shape, 0)
    col = lax.broadcasted_iota(jnp.int32, s.shape, 1)
    s = jnp.where((row % n_heads) == (col % n_heads), s, -jnp.inf)
    p = _softmax_rows(s).astype(BF16)
    o_ref[0] = jnp.dot(p, v2, preferred_element_type=F32).astype(o_ref.dtype)


def xattn_sample(q3, cache_k, cache_v, layer):
    n_seq, rows, hd = q3.shape
    _, _, n_mem, n_heads, _ = cache_k.shape
    kv_spec = pl.BlockSpec((1, 1, n_mem, n_heads, hd), lambda b: (layer, b, 0, 0, 0))
    return pl.pallas_call(
        functools.partial(_xattn_sample_body, n_heads=n_heads, scale=hd ** -0.5),
        grid=(n_seq,),
        in_specs=[pl.BlockSpec((1, rows, hd), lambda b: (b, 0, 0)), kv_spec, kv_spec],
        out_specs=pl.BlockSpec((1, rows, hd), lambda b: (b, 0, 0)),
        out_shape=jax.ShapeDtypeStruct((n_seq, rows, hd), BF16),
        compiler_params=_cparams("parallel"),
        name="xattn_sample",
    )(q3, cache_k, cache_v)


def kernel(x_prompt, x_sample, mem_prompt, cache_mem_k, cache_mem_v, state_mlstm_c, state_mlstm_n, state_mlstm_m, state_lru_h, state_lru_conv, state_sconv, norm_mix, norm_x, norm_ff, norm_final, w_in_e, b_if, mh_norm, conv_b_w, conv_b_b, lru_wa, lru_ba, lru_wi, lru_bi, lru_lam, w_out_e, w_in_o, conv_c_w, w_out_o, w_xq, w_xk, w_xv, w_xo, w_ff1, w_ff2):
    bp, sp, d = x_prompt.shape
    bs, ss, _ = x_sample.shape
    n_mem = mem_prompt.shape[1]
    depth, _, _, xh, xhd = cache_mem_k.shape
    _, _, ah, dk, dv = state_mlstm_c.shape
    bw = state_lru_h.shape[-1]
    bconv = state_lru_conv.shape[2] + 1
    cconv = state_sconv.shape[2] + 1
    cwid = state_sconv.shape[-1]
    aw = ah * dv
    aqk = ah * dk
    tp = bp * sp
    ts = bs * ss
    t = tp + ts
    TM = 544
    TN = 1024
    assert t % TM == 0 and tp % 512 == 0 and ts == 512

    x = jnp.concatenate([x_prompt.reshape(tp, d), x_sample.reshape(ts, d)], axis=0)
    mem_bf = mem_prompt.reshape(bp * n_mem, d).astype(BF16)
    w_out_e_bf, w_in_o_bf, w_out_o_bf = w_out_e.astype(BF16), w_in_o.astype(BF16), w_out_o.astype(BF16)
    w_xq_bf, w_xk_bf, w_xv_bf, w_xo_bf = w_xq.astype(BF16), w_xk.astype(BF16), w_xv.astype(BF16), w_xo.astype(BF16)
    w_ff1_bf, w_ff2_bf = w_ff1.astype(BF16), w_ff2.astype(BF16)

    outs = {}
    ie = io = 0
    p_mem_k, p_mem_v = [], []
    for l in range(depth):
        u = rmsnorm(x, norm_mix[l])
        if l % 2 == 0:
            w = w_in_e[ie]
            o_qkvo = 2 * aqk + 2 * aw
            o_if = o_qkvo + 2 * ah
            w_main = jnp.concatenate([w[:, :o_qkvo], w[:, o_if:]], axis=1).astype(BF16)
            w_if = jnp.pad(w[:, o_qkvo:o_if], ((0, 0), (0, V7X_LANES - 2 * ah))).astype(BF16)
            bias_if = jnp.pad(b_if[ie], (0, V7X_LANES - 2 * ah)).reshape(1, V7X_LANES)
            col_q, col_k, col_v, col_o = 0, aqk, 2 * aqk, 2 * aqk + aw
            col_x, col_g = o_qkvo, o_qkvo + bw
            proj = matmul([u], [w_main], tm=TM, tn=TN, name="in_proj_even")
            gc, gr = mlstm_gates(u, w_if, bias_if, n_prompt_rows=tp, seg_sample=ss)
            gr = gr.reshape(16, 1, t)
            mh = mh_norm[ie].reshape(1, aw)
            mkw = dict(n_heads=ah, dk=dk, dv=dv, col_q=col_q, col_k=col_k, col_v=col_v, col_o=col_o)
            ya_p, pc, pn, pm = mlstm_prompt(proj, gc, gr, mh, batch=bp, seq=sp, **mkw)
            n0_tok = jnp.repeat(jnp.transpose(state_mlstm_n[ie], (1, 0, 2)), ss, axis=1)
            m0_tok = jnp.broadcast_to(jnp.repeat(state_mlstm_m[ie].T, ss, axis=1)[:, :, None], (ah, ts, MLSTM_CHUNK))
            ya_s, sc_, sn_tok, sm_tok = mlstm_sample(proj, gc, gr, mh, state_mlstm_c[ie:ie + 1], n0_tok, m0_tok,
                                                     row0=tp, n_seq=bs, seg=ss, **mkw)
            outs.setdefault("pc", []).append(pc.reshape(bp, ah, dk, dv))
            outs.setdefault("pn", []).append(pn.reshape(bp, ah, dk))
            outs.setdefault("pm", []).append(pm.reshape(bp, ah))
            outs.setdefault("sc", []).append(sc_[0])
            outs.setdefault("sn", []).append(jnp.transpose(sn_tok[:, ss - 1::ss, :], (1, 0, 2)))
            outs.setdefault("sm", []).append(sm_tok[:, ss - 1::ss, 0].T)

            cw = conv_b_w[ie]
            cbias = conv_b_b[ie].reshape(1, bw)
            wa = lru_wa[ie].astype(BF16)
            wi = lru_wi[ie].astype(BF16)
            ba = lru_ba[ie].reshape(1, bw)
            bi = lru_bi[ie].reshape(1, bw)
            lam = lru_lam[ie].reshape(1, bw)
            halo_p = jnp.zeros((bp, V7X_SUBLANES, bw), F32)
            h0_p = jnp.zeros((bp, 1, bw), F32)
            yb_p, ph = lru_prompt(proj, halo_p, h0_p, cw, cbias, wa, wi, ba, bi, lam,
                                  batch=bp, seq=sp, width=bw, col_x=col_x, col_g=col_g)
            st_s = jnp.transpose(state_lru_conv[ie], (1, 0, 2))
            xb_p = proj[:tp, col_x:col_x + bw].reshape(bp, sp, bw)
            xb_s = proj[tp:, col_x:col_x + bw].reshape(bs, ss, bw)
            gb_s = proj[tp:, col_g:col_g + bw].reshape(bs, ss, bw)
            yb_s, sh = lru_sample(jnp.transpose(xb_s, (1, 0, 2)), jnp.transpose(gb_s, (1, 0, 2)), st_s,
                                  state_lru_h[ie], cw, cbias, wa, wi, ba, bi, lam)
            yb_s = jnp.transpose(yb_s, (1, 0, 2)).reshape(ts, bw)
            outs.setdefault("ph", []).append(ph.reshape(bp, bw))
            outs.setdefault("pcb", []).append(xb_p[:, sp - (bconv - 1):])
            outs.setdefault("sh", []).append(sh)
            outs.setdefault("scb", []).append(jnp.concatenate([state_lru_conv[ie], xb_s], axis=1)[:, ss:])

            ya = jnp.concatenate([ya_p, ya_s], axis=0)
            yb = jnp.concatenate([yb_p, yb_s], axis=0)
            x = matmul([ya, yb], [(w_out_e_bf, ie, 0), (w_out_e_bf, ie, 1)], res=x, tm=TM, tn=TN, name="out_proj_even")
            ie += 1
        else:
            proj = matmul([u], [(w_in_o_bf, io, 0)], tm=TM, tn=TN, name="in_proj_odd")
            cw = conv_c_w[io]
            halo_p = jnp.zeros((bp, V7X_SUBLANES, cwid), F32)
            y_p, tail = sconv_prompt(proj, halo_p, cw, batch=bp, seq=sp, width=cwid)
            st_s = jnp.transpose(state_sconv[io], (1, 0, 2))
            proj_t = jnp.transpose(proj[tp:].reshape(bs, ss, 3 * cwid), (1, 0, 2))
            y_s, ps = sconv_sample(proj_t, st_s, cw, width=cwid)
            y_s = jnp.transpose(y_s, (1, 0, 2)).reshape(ts, cwid)
            outs.setdefault("psb", []).append(tail[:, V7X_SUBLANES - (cconv - 1):])
            ps_b = jnp.transpose(ps, (1, 0, 2))
            outs.setdefault("ssb", []).append(jnp.concatenate([state_sconv[io], ps_b], axis=1)[:, ss:])
            y = jnp.concatenate([y_p, y_s], axis=0)
            x = matmul([y], [(w_out_o_bf, io, 0)], res=x, tm=TM, tn=TN, name="out_proj_odd")
            io += 1

        u = rmsnorm(x, norm_x[l])
        q = matmul([u], [(w_xq_bf, l, 0)], out_dtype=BF16, tm=TM, tn=TN, name="xattn_q")
        mk = matmul([mem_bf], [(w_xk_bf, l, 0)], tm=512, tn=TN, name="mem_k")
        mv = matmul([mem_bf], [(w_xv_bf, l, 0)], tm=512, tn=TN, name="mem_v")
        p_mem_k.append(mk.reshape(bp, n_mem, xh, xhd))
        p_mem_v.append(mv.reshape(bp, n_mem, xh, xhd))
        o_p = xattn_prompt(q, mk, mv, batch=bp, seq=sp, n_mem=n_mem, n_heads=xh, hd=xhd)
        o_s = xattn_sample(q[tp:].reshape(bs, ss * xh, xhd), cache_mem_k, cache_mem_v, l)
        o = jnp.concatenate([o_p, o_s.reshape(ts, d)], axis=0)
        x = matmul([o], [(w_xo_bf, l, 0)], res=x, tm=TM, tn=TN, name="xattn_o")

        u = rmsnorm(x, norm_ff[l])
        hmid = matmul([u], [(w_ff1_bf, l, 0)], act="relu2", out_dtype=BF16, tm=TM, tn=TN, name="ff1")
        x = matmul_ksplit(hmid, w_ff2_bf, l, x, tm=TM, tn=TN, tk=4096, name="ff2")

    y_p = rmsnorm(x, norm_final, out_dtype=F32, row0=0, nrows=tp)
    y_s = rmsnorm(x, norm_final, out_dtype=F32, row0=tp, nrows=ts)
    dt = x_prompt.dtype
    st = lambda name: jnp.stack(outs[name]).astype(dt)
    return (y_p.reshape(bp, sp, d), y_s.reshape(bs, ss, d),
            st("pc"), st("pn"), st("pm"), st("ph"), st("pcb"), st("psb"),
            jnp.stack(p_mem_k), jnp.stack(p_mem_v),
            st("sc"), st("sn"), st("sm"), st("sh"), st("scb"), st("ssb"))
```

```python
import functools

import jax
import jax.numpy as jnp
from jax import lax
from jax.experimental import pallas as pl
from jax.experimental.pallas import tpu as pltpu

F32 = jnp.float32
BF16 = jnp.bfloat16
EPS = 1e-6
LRU_C = 8.0

V7X_LANES = 128
V7X_SUBLANES = 8
V7X_VMEM_BUDGET_BYTES = 56 * 1024 * 1024

MLSTM_CHUNK = 128
SCAN_CHUNK = 256


def _cparams(*semantics):
    return pltpu.CompilerParams(dimension_semantics=semantics, vmem_limit_bytes=V7X_VMEM_BUDGET_BYTES)


def _log_sigmoid(x):
    return jnp.minimum(x, 0.0) - jnp.log1p(jnp.exp(-jnp.abs(x)))


def _sigmoid(x):
    return 1.0 / (1.0 + jnp.exp(-x))


def _gelu_tanh(x):
    c = 0.7978845608028654
    return 0.5 * x * (1.0 + jnp.tanh(c * (x + 0.044715 * (x * x * x))))


def _expm1(x):
    u = jnp.exp(x)
    um1 = u - 1.0
    safe = jnp.where(um1 == 0.0, 1.0, jnp.log(u))
    return jnp.where(um1 == 0.0, x, jnp.where(um1 == -1.0, -1.0, um1 * x / safe))


def _split3(x):
    hi = x.astype(BF16)
    r1 = x - hi.astype(F32)
    mid = r1.astype(BF16)
    lo = (r1 - mid.astype(F32)).astype(BF16)
    return hi, mid, lo


def _dot01(mask_bf16, x):
    hi, mid, lo = _split3(x)
    acc = jnp.dot(mask_bf16, lo, preferred_element_type=F32)
    acc = acc + jnp.dot(mask_bf16, mid, preferred_element_type=F32)
    return acc + jnp.dot(mask_bf16, hi, preferred_element_type=F32)


def _rmsnorm_body(x_ref, g_ref, o_ref):
    x = x_ref[...]
    inv = lax.rsqrt(jnp.mean(x * x, axis=-1, keepdims=True) + EPS)
    o_ref[...] = ((x * inv) * g_ref[...]).astype(o_ref.dtype)


def rmsnorm(x, g, out_dtype=BF16, rows=256, row0=0, nrows=None):
    d = x.shape[1]
    nrows = x.shape[0] if nrows is None else nrows
    b0 = row0 // rows
    return pl.pallas_call(
        _rmsnorm_body,
        grid=(nrows // rows,),
        in_specs=[pl.BlockSpec((rows, d), lambda i: (b0 + i, 0)), pl.BlockSpec((1, d), lambda i: (0, 0))],
        out_specs=pl.BlockSpec((rows, d), lambda i: (i, 0)),
        out_shape=jax.ShapeDtypeStruct((nrows, d), out_dtype),
        compiler_params=_cparams("parallel"),
        name="rmsnorm",
    )(x, g.reshape(1, d))


def _mm_body(*refs, n_in, has_res, act):
    n_io = 2 * n_in + (1 if has_res else 0)
    o_ref = refs[n_io]
    wbf_refs = refs[n_io + 1:]

    @pl.when(pl.program_id(1) == 0)
    def _():
        for k in range(n_in):
            wbf_refs[k][...] = refs[n_in + k][...].astype(BF16)

    acc = jnp.dot(refs[0][...], wbf_refs[0][...], preferred_element_type=F32)
    for k in range(1, n_in):
        acc = acc + jnp.dot(refs[k][...], wbf_refs[k][...], preferred_element_type=F32)
    if act == "relu2":
        r = jnp.maximum(acc, 0.0)
        acc = r * r
    if has_res:
        acc = refs[2 * n_in][...] + acc
    o_ref[...] = acc.astype(o_ref.dtype)


def matmul(xs, ws, *, res=None, act=None, out_dtype=F32, n_out=None, tm, tn, name):
    m = xs[0].shape[0]
    n = n_out if n_out is not None else (ws[0][0] if isinstance(ws[0], tuple) else ws[0]).shape[-1]
    n_in = len(xs)
    in_specs = [pl.BlockSpec((tm, x.shape[1]), lambda j, i: (i, 0)) for x in xs]
    args = list(xs)
    for x, w in zip(xs, ws):
        if isinstance(w, tuple):
            w, layer, kb = w
            in_specs.append(pl.BlockSpec((None, x.shape[1], tn), lambda j, i, layer=layer, kb=kb: (layer, kb, j)))
        else:
            in_specs.append(pl.BlockSpec((w.shape[0], tn), lambda j, i: (0, j)))
        args.append(w)
    if res is not None:
        in_specs.append(pl.BlockSpec((tm, tn), lambda j, i: (i, j)))
        args.append(res)
    return pl.pallas_call(
        functools.partial(_mm_body, n_in=n_in, has_res=res is not None, act=act),
        grid=(n // tn, m // tm),
        in_specs=in_specs,
        out_specs=pl.BlockSpec((tm, tn), lambda j, i: (i, j)),
        out_shape=jax.ShapeDtypeStruct((m, n), out_dtype),
        scratch_shapes=[pltpu.VMEM((x.shape[1], tn), BF16) for x in xs],
        compiler_params=_cparams("parallel", "arbitrary"),
        name=name,
    )(*args)


def _mm_acc_body(x_ref, w_ref, res_ref, o_ref, acc_ref):
    k = pl.program_id(2)

    @pl.when(k == 0)
    def _():
        acc_ref[...] = res_ref[...]

    acc_ref[...] += jnp.dot(x_ref[...], w_ref[...].astype(BF16), preferred_element_type=F32)

    @pl.when(k == pl.num_programs(2) - 1)
    def _():
        o_ref[...] = acc_ref[...]


def matmul_ksplit(x, w, layer, res, *, tm, tn, tk, name):
    m, kk = x.shape
    n = w.shape[-1]
    return pl.pallas_call(
        _mm_acc_body,
        grid=(m // tm, n // tn, kk // tk),
        in_specs=[
            pl.BlockSpec((tm, tk), lambda i, j, k: (i, k)),
            pl.BlockSpec((None, tk, tn), lambda i, j, k: (layer, k, j)),
            pl.BlockSpec((tm, tn), lambda i, j, k: (i, j)),
        ],
        out_specs=pl.BlockSpec((tm, tn), lambda i, j, k: (i, j)),
        out_shape=jax.ShapeDtypeStruct((m, n), F32),
        scratch_shapes=[pltpu.VMEM((tm, tn), F32)],
        compiler_params=_cparams("parallel", "parallel", "arbitrary"),
        name=name,
    )(x, w, res)


def _gates_body(u_ref, w_ref, b_ref, gc_ref, gr_ref, *, tm, n_prompt_tiles, seg_sample):
    L = MLSTM_CHUNK
    g = jnp.dot(u_ref[...], w_ref[...], preferred_element_type=F32) + b_ref[...]
    lane = lax.broadcasted_iota(jnp.int32, (tm, V7X_LANES), 1)
    val = jnp.where(lane < 8, g, _log_sigmoid(g))
    row = lax.broadcasted_iota(jnp.int32, (L, L), 0)
    col = lax.broadcasted_iota(jnp.int32, (L, L), 1)
    is_sample = pl.program_id(0) >= n_prompt_tiles
    tril = jnp.where(col <= row, 1.0, 0.0)
    tril_seg = jnp.where((col <= row) & ((row // seg_sample) == (col // seg_sample)), 1.0, 0.0)
    mask = jnp.where(is_sample, tril_seg, tril).astype(BF16)
    lane_l = lax.broadcasted_iota(jnp.int32, (L, V7X_LANES), 1)
    for s in range(tm // L):
        x = val[s * L:(s + 1) * L]
        out = jnp.where(lane_l < 8, x, _dot01(mask, x))
        gr_ref[:, s * L:(s + 1) * L] = out.T[:16]
        for c in range(16):
            gc_ref[c, s * L:(s + 1) * L, :] = jnp.broadcast_to(out[:, c:c + 1], (L, V7X_LANES))


def mlstm_gates(u, w_if, b_if, *, n_prompt_rows, seg_sample, tm=512):
    t, d = u.shape
    return pl.pallas_call(
        functools.partial(_gates_body, tm=tm, n_prompt_tiles=n_prompt_rows // tm, seg_sample=seg_sample),
        grid=(t // tm,),
        in_specs=[
            pl.BlockSpec((tm, d), lambda i: (i, 0)),
            pl.BlockSpec((d, V7X_LANES), lambda i: (0, 0)),
            pl.BlockSpec((1, V7X_LANES), lambda i: (0, 0)),
        ],
        out_specs=[
            pl.BlockSpec((16, tm, V7X_LANES), lambda i: (0, i, 0)),
            pl.BlockSpec((16, tm), lambda i: (0, i)),
        ],
        out_shape=[
            jax.ShapeDtypeStruct((16, t, V7X_LANES), F32),
            jax.ShapeDtypeStruct((16, t), F32),
        ],
        compiler_params=_cparams("parallel"),
        name="mlstm_gates",
    )(u, w_if, b_if)


def _head_out(hh, o, mh):
    hn = hh * lax.rsqrt(jnp.mean(hh * hh, axis=-1, keepdims=True) + EPS)
    return _sigmoid(o) * (hn * mh)


def _mlstm_prompt_body(q_ref, k_ref, v_ref, o_ref, ic_ref, bc_ref, ir_ref, br_ref, mh_ref,
                       ya_ref, c_ref, n_ref, m_ref, c_scr, n_scr, m_scr, *, n_heads, seq, scale):
    L = MLSTM_CHUNK
    c_scr[...] = jnp.zeros_like(c_scr)
    n_scr[...] = jnp.zeros_like(n_scr)
    m_scr[...] = jnp.zeros_like(m_scr)
    row = lax.broadcasted_iota(jnp.int32, (L, L), 0)
    col = lax.broadcasted_iota(jnp.int32, (L, L), 1)
    causal = col <= row

    def chunk(ci, carry):
        r0 = pl.multiple_of(ci * L, L)
        q = q_ref[pl.ds(r0, L), :] * scale
        k = k_ref[pl.ds(r0, L), :]
        v = v_ref[pl.ds(r0, L), :]
        bc = bc_ref[0, pl.ds(r0, L), :]
        ic = ic_ref[0, pl.ds(r0, L), :]
        i_row = ir_ref[0, :, pl.ds(r0, L)]
        b_row = br_ref[0, :, pl.ds(r0, L)]
        m_prev = m_scr[...]
        c_prev = c_scr[...]
        n_prev = n_scr[...]
        dmat = jnp.where(causal, bc - b_row + i_row, -jnp.inf)
        g = bc + m_prev
        mt = jnp.maximum(g, jnp.max(dmat, axis=-1, keepdims=True))
        w_inter = jnp.exp(g - mt)
        qb = q.astype(BF16)
        kb = k.astype(BF16)
        vb = v.astype(BF16)
        sc = lax.dot_general(qb, kb, (((1,), (1,)), ((), ())), preferred_element_type=F32) * jnp.exp(dmat - mt)
        inter = jnp.dot(qb, c_prev.astype(BF16), preferred_element_type=F32)
        num = jnp.concatenate([w_inter, w_inter], axis=1) * inter + jnp.dot(sc.astype(BF16), vb, preferred_element_type=F32)
        qn = jnp.sum(q * n_prev, axis=-1, keepdims=True)
        den = w_inter[:, :1] * qn + jnp.sum(sc, axis=-1, keepdims=True)
        hh = num / jnp.maximum(jnp.abs(den), jnp.exp(-mt[:, :1]))
        ya_ref[pl.ds(r0, L), :] = _head_out(hh, o_ref[pl.ds(r0, L), :], mh_ref[...]).astype(ya_ref.dtype)
        m_new = mt[L - 1:L, :1]
        b_last = b_row[:, L - 1:L]
        decay = jnp.exp(b_last + m_prev - m_new)
        kw = k * jnp.exp(b_last - bc + ic - m_new)
        c_scr[...] = decay * c_prev + lax.dot_general(kw.astype(BF16), vb, (((0,), (0,)), ((), ())), preferred_element_type=F32)
        n_scr[...] = decay * n_prev + jnp.sum(kw, axis=0, keepdims=True)
        m_scr[...] = m_new
        return carry

    lax.fori_loop(0, seq // L, chunk, 0)
    c_ref[0, 0] = c_scr[...]
    n_ref[0, 0] = n_scr[...]
    m_ref[0, 0] = m_scr[...]


def mlstm_prompt(proj, gc, gr, mh_norm, *, batch, seq, n_heads, dk, dv, col_q, col_k, col_v, col_o):
    L = MLSTM_CHUNK
    assert dk == L and dv == 2 * L
    rows = batch * seq
    return pl.pallas_call(
        functools.partial(_mlstm_prompt_body, n_heads=n_heads, seq=seq, scale=dk ** -0.5),
        grid=(batch, n_heads),
        in_specs=[
            pl.BlockSpec((seq, dk), lambda b, h: (b, col_q // dk + h)),
            pl.BlockSpec((seq, dk), lambda b, h: (b, col_k // dk + h)),
            pl.BlockSpec((seq, dv), lambda b, h: (b, col_v // dv + h)),
            pl.BlockSpec((seq, dv), lambda b, h: (b, col_o // dv + h)),
            pl.BlockSpec((1, seq, V7X_LANES), lambda b, h: (h, b, 0)),
            pl.BlockSpec((1, seq, V7X_LANES), lambda b, h: (h + n_heads, b, 0)),
            pl.BlockSpec((1, 1, seq), lambda b, h: (h, 0, b)),
            pl.BlockSpec((1, 1, seq), lambda b, h: (h + n_heads, 0, b)),
            pl.BlockSpec((1, dv), lambda b, h: (0, h)),
        ],
        out_specs=[
            pl.BlockSpec((seq, dv), lambda b, h: (b, h)),
            pl.BlockSpec((1, 1, dk, dv), lambda b, h: (b, h, 0, 0)),
            pl.BlockSpec((1, 1, 1, dk), lambda b, h: (b, h, 0, 0)),
            pl.BlockSpec((1, 1, 1, 1), lambda b, h: (b, h, 0, 0)),
        ],
        out_shape=[
            jax.ShapeDtypeStruct((rows, n_heads * dv), BF16),
            jax.ShapeDtypeStruct((batch, n_heads, dk, dv), F32),
            jax.ShapeDtypeStruct((batch, n_heads, 1, dk), F32),
            jax.ShapeDtypeStruct((batch, n_heads, 1, 1), F32),
        ],
        scratch_shapes=[pltpu.VMEM((dk, dv), F32), pltpu.VMEM((1, dk), F32), pltpu.VMEM((1, 1), F32)],
        compiler_params=_cparams("parallel", "parallel"),
        name="mlstm_prompt",
    )(proj, proj, proj, proj, gc, gc, gr, gr, mh_norm)


def _mlstm_sample_body(q_ref, k_ref, v_ref, o_ref, ic_ref, bc_ref, ir_ref, br_ref, mh_ref, c0_ref, n0_ref, m0_ref,
                       ya_ref, c_ref, n_ref, m_ref, inter_scr, *, n_heads, seg, scale):
    L = MLSTM_CHUNK
    nb = L // seg
    row = lax.broadcasted_iota(jnp.int32, (L, L), 0)
    col = lax.broadcasted_iota(jnp.int32, (L, L), 1)
    same = (row // seg) == (col // seg)
    causal = (col <= row) & same
    last = col == (row // seg) * seg + (seg - 1)
    q = q_ref[...] * scale
    k = k_ref[...]
    v = v_ref[...]
    bc = bc_ref[0]
    ic = ic_ref[0]
    i_row = ir_ref[0]
    b_row = br_ref[0]
    m_prev = m0_ref[0]
    n_prev = n0_ref[0]
    dmat = jnp.where(causal, bc - b_row + i_row, -jnp.inf)
    g = bc + m_prev
    mt = jnp.maximum(g, jnp.max(dmat, axis=-1, keepdims=True))
    w_inter = jnp.exp(g - mt)
    qb = q.astype(BF16)
    kb = k.astype(BF16)
    vb = v.astype(BF16)
    sc = lax.dot_general(qb, kb, (((1,), (1,)), ((), ())), preferred_element_type=F32) * jnp.exp(dmat - mt)

    per_group = V7X_SUBLANES // seg
    sub = lax.broadcasted_iota(jnp.int32, (V7X_SUBLANES, 2 * L), 0)
    for gi in range(L // V7X_SUBLANES):
        q8 = qb[gi * V7X_SUBLANES:(gi + 1) * V7X_SUBLANES]
        acc = jnp.zeros((V7X_SUBLANES, 2 * L), F32)
        for j in range(per_group):
            r = jnp.dot(q8, c0_ref[0, gi * per_group + j, 0].astype(BF16), preferred_element_type=F32)
            acc = jnp.where((sub // seg) == j, r, acc)
        inter_scr[gi * V7X_SUBLANES:(gi + 1) * V7X_SUBLANES, :] = acc
    inter = inter_scr[...]

    num = jnp.concatenate([w_inter, w_inter], axis=1) * inter + jnp.dot(sc.astype(BF16), vb, preferred_element_type=F32)
    qn = jnp.sum(q * n_prev, axis=-1, keepdims=True)
    den = w_inter[:, :1] * qn + jnp.sum(sc, axis=-1, keepdims=True)
    hh = num / jnp.maximum(jnp.abs(den), jnp.exp(-mt[:, :1]))
    ya_ref[...] = _head_out(hh, o_ref[...], mh_ref[...]).astype(ya_ref.dtype)

    b_last = jnp.sum(jnp.where(last, jnp.broadcast_to(b_row, (L, L)), 0.0), axis=-1, keepdims=True)
    mt_row = mt.T
    m_new = jnp.sum(jnp.where(last, mt_row, 0.0), axis=-1, keepdims=True)
    decay = jnp.exp(b_last + m_prev - m_new)
    kw = k * jnp.exp(b_last - bc + ic - m_new)
    seg_ones = jnp.where(same, 1.0, 0.0).astype(BF16)
    n_ref[0] = decay * n_prev + _dot01(seg_ones, kw)
    m_ref[0] = jnp.broadcast_to(m_new, (L, L))
    rowk = lax.broadcasted_iota(jnp.int32, (L, L), 0)
    for bi in range(nb):
        kw_b = jnp.where((rowk // seg) == bi, kw, 0.0).astype(BF16)
        upd = lax.dot_general(kw_b, vb, (((0,), (0,)), ((), ())), preferred_element_type=F32)
        d_b = decay[bi * seg:bi * seg + 1, :1]
        c_ref[0, bi, 0] = d_b * c0_ref[0, bi, 0] + upd


def mlstm_sample(proj, gc, gr, mh_norm, c0, n0_tok, m0_tok, *, row0, n_seq, seg, n_heads, dk, dv,
                 col_q, col_k, col_v, col_o):
    L = MLSTM_CHUNK
    assert dk == L and dv == 2 * L and V7X_SUBLANES % seg == 0 and row0 % L == 0
    rows = n_seq * seg
    nb = L // seg
    t0 = row0 // L
    return pl.pallas_call(
        functools.partial(_mlstm_sample_body, n_heads=n_heads, seg=seg, scale=dk ** -0.5),
        grid=(rows // L, n_heads),
        in_specs=[
            pl.BlockSpec((L, dk), lambda j, h: (t0 + j, col_q // dk + h)),
            pl.BlockSpec((L, dk), lambda j, h: (t0 + j, col_k // dk + h)),
            pl.BlockSpec((L, dv), lambda j, h: (t0 + j, col_v // dv + h)),
            pl.BlockSpec((L, dv), lambda j, h: (t0 + j, col_o // dv + h)),
            pl.BlockSpec((1, L, V7X_LANES), lambda j, h: (h, t0 + j, 0)),
            pl.BlockSpec((1, L, V7X_LANES), lambda j, h: (h + n_heads, t0 + j, 0)),
            pl.BlockSpec((1, 1, L), lambda j, h: (h, 0, t0 + j)),
            pl.BlockSpec((1, 1, L), lambda j, h: (h + n_heads, 0, t0 + j)),
            pl.BlockSpec((1, dv), lambda j, h: (0, h)),
            pl.BlockSpec((1, nb, 1, dk, dv), lambda j, h: (0, j, h, 0, 0)),
            pl.BlockSpec((1, L, dk), lambda j, h: (h, j, 0)),
            pl.BlockSpec((1, L, L), lambda j, h: (h, j, 0)),
        ],
        out_specs=[
            pl.BlockSpec((L, dv), lambda j, h: (j, h)),
            pl.BlockSpec((1, nb, 1, dk, dv), lambda j, h: (0, j, h, 0, 0)),
            pl.BlockSpec((1, L, dk), lambda j, h: (h, j, 0)),
            pl.BlockSpec((1, L, L), lambda j, h: (h, j, 0)),
        ],
        out_shape=[
            jax.ShapeDtypeStruct((rows, n_heads * dv), BF16),
            jax.ShapeDtypeStruct((1, n_seq, n_heads, dk, dv), F32),
            jax.ShapeDtypeStruct((n_heads, rows, dk), F32),
            jax.ShapeDtypeStruct((n_heads, rows, L), F32),
        ],
        scratch_shapes=[pltpu.VMEM((L, dv), F32)],
        compiler_params=_cparams("parallel", "parallel"),
        name="mlstm_sample",
    )(proj, proj, proj, proj, gc, gc, gr, gr, mh_norm, c0, n0_tok, m0_tok)


def _lru_gates(xc, wa_ref, wi_ref, ba, bi, lam):
    nblk = wa_ref.shape[0]
    blk = wa_ref.shape[1]
    rs, igs = [], []
    for n in range(nblk):
        xb = xc[:, n * blk:(n + 1) * blk].astype(BF16)
        rs.append(jnp.dot(xb, wa_ref[n], preferred_element_type=F32))
        igs.append(jnp.dot(xb, wi_ref[n], preferred_element_type=F32))
    r = _sigmoid(jnp.concatenate(rs, axis=1) + ba)
    ig = _sigmoid(jnp.concatenate(igs, axis=1) + bi)
    log_a = LRU_C * r * _log_sigmoid(lam)
    a = jnp.exp(log_a)
    u = jnp.sqrt(-_expm1(2.0 * log_a)) * (ig * xc)
    return a, u


def _lru_prompt_body(x_ref, g_ref, halo_ref, h0_ref, cw_ref, cb_ref, wa_ref, wi_ref, ba_ref, bi_ref, lam_ref,
                     y_ref, hl_ref, h_scr, *, seq):
    Lc = SCAN_CHUNK
    cbw = x_ref.shape[1]
    h_scr[...] = h0_ref[0]
    row = lax.broadcasted_iota(jnp.int32, (Lc, cbw), 0)
    cw = cw_ref[...]
    width = cw.shape[0]

    def chunk(ci, carry):
        r0 = pl.multiple_of(ci * Lc, Lc)
        x = x_ref[pl.ds(r0, Lc), :]
        prev = x_ref[pl.ds(pl.multiple_of(jnp.maximum(r0 - V7X_SUBLANES, 0), V7X_SUBLANES), V7X_SUBLANES), :]
        prev = jnp.where(ci == 0, halo_ref[0], prev)
        xfull = jnp.concatenate([prev, x], axis=0)
        xc = None
        for j in range(width - 1, 0, -1):
            term = pltpu.roll(xfull, j, 0)[V7X_SUBLANES:] * cw[width - 1 - j:width - j]
            xc = term if xc is None else xc + term
        xc = xc + x * cw[width - 1:width] + cb_ref[...]
        a, u = _lru_gates(xc, wa_ref, wi_ref, ba_ref[...], bi_ref[...], lam_ref[...])
        d = 1
        while d < Lc:
            ok = row >= d
            a_sh = jnp.where(ok, pltpu.roll(a, d, 0), 1.0)
            u_sh = jnp.where(ok, pltpu.roll(u, d, 0), 0.0)
            u = a * u_sh + u
            a = a * a_sh
            d *= 2
        hs = u + a * h_scr[...]
        h_scr[...] = hs[Lc - 1:Lc]
        y_ref[pl.ds(r0, Lc), :] = (hs * _gelu_tanh(g_ref[pl.ds(r0, Lc), :])).astype(y_ref.dtype)
        return carry

    lax.fori_loop(0, seq // Lc, chunk, 0)
    hl_ref[0] = h_scr[...]


def lru_prompt(proj, halo, h0, cw, cb, wa, wi, ba, bi, lam, *, batch, seq, width, col_x, col_g, cbw=256):
    blk = wa.shape[1]
    nper = cbw // blk
    vec = lambda b, c: (0, c)
    return pl.pallas_call(
        functools.partial(_lru_prompt_body, seq=seq),
        grid=(batch, width // cbw),
        in_specs=[
            pl.BlockSpec((seq, cbw), lambda b, c: (b, col_x // cbw + c)),
            pl.BlockSpec((seq, cbw), lambda b, c: (b, col_g // cbw + c)),
            pl.BlockSpec((1, V7X_SUBLANES, cbw), lambda b, c: (b, 0, c)),
            pl.BlockSpec((1, 1, cbw), lambda b, c: (b, 0, c)),
            pl.BlockSpec((cw.shape[0], cbw), vec),
            pl.BlockSpec((1, cbw), vec),
            pl.BlockSpec((nper, blk, blk), lambda b, c: (c, 0, 0)),
            pl.BlockSpec((nper, blk, blk), lambda b, c: (c, 0, 0)),
            pl.BlockSpec((1, cbw), vec),
            pl.BlockSpec((1, cbw), vec),
            pl.BlockSpec((1, cbw), vec),
        ],
        out_specs=[
            pl.BlockSpec((seq, cbw), lambda b, c: (b, c)),
            pl.BlockSpec((1, 1, cbw), lambda b, c: (b, 0, c)),
        ],
        out_shape=[
            jax.ShapeDtypeStruct((batch * seq, width), BF16),
            jax.ShapeDtypeStruct((batch, 1, width), F32),
        ],
        scratch_shapes=[pltpu.VMEM((1, cbw), F32)],
        compiler_params=_cparams("parallel", "parallel"),
        name="lru_prompt",
    )(proj, proj, halo, h0, cw, cb, wa, wi, ba, bi, lam)


def _lru_sample_body(x_ref, g_ref, st_ref, h0_ref, cw_ref, cb_ref, wa_ref, wi_ref, ba_ref, bi_ref, lam_ref,
                     y_ref, hl_ref, *, seg):
    cw = cw_ref[...]
    width = cw.shape[0]
    hist = [st_ref[j] for j in range(width - 1)]
    h = h0_ref[...]
    for t in range(seg):
        x = x_ref[t]
        taps = hist + [x]
        xc = taps[0] * cw[0:1]
        for j in range(1, width):
            xc = xc + taps[j] * cw[j:j + 1]
        xc = xc + cb_ref[...]
        a, u = _lru_gates(xc, wa_ref, wi_ref, ba_ref[...], bi_ref[...], lam_ref[...])
        h = a * h + u
        y_ref[t] = (h * _gelu_tanh(g_ref[t])).astype(y_ref.dtype)
        hist = hist[1:] + [x]
    hl_ref[...] = h


def lru_sample(x_t, g_t, st, h0, cw, cb, wa, wi, ba, bi, lam, *, cbw=256):
    seg, n_seq, width = x_t.shape
    blk = wa.shape[1]
    nper = cbw // blk
    vec = lambda c: (0, c)
    return pl.pallas_call(
        functools.partial(_lru_sample_body, seg=seg),
        grid=(width // cbw,),
        in_specs=[
            pl.BlockSpec((seg, n_seq, cbw), lambda c: (0, 0, c)),
            pl.BlockSpec((seg, n_seq, cbw), lambda c: (0, 0, c)),
            pl.BlockSpec((st.shape[0], n_seq, cbw), lambda c: (0, 0, c)),
            pl.BlockSpec((n_seq, cbw), vec),
            pl.BlockSpec((cw.shape[0], cbw), vec),
            pl.BlockSpec((1, cbw), vec),
            pl.BlockSpec((nper, blk, blk), lambda c: (c, 0, 0)),
            pl.BlockSpec((nper, blk, blk), lambda c: (c, 0, 0)),
            pl.BlockSpec((1, cbw), vec),
            pl.BlockSpec((1, cbw), vec),
            pl.BlockSpec((1, cbw), vec),
        ],
        out_specs=[
            pl.BlockSpec((seg, n_seq, cbw), lambda c: (0, 0, c)),
            pl.BlockSpec((n_seq, cbw), vec),
        ],
        out_shape=[
            jax.ShapeDtypeStruct((seg, n_seq, width), BF16),
            jax.ShapeDtypeStruct((n_seq, width), F32),
        ],
        compiler_params=_cparams("parallel"),
        name="lru_sample",
    )(x_t, g_t, st, h0, cw, cb, wa, wi, ba, bi, lam)


def _sconv_prompt_body(bg_ref, cg_ref, hx_ref, halo_ref, cw_ref, y_ref, tail_ref, *, seq):
    Lc = SCAN_CHUNK
    cw = cw_ref[...]
    width = cw.shape[0]

    def chunk(ci, carry):
        r0 = pl.multiple_of(ci * Lc, Lc)
        p = cg_ref[pl.ds(r0, Lc), :] * hx_ref[pl.ds(r0, Lc), :]
        rp = pl.multiple_of(jnp.maximum(r0 - V7X_SUBLANES, 0), V7X_SUBLANES)
        prev = cg_ref[pl.ds(rp, V7X_SUBLANES), :] * hx_ref[pl.ds(rp, V7X_SUBLANES), :]
        prev = jnp.where(ci == 0, halo_ref[0], prev)
        pfull = jnp.concatenate([prev, p], axis=0)
        z = None
        for j in range(width - 1, 0, -1):
            term = pltpu.roll(pfull, j, 0)[V7X_SUBLANES:] * cw[width - 1 - j:width - j]
            z = term if z is None else z + term
        z = z + p * cw[width - 1:width]
        y_ref[pl.ds(r0, Lc), :] = (bg_ref[pl.ds(r0, Lc), :] * z).astype(y_ref.dtype)
        return carry

    lax.fori_loop(0, seq // Lc, chunk, 0)
    rt = seq - V7X_SUBLANES
    tail_ref[0] = cg_ref[pl.ds(rt, V7X_SUBLANES), :] * hx_ref[pl.ds(rt, V7X_SUBLANES), :]


def sconv_prompt(proj, halo, cw, *, batch, seq, width, cbw=512):
    nb = width // cbw
    return pl.pallas_call(
        functools.partial(_sconv_prompt_body, seq=seq),
        grid=(batch, nb),
        in_specs=[
            pl.BlockSpec((seq, cbw), lambda b, c: (b, c)),
            pl.BlockSpec((seq, cbw), lambda b, c: (b, nb + c)),
            pl.BlockSpec((seq, cbw), lambda b, c: (b, 2 * nb + c)),
            pl.BlockSpec((1, V7X_SUBLANES, cbw), lambda b, c: (b, 0, c)),
            pl.BlockSpec((cw.shape[0], cbw), lambda b, c: (0, c)),
        ],
        out_specs=[
            pl.BlockSpec((seq, cbw), lambda b, c: (b, c)),
            pl.BlockSpec((1, V7X_SUBLANES, cbw), lambda b, c: (b, 0, c)),
        ],
        out_shape=[
            jax.ShapeDtypeStruct((batch * seq, width), BF16),
            jax.ShapeDtypeStruct((batch, V7X_SUBLANES, width), F32),
        ],
        compiler_params=_cparams("parallel", "parallel"),
        name="sconv_prompt",
    )(proj, proj, proj, halo, cw)


def _sconv_sample_body(bg_ref, cg_ref, hx_ref, st_ref, cw_ref, y_ref, ps_ref, *, seg):
    cw = cw_ref[...]
    width = cw.shape[0]
    hist = [st_ref[j] for j in range(width - 1)]
    for t in range(seg):
        p = cg_ref[t] * hx_ref[t]
        ps_ref[t] = p
        taps = hist + [p]
        z = taps[0] * cw[0:1]
        for j in range(1, width):
            z = z + taps[j] * cw[j:j + 1]
        y_ref[t] = (bg_ref[t] * z).astype(y_ref.dtype)
        hist = hist[1:] + [p]


def sconv_sample(proj_t, st, cw, *, width, cbw=512):
    seg, n_seq, _ = proj_t.shape
    nb = width // cbw
    return pl.pallas_call(
        functools.partial(_sconv_sample_body, seg=seg),
        grid=(nb,),
        in_specs=[
            pl.BlockSpec((seg, n_seq, cbw), lambda c: (0, 0, c)),
            pl.BlockSpec((seg, n_seq, cbw), lambda c: (0, 0, nb + c)),
            pl.BlockSpec((seg, n_seq, cbw), lambda c: (0, 0, 2 * nb + c)),
            pl.BlockSpec((st.shape[0], n_seq, cbw), lambda c: (0, 0, c)),
            pl.BlockSpec((cw.shape[0], cbw), lambda c: (0, c)),
        ],
        out_specs=[
            pl.BlockSpec((seg, n_seq, cbw), lambda c: (0, 0, c)),
            pl.BlockSpec((seg, n_seq, cbw), lambda c: (0, 0, c)),
        ],
        out_shape=[
            jax.ShapeDtypeStruct((seg, n_seq, width), BF16),
            jax.ShapeDtypeStruct((seg, n_seq, width), F32),
        ],
        compiler_params=_cparams("parallel"),
        name="sconv_sample",
    )(proj_t, proj_t, proj_t, st, cw)


def _softmax_rows(s):
    e = jnp.exp(s - jnp.max(s, axis=-1, keepdims=True))
    return e / jnp.sum(e, axis=-1, keepdims=True)


def _xattn_prompt_body(q_ref, k_ref, v_ref, o_ref, *, scale):
    s = lax.dot_general(q_ref[...], k_ref[...].astype(BF16), (((1,), (1,)), ((), ())), preferred_element_type=F32) * scale
    p = _softmax_rows(s).astype(BF16)
    o_ref[...] = jnp.dot(p, v_ref[...].astype(BF16), preferred_element_type=F32).astype(o_ref.dtype)


def xattn_prompt(q, mk, mv, *, batch, seq, n_mem, n_heads, hd, tq=1024):
    nq = seq // tq
    return pl.pallas_call(
        functools.partial(_xattn_prompt_body, scale=hd ** -0.5),
        grid=(batch, n_heads, nq),
        in_specs=[
            pl.BlockSpec((tq, hd), lambda b, h, i: (b * nq + i, h)),
            pl.BlockSpec((n_mem, hd), lambda b, h, i: (b, h)),
            pl.BlockSpec((n_mem, hd), lambda b, h, i: (b, h)),
        ],
        out_specs=pl.BlockSpec((tq, hd), lambda b, h, i: (b * nq + i, h)),
        out_shape=jax.ShapeDtypeStruct((batch * seq, n_heads * hd), BF16),
        compiler_params=_cparams("parallel", "parallel", "parallel"),
        name="xattn_prompt",
    )(q, mk, mv)


def _xattn_sample_body(q_ref, k_ref, v_ref, o_ref, *, n_heads, scale):
    n_mem, hd = k_ref.shape[2], k_ref.shape[4]
    k2 = k_ref[0, 0].reshape(n_mem * n_heads, hd).astype(BF16)
    v2 = v_ref[0, 0].reshape(n_mem * n_heads, hd).astype(BF16)
    s = lax.dot_general(q_ref[0], k2, (((1,), (1,)), ((), ())), preferred_element_type=F32) * scale
    row = lax.broadcasted_iota(jnp.int32, s.shape, 0)
    col = lax.broadcasted_iota(jnp.int32, s.shape, 1)
    s = jnp.where((row % n_heads) == (col % n_heads), s, -jnp.inf)
    p = _softmax_rows(s).astype(BF16)
    o_ref[0] = jnp.dot(p, v2, preferred_element_type=F32).astype(o_ref.dtype)


def xattn_sample(q3, cache_k, cache_v, layer):
    n_seq, rows, hd = q3.shape
    _, _, n_mem, n_heads, _ = cache_k.shape
    kv_spec = pl.BlockSpec((1, 1, n_mem, n_heads, hd), lambda b: (layer, b, 0, 0, 0))
    return pl.pallas_call(
        functools.partial(_xattn_sample_body, n_heads=n_heads, scale=hd ** -0.5),
        grid=(n_seq,),
        in_specs=[pl.BlockSpec((1, rows, hd), lambda b: (b, 0, 0)), kv_spec, kv_spec],
        out_specs=pl.BlockSpec((1, rows, hd), lambda b: (b, 0, 0)),
        out_shape=jax.ShapeDtypeStruct((n_seq, rows, hd), BF16),
        compiler_params=_cparams("parallel"),
        name="xattn_sample",
    )(q3, cache_k, cache_v)


def kernel(x_prompt, x_sample, mem_prompt, cache_mem_k, cache_mem_v, state_mlstm_c, state_mlstm_n, state_mlstm_m, state_lru_h, state_lru_conv, state_sconv, norm_mix, norm_x, norm_ff, norm_final, w_in_e, b_if, mh_norm, conv_b_w, conv_b_b, lru_wa, lru_ba, lru_wi, lru_bi, lru_lam, w_out_e, w_in_o, conv_c_w, w_out_o, w_xq, w_xk, w_xv, w_xo, w_ff1, w_ff2):
    bp, sp, d = x_prompt.shape
    bs, ss, _ = x_sample.shape
    n_mem = mem_prompt.shape[1]
    depth, _, _, xh, xhd = cache_mem_k.shape
    _, _, ah, dk, dv = state_mlstm_c.shape
    bw = state_lru_h.shape[-1]
    bconv = state_lru_conv.shape[2] + 1
    cconv = state_sconv.shape[2] + 1
    cwid = state_sconv.shape[-1]
    aw = ah * dv
    aqk = ah * dk
    tp = bp * sp
    ts = bs * ss
    t = tp + ts
    TM = 1088
    TN = 512
    TM2 = 1088
    TN2 = 512
    assert t % TM == 0 and tp % 512 == 0 and ts == 512

    x = jnp.concatenate([x_prompt.reshape(tp, d), x_sample.reshape(ts, d)], axis=0)
    mem_bf = mem_prompt.reshape(bp * n_mem, d).astype(BF16)

    outs = {}
    ie = io = 0
    p_mem_k, p_mem_v = [], []
    for l in range(depth):
        u = rmsnorm(x, norm_mix[l])
        if l % 2 == 0:
            w = w_in_e[ie]
            o_qkvo = 2 * aqk + 2 * aw
            o_if = o_qkvo + 2 * ah
            w_if = jnp.pad(w[:, o_qkvo:o_if], ((0, 0), (0, V7X_LANES - 2 * ah))).astype(BF16)
            bias_if = jnp.pad(b_if[ie], (0, V7X_LANES - 2 * ah)).reshape(1, V7X_LANES)
            col_q, col_k, col_v, col_o = 0, aqk, 2 * aqk, 2 * aqk + aw
            col_x, col_g = 0, bw
            proj = matmul([u], [(w_in_e, ie, 0)], n_out=o_qkvo, tm=TM, tn=TN, name="in_proj_even_qkvo")
            proj_b = matmul([u], [w[:, o_if:]], tm=TM, tn=TN, name="in_proj_even_lru")
            gc, gr = mlstm_gates(u, w_if, bias_if, n_prompt_rows=tp, seg_sample=ss)
            gr = gr.reshape(16, 1, t)
            mh = mh_norm[ie].reshape(1, aw)
            mkw = dict(n_heads=ah, dk=dk, dv=dv, col_q=col_q, col_k=col_k, col_v=col_v, col_o=col_o)
            ya_p, pc, pn, pm = mlstm_prompt(proj, gc, gr, mh, batch=bp, seq=sp, **mkw)
            n0_tok = jnp.repeat(jnp.transpose(state_mlstm_n[ie], (1, 0, 2)), ss, axis=1)
            m0_tok = jnp.broadcast_to(jnp.repeat(state_mlstm_m[ie].T, ss, axis=1)[:, :, None], (ah, ts, MLSTM_CHUNK))
            ya_s, sc_, sn_tok, sm_tok = mlstm_sample(proj, gc, gr, mh, state_mlstm_c[ie:ie + 1], n0_tok, m0_tok,
                                                     row0=tp, n_seq=bs, seg=ss, **mkw)
            outs.setdefault("pc", []).append(pc.reshape(bp, ah, dk, dv))
            outs.setdefault("pn", []).append(pn.reshape(bp, ah, dk))
            outs.setdefault("pm", []).append(pm.reshape(bp, ah))
            outs.setdefault("sc", []).append(sc_[0])
            outs.setdefault("sn", []).append(jnp.transpose(sn_tok[:, ss - 1::ss, :], (1, 0, 2)))
            outs.setdefault("sm", []).append(sm_tok[:, ss - 1::ss, 0].T)

            cw = conv_b_w[ie]
            cbias = conv_b_b[ie].reshape(1, bw)
            wa = lru_wa[ie].astype(BF16)
            wi = lru_wi[ie].astype(BF16)
            ba = lru_ba[ie].reshape(1, bw)
            bi = lru_bi[ie].reshape(1, bw)
            lam = lru_lam[ie].reshape(1, bw)
            halo_p = jnp.zeros((bp, V7X_SUBLANES, bw), F32)
            h0_p = jnp.zeros((bp, 1, bw), F32)
            yb_p, ph = lru_prompt(proj_b, halo_p, h0_p, cw, cbias, wa, wi, ba, bi, lam,
                                  batch=bp, seq=sp, width=bw, col_x=col_x, col_g=col_g)
            st_s = jnp.transpose(state_lru_conv[ie], (1, 0, 2))
            xb_p = proj_b[:tp, col_x:col_x + bw].reshape(bp, sp, bw)
            xb_s = proj_b[tp:, col_x:col_x + bw].reshape(bs, ss, bw)
            gb_s = proj_b[tp:, col_g:col_g + bw].reshape(bs, ss, bw)
            yb_s, sh = lru_sample(jnp.transpose(xb_s, (1, 0, 2)), jnp.transpose(gb_s, (1, 0, 2)), st_s,
                                  state_lru_h[ie], cw, cbias, wa, wi, ba, bi, lam)
            yb_s = jnp.transpose(yb_s, (1, 0, 2)).reshape(ts, bw)
            outs.setdefault("ph", []).append(ph.reshape(bp, bw))
            outs.setdefault("pcb", []).append(xb_p[:, sp - (bconv - 1):])
            outs.setdefault("sh", []).append(sh)
            outs.setdefault("scb", []).append(jnp.concatenate([state_lru_conv[ie], xb_s], axis=1)[:, ss:])

            ya = jnp.concatenate([ya_p, ya_s], axis=0)
            yb = jnp.concatenate([yb_p, yb_s], axis=0)
            x = matmul([ya, yb], [(w_out_e, ie, 0), (w_out_e, ie, 1)], res=x, tm=TM, tn=TN, name="out_proj_even")
            ie += 1
        else:
            proj = matmul([u], [(w_in_o, io, 0)], tm=TM, tn=TN, name="in_proj_odd")
            cw = conv_c_w[io]
            halo_p = jnp.zeros((bp, V7X_SUBLANES, cwid), F32)
            y_p, tail = sconv_prompt(proj, halo_p, cw, batch=bp, seq=sp, width=cwid)
            st_s = jnp.transpose(state_sconv[io], (1, 0, 2))
            proj_t = jnp.transpose(proj[tp:].reshape(bs, ss, 3 * cwid), (1, 0, 2))
            y_s, ps = sconv_sample(proj_t, st_s, cw, width=cwid)
            y_s = jnp.transpose(y_s, (1, 0, 2)).reshape(ts, cwid)
            outs.setdefault("psb", []).append(tail[:, V7X_SUBLANES - (cconv - 1):])
            ps_b = jnp.transpose(ps, (1, 0, 2))
            outs.setdefault("ssb", []).append(jnp.concatenate([state_sconv[io], ps_b], axis=1)[:, ss:])
            y = jnp.concatenate([y_p, y_s], axis=0)
            x = matmul([y], [(w_out_o, io, 0)], res=x, tm=TM, tn=TN, name="out_proj_odd")
            io += 1

        u = rmsnorm(x, norm_x[l])
        q = matmul([u], [(w_xq, l, 0)], out_dtype=BF16, tm=TM, tn=TN, name="xattn_q")
        mk = matmul([mem_bf], [(w_xk, l, 0)], tm=bp * n_mem, tn=TN, name="mem_k")
        mv = matmul([mem_bf], [(w_xv, l, 0)], tm=bp * n_mem, tn=TN, name="mem_v")
        p_mem_k.append(mk.reshape(bp, n_mem, xh, xhd))
        p_mem_v.append(mv.reshape(bp, n_mem, xh, xhd))
        o_p = xattn_prompt(q, mk, mv, batch=bp, seq=sp, n_mem=n_mem, n_heads=xh, hd=xhd)
        o_s = xattn_sample(q[tp:].reshape(bs, ss * xh, xhd), cache_mem_k, cache_mem_v, l)
        o = jnp.concatenate([o_p, o_s.reshape(ts, d)], axis=0)
        x = matmul([o], [(w_xo, l, 0)], res=x, tm=TM, tn=TN, name="xattn_o")

        u = rmsnorm(x, norm_ff[l])
        hmid = matmul([u], [(w_ff1, l, 0)], act="relu2", out_dtype=BF16, tm=TM, tn=TN, name="ff1")
        x = matmul_ksplit(hmid, w_ff2, l, x, tm=TM2, tn=TN2, tk=4096, name="ff2")

    y_p = rmsnorm(x, norm_final, out_dtype=F32, row0=0, nrows=tp)
    y_s = rmsnorm(x, norm_final, out_dtype=F32, row0=tp, nrows=ts)
    dt = x_prompt.dtype
    st = lambda name: jnp.stack(outs[name]).astype(dt)
    return (y_p.reshape(bp, sp, d), y_s.reshape(bs, ss, d),
            st("pc"), st("pn"), st("pm"), st("ph"), st("pcb"), st("psb"),
            jnp.stack(p_mem_k), jnp.stack(p_mem_v),
            st("sc"), st("sn"), st("sm"), st("sh"), st("scb"), st("ssb"))
```

```python
import functools

import jax
import jax.numpy as jnp
from jax import lax
from jax.experimental import pallas as pl
from jax.experimental.pallas import tpu as pltpu

F32 = jnp.float32
BF16 = jnp.bfloat16
EPS = 1e-6
LRU_C = 8.0

V7X_LANES = 128
V7X_SUBLANES = 8
V7X_VMEM_BUDGET_BYTES = 60 * 1024 * 1024

MLSTM_CHUNK = 128
SCAN_CHUNK = 256


def _cparams(*semantics):
    return pltpu.CompilerParams(dimension_semantics=semantics, vmem_limit_bytes=V7X_VMEM_BUDGET_BYTES)


def _log_sigmoid(x):
    return jnp.minimum(x, 0.0) - jnp.log1p(jnp.exp(-jnp.abs(x)))


def _sigmoid(x):
    return 1.0 / (1.0 + jnp.exp(-x))


def _gelu_tanh(x):
    c = 0.7978845608028654
    return 0.5 * x * (1.0 + jnp.tanh(c * (x + 0.044715 * (x * x * x))))


def _expm1(x):
    u = jnp.exp(x)
    um1 = u - 1.0
    safe = jnp.where(um1 == 0.0, 1.0, jnp.log(u))
    return jnp.where(um1 == 0.0, x, jnp.where(um1 == -1.0, -1.0, um1 * x / safe))


def _split3(x):
    hi = x.astype(BF16)
    r1 = x - hi.astype(F32)
    mid = r1.astype(BF16)
    lo = (r1 - mid.astype(F32)).astype(BF16)
    return hi, mid, lo


def _dot01(mask_bf16, x):
    hi, mid, lo = _split3(x)
    acc = jnp.dot(mask_bf16, lo, preferred_element_type=F32)
    acc = acc + jnp.dot(mask_bf16, mid, preferred_element_type=F32)
    return acc + jnp.dot(mask_bf16, hi, preferred_element_type=F32)


def _rmsnorm_body(x_ref, g_ref, o_ref):
    x = x_ref[...]
    inv = lax.rsqrt(jnp.mean(x * x, axis=-1, keepdims=True) + EPS)
    o_ref[...] = ((x * inv) * g_ref[...]).astype(o_ref.dtype)


def rmsnorm(x, g, out_dtype=BF16, rows=256, row0=0, nrows=None):
    d = x.shape[1]
    nrows = x.shape[0] if nrows is None else nrows
    b0 = row0 // rows
    return pl.pallas_call(
        _rmsnorm_body,
        grid=(nrows // rows,),
        in_specs=[pl.BlockSpec((rows, d), lambda i: (b0 + i, 0)), pl.BlockSpec((1, d), lambda i: (0, 0))],
        out_specs=pl.BlockSpec((rows, d), lambda i: (i, 0)),
        out_shape=jax.ShapeDtypeStruct((nrows, d), out_dtype),
        compiler_params=_cparams("parallel"),
        name="rmsnorm",
    )(x, g.reshape(1, d))


def _mm_body(*refs, n_in, has_res, act):
    o_ref = refs[-1]
    acc = jnp.dot(refs[0][...], refs[n_in][...], preferred_element_type=F32)
    for k in range(1, n_in):
        acc = acc + jnp.dot(refs[k][...], refs[n_in + k][...], preferred_element_type=F32)
    if act == "relu2":
        r = jnp.maximum(acc, 0.0)
        acc = r * r
    if has_res:
        acc = refs[2 * n_in][...] + acc
    o_ref[...] = acc.astype(o_ref.dtype)


def matmul(xs, ws, *, res=None, act=None, out_dtype=F32, n_out=None, tm, tn, name):
    m = xs[0].shape[0]
    n = n_out if n_out is not None else (ws[0][0] if isinstance(ws[0], tuple) else ws[0]).shape[-1]
    n_in = len(xs)
    in_specs = [pl.BlockSpec((tm, x.shape[1]), lambda j, i: (i, 0)) for x in xs]
    args = list(xs)
    for x, w in zip(xs, ws):
        if isinstance(w, tuple):
            w, layer, kb = w
            in_specs.append(pl.BlockSpec((None, x.shape[1], tn), lambda j, i, layer=layer, kb=kb: (layer, kb, j)))
        else:
            in_specs.append(pl.BlockSpec((w.shape[0], tn), lambda j, i: (0, j)))
        args.append(w)
    if res is not None:
        in_specs.append(pl.BlockSpec((tm, tn), lambda j, i: (i, j)))
        args.append(res)
    return pl.pallas_call(
        functools.partial(_mm_body, n_in=n_in, has_res=res is not None, act=act),
        grid=(n // tn, m // tm),
        in_specs=in_specs,
        out_specs=pl.BlockSpec((tm, tn), lambda j, i: (i, j)),
        out_shape=jax.ShapeDtypeStruct((m, n), out_dtype),
        compiler_params=_cparams("parallel", "parallel"),
        name=name,
    )(*args)


def _mm_acc_body(x_ref, w_ref, res_ref, o_ref):
    k = pl.program_id(2)

    @pl.when(k == 0)
    def _():
        o_ref[...] = res_ref[...] + jnp.dot(x_ref[...], w_ref[...], preferred_element_type=F32)

    @pl.when(k != 0)
    def _():
        o_ref[...] += jnp.dot(x_ref[...], w_ref[...], preferred_element_type=F32)


def matmul_ksplit(x, w, layer, res, *, tm, tn, tk, name):
    m, kk = x.shape
    n = w.shape[-1]
    return pl.pallas_call(
        _mm_acc_body,
        grid=(m // tm, n // tn, kk // tk),
        in_specs=[
            pl.BlockSpec((tm, tk), lambda i, j, k: (i, k)),
            pl.BlockSpec((None, tk, tn), lambda i, j, k: (layer, k, j)),
            pl.BlockSpec((tm, tn), lambda i, j, k: (i, j)),
        ],
        out_specs=pl.BlockSpec((tm, tn), lambda i, j, k: (i, j)),
        out_shape=jax.ShapeDtypeStruct((m, n), F32),
        compiler_params=_cparams("parallel", "parallel", "arbitrary"),
        name=name,
    )(x, w, res)


def _gates_body(u_ref, w_ref, b_ref, gc_ref, gr_ref, *, tm, n_prompt_tiles, seg_sample):
    L = MLSTM_CHUNK
    g = jnp.dot(u_ref[...], w_ref[...], preferred_element_type=F32) + b_ref[...]
    lane = lax.broadcasted_iota(jnp.int32, (tm, V7X_LANES), 1)
    val = jnp.where(lane < 8, g, _log_sigmoid(g))
    row = lax.broadcasted_iota(jnp.int32, (L, L), 0)
    col = lax.broadcasted_iota(jnp.int32, (L, L), 1)
    is_sample = pl.program_id(0) >= n_prompt_tiles
    tril = jnp.where(col <= row, 1.0, 0.0)
    tril_seg = jnp.where((col <= row) & ((row // seg_sample) == (col // seg_sample)), 1.0, 0.0)
    mask = jnp.where(is_sample, tril_seg, tril).astype(BF16)
    lane_l = lax.broadcasted_iota(jnp.int32, (L, V7X_LANES), 1)
    for s in range(tm // L):
        x = val[s * L:(s + 1) * L]
        out = jnp.where(lane_l < 8, x, _dot01(mask, x))
        gr_ref[:, s * L:(s + 1) * L] = out.T[:16]
        for c in range(16):
            gc_ref[c, s * L:(s + 1) * L, :] = jnp.broadcast_to(out[:, c:c + 1], (L, V7X_LANES))


def mlstm_gates(u, w_if, b_if, *, n_prompt_rows, seg_sample, tm=512):
    t, d = u.shape
    return pl.pallas_call(
        functools.partial(_gates_body, tm=tm, n_prompt_tiles=n_prompt_rows // tm, seg_sample=seg_sample),
        grid=(t // tm,),
        in_specs=[
            pl.BlockSpec((tm, d), lambda i: (i, 0)),
            pl.BlockSpec((d, V7X_LANES), lambda i: (0, 0)),
            pl.BlockSpec((1, V7X_LANES), lambda i: (0, 0)),
        ],
        out_specs=[
            pl.BlockSpec((16, tm, V7X_LANES), lambda i: (0, i, 0)),
            pl.BlockSpec((16, tm), lambda i: (0, i)),
        ],
        out_shape=[
            jax.ShapeDtypeStruct((16, t, V7X_LANES), F32),
            jax.ShapeDtypeStruct((16, t), F32),
        ],
        compiler_params=_cparams("parallel"),
        name="mlstm_gates",
    )(u, w_if, b_if)


def _head_out(hh, o, mh):
    hn = hh * lax.rsqrt(jnp.mean(hh * hh, axis=-1, keepdims=True) + EPS)
    return _sigmoid(o) * (hn * mh)


def _mlstm_prompt_body(q_ref, k_ref, v_ref, o_ref, ic_ref, bc_ref, ir_ref, br_ref, mh_ref,
                       ya_ref, c_ref, n_ref, m_ref, c_scr, n_scr, m_scr, *, n_heads, seq, scale):
    L = MLSTM_CHUNK
    c_scr[...] = jnp.zeros_like(c_scr)
    n_scr[...] = jnp.zeros_like(n_scr)
    m_scr[...] = jnp.zeros_like(m_scr)
    row = lax.broadcasted_iota(jnp.int32, (L, L), 0)
    col = lax.broadcasted_iota(jnp.int32, (L, L), 1)
    causal = col <= row

    def chunk(ci, carry):
        r0 = pl.multiple_of(ci * L, L)
        q = q_ref[pl.ds(r0, L), :] * scale
        k = k_ref[pl.ds(r0, L), :]
        v = v_ref[pl.ds(r0, L), :]
        bc = bc_ref[0, pl.ds(r0, L), :]
        ic = ic_ref[0, pl.ds(r0, L), :]
        i_row = ir_ref[0, :, pl.ds(r0, L)]
        b_row = br_ref[0, :, pl.ds(r0, L)]
        m_prev = m_scr[...]
        c_prev = c_scr[...]
        n_prev = n_scr[...]
        dmat = jnp.where(causal, bc - b_row + i_row, -jnp.inf)
        g = bc + m_prev
        mt = jnp.maximum(g, jnp.max(dmat, axis=-1, keepdims=True))
        w_inter = jnp.exp(g - mt)
        qb = q.astype(BF16)
        kb = k.astype(BF16)
        vb = v.astype(BF16)
        sc = lax.dot_general(qb, kb, (((1,), (1,)), ((), ())), preferred_element_type=F32) * jnp.exp(dmat - mt)
        inter = jnp.dot(qb, c_prev.astype(BF16), preferred_element_type=F32)
        num = jnp.concatenate([w_inter, w_inter], axis=1) * inter + jnp.dot(sc.astype(BF16), vb, preferred_element_type=F32)
        qn = jnp.sum(q * n_prev, axis=-1, keepdims=True)
        den = w_inter[:, :1] * qn + jnp.sum(sc, axis=-1, keepdims=True)
        hh = num / jnp.maximum(jnp.abs(den), jnp.exp(-mt[:, :1]))
        ya_ref[pl.ds(r0, L), :] = _head_out(hh, o_ref[pl.ds(r0, L), :], mh_ref[...]).astype(ya_ref.dtype)
        m_new = mt[L - 1:L, :1]
        b_last = b_row[:, L - 1:L]
        decay = jnp.exp(b_last + m_prev - m_new)
        kw = k * jnp.exp(b_last - bc + ic - m_new)
        c_scr[...] = decay * c_prev + lax.dot_general(kw.astype(BF16), vb, (((0,), (0,)), ((), ())), preferred_element_type=F32)
        n_scr[...] = decay * n_prev + jnp.sum(kw, axis=0, keepdims=True)
        m_scr[...] = m_new
        return carry

    lax.fori_loop(0, seq // L, chunk, 0)
    c_ref[0, 0] = c_scr[...]
    n_ref[0, 0] = n_scr[...]
    m_ref[0, 0] = m_scr[...]


def mlstm_prompt(proj, gc, gr, mh_norm, *, batch, seq, n_heads, dk, dv, col_q, col_k, col_v, col_o):
    L = MLSTM_CHUNK
    assert dk == L and dv == 2 * L
    rows = batch * seq
    return pl.pallas_call(
        functools.partial(_mlstm_prompt_body, n_heads=n_heads, seq=seq, scale=dk ** -0.5),
        grid=(batch, n_heads),
        in_specs=[
            pl.BlockSpec((seq, dk), lambda b, h: (b, col_q // dk + h)),
            pl.BlockSpec((seq, dk), lambda b, h: (b, col_k // dk + h)),
            pl.BlockSpec((seq, dv), lambda b, h: (b, col_v // dv + h)),
            pl.BlockSpec((seq, dv), lambda b, h: (b, col_o // dv + h)),
            pl.BlockSpec((1, seq, V7X_LANES), lambda b, h: (h, b, 0)),
            pl.BlockSpec((1, seq, V7X_LANES), lambda b, h: (h + n_heads, b, 0)),
            pl.BlockSpec((1, 1, seq), lambda b, h: (h, 0, b)),
            pl.BlockSpec((1, 1, seq), lambda b, h: (h + n_heads, 0, b)),
            pl.BlockSpec((1, dv), lambda b, h: (0, h)),
        ],
        out_specs=[
            pl.BlockSpec((seq, dv), lambda b, h: (b, h)),
            pl.BlockSpec((1, 1, dk, dv), lambda b, h: (b, h, 0, 0)),
            pl.BlockSpec((1, 1, 1, dk), lambda b, h: (b, h, 0, 0)),
            pl.BlockSpec((1, 1, 1, 1), lambda b, h: (b, h, 0, 0)),
        ],
        out_shape=[
            jax.ShapeDtypeStruct((rows, n_heads * dv), BF16),
            jax.ShapeDtypeStruct((batch, n_heads, dk, dv), F32),
            jax.ShapeDtypeStruct((batch, n_heads, 1, dk), F32),
            jax.ShapeDtypeStruct((batch, n_heads, 1, 1), F32),
        ],
        scratch_shapes=[pltpu.VMEM((dk, dv), F32), pltpu.VMEM((1, dk), F32), pltpu.VMEM((1, 1), F32)],
        compiler_params=_cparams("parallel", "parallel"),
        name="mlstm_prompt",
    )(proj, proj, proj, proj, gc, gc, gr, gr, mh_norm)


def _mlstm_sample_body(q_ref, k_ref, v_ref, o_ref, ic_ref, bc_ref, ir_ref, br_ref, mh_ref, c0_ref, n0_ref, m0_ref,
                       ya_ref, c_ref, n_ref, m_ref, inter_scr, *, n_heads, seg, scale):
    L = MLSTM_CHUNK
    nb = L // seg
    row = lax.broadcasted_iota(jnp.int32, (L, L), 0)
    col = lax.broadcasted_iota(jnp.int32, (L, L), 1)
    same = (row // seg) == (col // seg)
    causal = (col <= row) & same
    last = col == (row // seg) * seg + (seg - 1)
    q = q_ref[...] * scale
    k = k_ref[...]
    v = v_ref[...]
    bc = bc_ref[0]
    ic = ic_ref[0]
    i_row = ir_ref[0]
    b_row = br_ref[0]
    m_prev = m0_ref[0]
    n_prev = n0_ref[0]
    dmat = jnp.where(causal, bc - b_row + i_row, -jnp.inf)
    g = bc + m_prev
    mt = jnp.maximum(g, jnp.max(dmat, axis=-1, keepdims=True))
    w_inter = jnp.exp(g - mt)
    qb = q.astype(BF16)
    kb = k.astype(BF16)
    vb = v.astype(BF16)
    sc = lax.dot_general(qb, kb, (((1,), (1,)), ((), ())), preferred_element_type=F32) * jnp.exp(dmat - mt)

    per_group = V7X_SUBLANES // seg
    sub = lax.broadcasted_iota(jnp.int32, (V7X_SUBLANES, 2 * L), 0)
    for gi in range(L // V7X_SUBLANES):
        q8 = qb[gi * V7X_SUBLANES:(gi + 1) * V7X_SUBLANES]
        acc = jnp.zeros((V7X_SUBLANES, 2 * L), F32)
        for j in range(per_group):
            r = jnp.dot(q8, c0_ref[0, gi * per_group + j, 0].astype(BF16), preferred_element_type=F32)
            acc = jnp.where((sub // seg) == j, r, acc)
        inter_scr[gi * V7X_SUBLANES:(gi + 1) * V7X_SUBLANES, :] = acc
    inter = inter_scr[...]

    num = jnp.concatenate([w_inter, w_inter], axis=1) * inter + jnp.dot(sc.astype(BF16), vb, preferred_element_type=F32)
    qn = jnp.sum(q * n_prev, axis=-1, keepdims=True)
    den = w_inter[:, :1] * qn + jnp.sum(sc, axis=-1, keepdims=True)
    hh = num / jnp.maximum(jnp.abs(den), jnp.exp(-mt[:, :1]))
    ya_ref[...] = _head_out(hh, o_ref[...], mh_ref[...]).astype(ya_ref.dtype)

    b_last = jnp.sum(jnp.where(last, jnp.broadcast_to(b_row, (L, L)), 0.0), axis=-1, keepdims=True)
    mt_row = mt.T
    m_new = jnp.sum(jnp.where(last, mt_row, 0.0), axis=-1, keepdims=True)
    decay = jnp.exp(b_last + m_prev - m_new)
    kw = k * jnp.exp(b_last - bc + ic - m_new)
    seg_ones = jnp.where(same, 1.0, 0.0).astype(BF16)
    n_ref[0] = decay * n_prev + _dot01(seg_ones, kw)
    m_ref[0] = jnp.broadcast_to(m_new, (L, L))
    rowk = lax.broadcasted_iota(jnp.int32, (L, L), 0)
    for bi in range(nb):
        kw_b = jnp.where((rowk // seg) == bi, kw, 0.0).astype(BF16)
        upd = lax.dot_general(kw_b, vb, (((0,), (0,)), ((), ())), preferred_element_type=F32)
        d_b = decay[bi * seg:bi * seg + 1, :1]
        c_ref[0, bi, 0] = d_b * c0_ref[0, bi, 0] + upd


def mlstm_sample(proj, gc, gr, mh_norm, c0, n0_tok, m0_tok, *, row0, n_seq, seg, n_heads, dk, dv,
                 col_q, col_k, col_v, col_o):
    L = MLSTM_CHUNK
    assert dk == L and dv == 2 * L and V7X_SUBLANES % seg == 0 and row0 % L == 0
    rows = n_seq * seg
    nb = L // seg
    t0 = row0 // L
    return pl.pallas_call(
        functools.partial(_mlstm_sample_body, n_heads=n_heads, seg=seg, scale=dk ** -0.5),
        grid=(rows // L, n_heads),
        in_specs=[
            pl.BlockSpec((L, dk), lambda j, h: (t0 + j, col_q // dk + h)),
            pl.BlockSpec((L, dk), lambda j, h: (t0 + j, col_k // dk + h)),
            pl.BlockSpec((L, dv), lambda j, h: (t0 + j, col_v // dv + h)),
            pl.BlockSpec((L, dv), lambda j, h: (t0 + j, col_o // dv + h)),
            pl.BlockSpec((1, L, V7X_LANES), lambda j, h: (h, t0 + j, 0)),
            pl.BlockSpec((1, L, V7X_LANES), lambda j, h: (h + n_heads, t0 + j, 0)),
            pl.BlockSpec((1, 1, L), lambda j, h: (h, 0, t0 + j)),
            pl.BlockSpec((1, 1, L), lambda j, h: (h + n_heads, 0, t0 + j)),
            pl.BlockSpec((1, dv), lambda j, h: (0, h)),
            pl.BlockSpec((1, nb, 1, dk, dv), lambda j, h: (0, j, h, 0, 0)),
            pl.BlockSpec((1, L, dk), lambda j, h: (h, j, 0)),
            pl.BlockSpec((1, L, L), lambda j, h: (h, j, 0)),
        ],
        out_specs=[
            pl.BlockSpec((L, dv), lambda j, h: (j, h)),
            pl.BlockSpec((1, nb, 1, dk, dv), lambda j, h: (0, j, h, 0, 0)),
            pl.BlockSpec((1, L, dk), lambda j, h: (h, j, 0)),
            pl.BlockSpec((1, L, L), lambda j, h: (h, j, 0)),
        ],
        out_shape=[
            jax.ShapeDtypeStruct((rows, n_heads * dv), BF16),
            jax.ShapeDtypeStruct((1, n_seq, n_heads, dk, dv), F32),
            jax.ShapeDtypeStruct((n_heads, rows, dk), F32),
            jax.ShapeDtypeStruct((n_heads, rows, L), F32),
        ],
        scratch_shapes=[pltpu.VMEM((L, dv), F32)],
        compiler_params=_cparams("parallel", "parallel"),
        name="mlstm_sample",
    )(proj, proj, proj, proj, gc, gc, gr, gr, mh_norm, c0, n0_tok, m0_tok)


def _lru_gates(xc, wa_ref, wi_ref, ba, bi, lam):
    nblk = wa_ref.shape[0]
    blk = wa_ref.shape[1]
    rs, igs = [], []
    for n in range(nblk):
        xb = xc[:, n * blk:(n + 1) * blk].astype(BF16)
        rs.append(jnp.dot(xb, wa_ref[n], preferred_element_type=F32))
        igs.append(jnp.dot(xb, wi_ref[n], preferred_element_type=F32))
    r = _sigmoid(jnp.concatenate(rs, axis=1) + ba)
    ig = _sigmoid(jnp.concatenate(igs, axis=1) + bi)
    log_a = LRU_C * r * _log_sigmoid(lam)
    a = jnp.exp(log_a)
    u = jnp.sqrt(-_expm1(2.0 * log_a)) * (ig * xc)
    return a, u


def _lru_prompt_body(x_ref, g_ref, halo_ref, h0_ref, cw_ref, cb_ref, wa_ref, wi_ref, ba_ref, bi_ref, lam_ref,
                     y_ref, hl_ref, h_scr, *, seq):
    Lc = SCAN_CHUNK
    cbw = x_ref.shape[1]
    h_scr[...] = h0_ref[0]
    row = lax.broadcasted_iota(jnp.int32, (Lc, cbw), 0)
    cw = cw_ref[...]
    width = cw.shape[0]

    def chunk(ci, carry):
        r0 = pl.multiple_of(ci * Lc, Lc)
        x = x_ref[pl.ds(r0, Lc), :]
        prev = x_ref[pl.ds(pl.multiple_of(jnp.maximum(r0 - V7X_SUBLANES, 0), V7X_SUBLANES), V7X_SUBLANES), :]
        prev = jnp.where(ci == 0, halo_ref[0], prev)
        xfull = jnp.concatenate([prev, x], axis=0)
        xc = None
        for j in range(width - 1, 0, -1):
            term = pltpu.roll(xfull, j, 0)[V7X_SUBLANES:] * cw[width - 1 - j:width - j]
            xc = term if xc is None else xc + term
        xc = xc + x * cw[width - 1:width] + cb_ref[...]
        a, u = _lru_gates(xc, wa_ref, wi_ref, ba_ref[...], bi_ref[...], lam_ref[...])
        d = 1
        while d < Lc:
            ok = row >= d
            a_sh = jnp.where(ok, pltpu.roll(a, d, 0), 1.0)
            u_sh = jnp.where(ok, pltpu.roll(u, d, 0), 0.0)
            u = a * u_sh + u
            a = a * a_sh
            d *= 2
        hs = u + a * h_scr[...]
        h_scr[...] = hs[Lc - 1:Lc]
        y_ref[pl.ds(r0, Lc), :] = (hs * _gelu_tanh(g_ref[pl.ds(r0, Lc), :])).astype(y_ref.dtype)
        return carry

    lax.fori_loop(0, seq // Lc, chunk, 0)
    hl_ref[0] = h_scr[...]


def lru_prompt(proj, halo, h0, cw, cb, wa, wi, ba, bi, lam, *, batch, seq, width, col_x, col_g, cbw=256):
    blk = wa.shape[1]
    nper = cbw // blk
    vec = lambda b, c: (0, c)
    return pl.pallas_call(
        functools.partial(_lru_prompt_body, seq=seq),
        grid=(batch, width // cbw),
        in_specs=[
            pl.BlockSpec((seq, cbw), lambda b, c: (b, col_x // cbw + c)),
            pl.BlockSpec((seq, cbw), lambda b, c: (b, col_g // cbw + c)),
            pl.BlockSpec((1, V7X_SUBLANES, cbw), lambda b, c: (b, 0, c)),
            pl.BlockSpec((1, 1, cbw), lambda b, c: (b, 0, c)),
            pl.BlockSpec((cw.shape[0], cbw), vec),
            pl.BlockSpec((1, cbw), vec),
            pl.BlockSpec((nper, blk, blk), lambda b, c: (c, 0, 0)),
            pl.BlockSpec((nper, blk, blk), lambda b, c: (c, 0, 0)),
            pl.BlockSpec((1, cbw), vec),
            pl.BlockSpec((1, cbw), vec),
            pl.BlockSpec((1, cbw), vec),
        ],
        out_specs=[
            pl.BlockSpec((seq, cbw), lambda b, c: (b, c)),
            pl.BlockSpec((1, 1, cbw), lambda b, c: (b, 0, c)),
        ],
        out_shape=[
            jax.ShapeDtypeStruct((batch * seq, width), BF16),
            jax.ShapeDtypeStruct((batch, 1, width), F32),
        ],
        scratch_shapes=[pltpu.VMEM((1, cbw), F32)],
        compiler_params=_cparams("parallel", "parallel"),
        name="lru_prompt",
    )(proj, proj, halo, h0, cw, cb, wa, wi, ba, bi, lam)


def _lru_sample_body(x_ref, g_ref, st_ref, h0_ref, cw_ref, cb_ref, wa_ref, wi_ref, ba_ref, bi_ref, lam_ref,
                     y_ref, hl_ref, *, seg):
    cw = cw_ref[...]
    width = cw.shape[0]
    hist = [st_ref[j] for j in range(width - 1)]
    h = h0_ref[...]
    for t in range(seg):
        x = x_ref[t]
        taps = hist + [x]
        xc = taps[0] * cw[0:1]
        for j in range(1, width):
            xc = xc + taps[j] * cw[j:j + 1]
        xc = xc + cb_ref[...]
        a, u = _lru_gates(xc, wa_ref, wi_ref, ba_ref[...], bi_ref[...], lam_ref[...])
        h = a * h + u
        y_ref[t] = (h * _gelu_tanh(g_ref[t])).astype(y_ref.dtype)
        hist = hist[1:] + [x]
    hl_ref[...] = h


def lru_sample(x_t, g_t, st, h0, cw, cb, wa, wi, ba, bi, lam, *, cbw=256):
    seg, n_seq, width = x_t.shape
    blk = wa.shape[1]
    nper = cbw // blk
    vec = lambda c: (0, c)
    return pl.pallas_call(
        functools.partial(_lru_sample_body, seg=seg),
        grid=(width // cbw,),
        in_specs=[
            pl.BlockSpec((seg, n_seq, cbw), lambda c: (0, 0, c)),
            pl.BlockSpec((seg, n_seq, cbw), lambda c: (0, 0, c)),
            pl.BlockSpec((st.shape[0], n_seq, cbw), lambda c: (0, 0, c)),
            pl.BlockSpec((n_seq, cbw), vec),
            pl.BlockSpec((cw.shape[0], cbw), vec),
            pl.BlockSpec((1, cbw), vec),
            pl.BlockSpec((nper, blk, blk), lambda c: (c, 0, 0)),
            pl.BlockSpec((nper, blk, blk), lambda c: (c, 0, 0)),
            pl.BlockSpec((1, cbw), vec),
            pl.BlockSpec((1, cbw), vec),
            pl.BlockSpec((1, cbw), vec),
        ],
        out_specs=[
            pl.BlockSpec((seg, n_seq, cbw), lambda c: (0, 0, c)),
            pl.BlockSpec((n_seq, cbw), vec),
        ],
        out_shape=[
            jax.ShapeDtypeStruct((seg, n_seq, width), BF16),
            jax.ShapeDtypeStruct((n_seq, width), F32),
        ],
        compiler_params=_cparams("parallel"),
        name="lru_sample",
    )(x_t, g_t, st, h0, cw, cb, wa, wi, ba, bi, lam)


def _sconv_prompt_body(bg_ref, cg_ref, hx_ref, halo_ref, cw_ref, y_ref, tail_ref, *, seq):
    Lc = SCAN_CHUNK
    cw = cw_ref[...]
    width = cw.shape[0]

    def chunk(ci, carry):
        r0 = pl.multiple_of(ci * Lc, Lc)
        p = cg_ref[pl.ds(r0, Lc), :] * hx_ref[pl.ds(r0, Lc), :]
        rp = pl.multiple_of(jnp.maximum(r0 - V7X_SUBLANES, 0), V7X_SUBLANES)
        prev = cg_ref[pl.ds(rp, V7X_SUBLANES), :] * hx_ref[pl.ds(rp, V7X_SUBLANES), :]
        prev = jnp.where(ci == 0, halo_ref[0], prev)
        pfull = jnp.concatenate([prev, p], axis=0)
        z = None
        for j in range(width - 1, 0, -1):
            term = pltpu.roll(pfull, j, 0)[V7X_SUBLANES:] * cw[width - 1 - j:width - j]
            z = term if z is None else z + term
        z = z + p * cw[width - 1:width]
        y_ref[pl.ds(r0, Lc), :] = (bg_ref[pl.ds(r0, Lc), :] * z).astype(y_ref.dtype)
        return carry

    lax.fori_loop(0, seq // Lc, chunk, 0)
    rt = seq - V7X_SUBLANES
    tail_ref[0] = cg_ref[pl.ds(rt, V7X_SUBLANES), :] * hx_ref[pl.ds(rt, V7X_SUBLANES), :]


def sconv_prompt(proj, halo, cw, *, batch, seq, width, cbw=512):
    nb = width // cbw
    return pl.pallas_call(
        functools.partial(_sconv_prompt_body, seq=seq),
        grid=(batch, nb),
        in_specs=[
            pl.BlockSpec((seq, cbw), lambda b, c: (b, c)),
            pl.BlockSpec((seq, cbw), lambda b, c: (b, nb + c)),
            pl.BlockSpec((seq, cbw), lambda b, c: (b, 2 * nb + c)),
            pl.BlockSpec((1, V7X_SUBLANES, cbw), lambda b, c: (b, 0, c)),
            pl.BlockSpec((cw.shape[0], cbw), lambda b, c: (0, c)),
        ],
        out_specs=[
            pl.BlockSpec((seq, cbw), lambda b, c: (b, c)),
            pl.BlockSpec((1, V7X_SUBLANES, cbw), lambda b, c: (b, 0, c)),
        ],
        out_shape=[
            jax.ShapeDtypeStruct((batch * seq, width), BF16),
            jax.ShapeDtypeStruct((batch, V7X_SUBLANES, width), F32),
        ],
        compiler_params=_cparams("parallel", "parallel"),
        name="sconv_prompt",
    )(proj, proj, proj, halo, cw)


def _sconv_sample_body(bg_ref, cg_ref, hx_ref, st_ref, cw_ref, y_ref, ps_ref, *, seg):
    cw = cw_ref[...]
    width = cw.shape[0]
    hist = [st_ref[j] for j in range(width - 1)]
    for t in range(seg):
        p = cg_ref[t] * hx_ref[t]
        ps_ref[t] = p
        taps = hist + [p]
        z = taps[0] * cw[0:1]
        for j in range(1, width):
            z = z + taps[j] * cw[j:j + 1]
        y_ref[t] = (bg_ref[t] * z).astype(y_ref.dtype)
        hist = hist[1:] + [p]


def sconv_sample(proj_t, st, cw, *, width, cbw=512):
    seg, n_seq, _ = proj_t.shape
    nb = width // cbw
    return pl.pallas_call(
        functools.partial(_sconv_sample_body, seg=seg),
        grid=(nb,),
        in_specs=[
            pl.BlockSpec((seg, n_seq, cbw), lambda c: (0, 0, c)),
            pl.BlockSpec((seg, n_seq, cbw), lambda c: (0, 0, nb + c)),
            pl.BlockSpec((seg, n_seq, cbw), lambda c: (0, 0, 2 * nb + c)),
            pl.BlockSpec((st.shape[0], n_seq, cbw), lambda c: (0, 0, c)),
            pl.BlockSpec((cw.shape[0], cbw), lambda c: (0, c)),
        ],
        out_specs=[
            pl.BlockSpec((seg, n_seq, cbw), lambda c: (0, 0, c)),
            pl.BlockSpec((seg, n_seq, cbw), lambda c: (0, 0, c)),
        ],
        out_shape=[
            jax.ShapeDtypeStruct((seg, n_seq, width), BF16),
            jax.ShapeDtypeStruct((seg, n_seq, width), F32),
        ],
        compiler_params=_cparams("parallel"),
        name="sconv_sample",
    )(proj_t, proj_t, proj_t, st, cw)


def _softmax_rows(s):
    e = jnp.exp(s - jnp.max(s, axis=-1, keepdims=True))
    return e / jnp.sum(e, axis=-1, keepdims=True)


def _xattn_prompt_body(q_ref, k_ref, v_ref, o_ref, *, scale):
    s = lax.dot_general(q_ref[...], k_ref[...].astype(BF16), (((1,), (1,)), ((), ())), preferred_element_type=F32) * scale
    p = _softmax_rows(s).astype(BF16)
    o_ref[...] = jnp.dot(p, v_ref[...].astype(BF16), preferred_element_type=F32).astype(o_ref.dtype)


def xattn_prompt(q, mk, mv, *, batch, seq, n_mem, n_heads, hd, tq=1024):
    nq = seq // tq
    return pl.pallas_call(
        functools.partial(_xattn_prompt_body, scale=hd ** -0.5),
        grid=(batch, n_heads, nq),
        in_specs=[
            pl.BlockSpec((tq, hd), lambda b, h, i: (b * nq + i, h)),
            pl.BlockSpec((n_mem, hd), lambda b, h, i: (b, h)),
            pl.BlockSpec((n_mem, hd), lambda b, h, i: (b, h)),
        ],
        out_specs=pl.BlockSpec((tq, hd), lambda b, h, i: (b * nq + i, h)),
        out_shape=jax.ShapeDtypeStruct((batch * seq, n_heads * hd), BF16),
        compiler_params=_cparams("parallel", "parallel", "parallel"),
        name="xattn_prompt",
    )(q, mk, mv)


def _xattn_sample_body(q_ref, k_ref, v_ref, o_ref, *, n_heads, scale):
    n_mem, hd = k_ref.shape[2], k_ref.shape[4]
    k2 = k_ref[0, 0].reshape(n_mem * n_heads, hd).astype(BF16)
    v2 = v_ref[0, 0].reshape(n_mem * n_heads, hd).astype(BF16)
    s = lax.dot_general(q_ref[0], k2, (((1,), (1,)), ((), ())), preferred_element_type=F32) * scale
    row = lax.broadcasted_iota(jnp.int32, s.shape, 0)
    col = lax.broadcasted_iota(jnp.int32, s.shape, 1)
    s = jnp.where((row % n_heads) == (col % n_heads), s, -jnp.inf)
    p = _softmax_rows(s).astype(BF16)
    o_ref[0] = jnp.dot(p, v2, preferred_element_type=F32).astype(o_ref.dtype)


def xattn_sample(q3, cache_k, cache_v, layer):
    n_seq, rows, hd = q3.shape
    _, _, n_mem, n_heads, _ = cache_k.shape
    kv_spec = pl.BlockSpec((1, 1, n_mem, n_heads, hd), lambda b: (layer, b, 0, 0, 0))
    return pl.pallas_call(
        functools.partial(_xattn_sample_body, n_heads=n_heads, scale=hd ** -0.5),
        grid=(n_seq,),
        in_specs=[pl.BlockSpec((1, rows, hd), lambda b: (b, 0, 0)), kv_spec, kv_spec],
        out_specs=pl.BlockSpec((1, rows, hd), lambda b: (b, 0, 0)),
        out_shape=jax.ShapeDtypeStruct((n_seq, rows, hd), BF16),
        compiler_params=_cparams("parallel"),
        name="xattn_sample",
    )(q3, cache_k, cache_v)


def kernel(x_prompt, x_sample, mem_prompt, cache_mem_k, cache_mem_v, state_mlstm_c, state_mlstm_n, state_mlstm_m, state_lru_h, state_lru_conv, state_sconv, norm_mix, norm_x, norm_ff, norm_final, w_in_e, b_if, mh_norm, conv_b_w, conv_b_b, lru_wa, lru_ba, lru_wi, lru_bi, lru_lam, w_out_e, w_in_o, conv_c_w, w_out_o, w_xq, w_xk, w_xv, w_xo, w_ff1, w_ff2):
    bp, sp, d = x_prompt.shape
    bs, ss, _ = x_sample.shape
    n_mem = mem_prompt.shape[1]
    depth, _, _, xh, xhd = cache_mem_k.shape
    _, _, ah, dk, dv = state_mlstm_c.shape
    bw = state_lru_h.shape[-1]
    bconv = state_lru_conv.shape[2] + 1
    cconv = state_sconv.shape[2] + 1
    cwid = state_sconv.shape[-1]
    aw = ah * dv
    aqk = ah * dk
    tp = bp * sp
    ts = bs * ss
    t = tp + ts
    TM = 1088
    TN = 1024
    TK2 = 4096
    assert t % TM == 0 and tp % 512 == 0 and ts == 512
    w_in_e, w_out_e, w_in_o, w_out_o = (w.astype(BF16) for w in (w_in_e, w_out_e, w_in_o, w_out_o))
    w_xq, w_xk, w_xv, w_xo = (w.astype(BF16) for w in (w_xq, w_xk, w_xv, w_xo))
    w_ff1, w_ff2 = w_ff1.astype(BF16), w_ff2.astype(BF16)

    x = jnp.concatenate([x_prompt.reshape(tp, d), x_sample.reshape(ts, d)], axis=0)
    mem_bf = mem_prompt.reshape(bp * n_mem, d).astype(BF16)

    outs = {}
    ie = io = 0
    p_mem_k, p_mem_v = [], []
    for l in range(depth):
        u = rmsnorm(x, norm_mix[l])
        if l % 2 == 0:
            w = w_in_e[ie]
            o_qkvo = 2 * aqk + 2 * aw
            o_if = o_qkvo + 2 * ah
            w_if = jnp.pad(w[:, o_qkvo:o_if], ((0, 0), (0, V7X_LANES - 2 * ah))).astype(BF16)
            bias_if = jnp.pad(b_if[ie], (0, V7X_LANES - 2 * ah)).reshape(1, V7X_LANES)
            col_q, col_k, col_v, col_o = 0, aqk, 2 * aqk, 2 * aqk + aw
            col_x, col_g = 0, bw
            proj = matmul([u], [(w_in_e, ie, 0)], n_out=o_qkvo, tm=TM, tn=TN, name="in_proj_even_qkvo")
            proj_b = matmul([u], [w[:, o_if:]], tm=TM, tn=TN, name="in_proj_even_lru")
            gc, gr = mlstm_gates(u, w_if, bias_if, n_prompt_rows=tp, seg_sample=ss)
            gr = gr.reshape(16, 1, t)
            mh = mh_norm[ie].reshape(1, aw)
            mkw = dict(n_heads=ah, dk=dk, dv=dv, col_q=col_q, col_k=col_k, col_v=col_v, col_o=col_o)
            ya_p, pc, pn, pm = mlstm_prompt(proj, gc, gr, mh, batch=bp, seq=sp, **mkw)
            n0_tok = jnp.repeat(jnp.transpose(state_mlstm_n[ie], (1, 0, 2)), ss, axis=1)
            m0_tok = jnp.broadcast_to(jnp.repeat(state_mlstm_m[ie].T, ss, axis=1)[:, :, None], (ah, ts, MLSTM_CHUNK))
            ya_s, sc_, sn_tok, sm_tok = mlstm_sample(proj, gc, gr, mh, state_mlstm_c[ie:ie + 1], n0_tok, m0_tok,
                                                     row0=tp, n_seq=bs, seg=ss, **mkw)
            outs.setdefault("pc", []).append(pc.reshape(bp, ah, dk, dv))
            outs.setdefault("pn", []).append(pn.reshape(bp, ah, dk))
            outs.setdefault("pm", []).append(pm.reshape(bp, ah))
            outs.setdefault("sc", []).append(sc_[0])
            outs.setdefault("sn", []).append(jnp.transpose(sn_tok[:, ss - 1::ss, :], (1, 0, 2)))
            outs.setdefault("sm", []).append(sm_tok[:, ss - 1::ss, 0].T)

            cw = conv_b_w[ie]
            cbias = conv_b_b[ie].reshape(1, bw)
            wa = lru_wa[ie].astype(BF16)
            wi = lru_wi[ie].astype(BF16)
            ba = lru_ba[ie].reshape(1, bw)
            bi = lru_bi[ie].reshape(1, bw)
            lam = lru_lam[ie].reshape(1, bw)
            halo_p = jnp.zeros((bp, V7X_SUBLANES, bw), F32)
            h0_p = jnp.zeros((bp, 1, bw), F32)
            yb_p, ph = lru_prompt(proj_b, halo_p, h0_p, cw, cbias, wa, wi, ba, bi, lam,
                                  batch=bp, seq=sp, width=bw, col_x=col_x, col_g=col_g)
            st_s = jnp.transpose(state_lru_conv[ie], (1, 0, 2))
            xb_p = proj_b[:tp, col_x:col_x + bw].reshape(bp, sp, bw)
            xb_s = proj_b[tp:, col_x:col_x + bw].reshape(bs, ss, bw)
            gb_s = proj_b[tp:, col_g:col_g + bw].reshape(bs, ss, bw)
            yb_s, sh = lru_sample(jnp.transpose(xb_s, (1, 0, 2)), jnp.transpose(gb_s, (1, 0, 2)), st_s,
                                  state_lru_h[ie], cw, cbias, wa, wi, ba, bi, lam)
            yb_s = jnp.transpose(yb_s, (1, 0, 2)).reshape(ts, bw)
            outs.setdefault("ph", []).append(ph.reshape(bp, bw))
            outs.setdefault("pcb", []).append(xb_p[:, sp - (bconv - 1):])
            outs.setdefault("sh", []).append(sh)
            outs.setdefault("scb", []).append(jnp.concatenate([state_lru_conv[ie], xb_s], axis=1)[:, ss:])

            ya = jnp.concatenate([ya_p, ya_s], axis=0)
            yb = jnp.concatenate([yb_p, yb_s], axis=0)
            x = matmul([ya, yb], [(w_out_e, ie, 0), (w_out_e, ie, 1)], res=x, tm=TM, tn=TN, name="out_proj_even")
            ie += 1
        else:
            proj = matmul([u], [(w_in_o, io, 0)], tm=TM, tn=TN, name="in_proj_odd")
            cw = conv_c_w[io]
            halo_p = jnp.zeros((bp, V7X_SUBLANES, cwid), F32)
            y_p, tail = sconv_prompt(proj, halo_p, cw, batch=bp, seq=sp, width=cwid)
            st_s = jnp.transpose(state_sconv[io], (1, 0, 2))
            proj_t = jnp.transpose(proj[tp:].reshape(bs, ss, 3 * cwid), (1, 0, 2))
            y_s, ps = sconv_sample(proj_t, st_s, cw, width=cwid)
            y_s = jnp.transpose(y_s, (1, 0, 2)).reshape(ts, cwid)
            outs.setdefault("psb", []).append(tail[:, V7X_SUBLANES - (cconv - 1):])
            ps_b = jnp.transpose(ps, (1, 0, 2))
            outs.setdefault("ssb", []).append(jnp.concatenate([state_sconv[io], ps_b], axis=1)[:, ss:])
            y = jnp.concatenate([y_p, y_s], axis=0)
            x = matmul([y], [(w_out_o, io, 0)], res=x, tm=TM, tn=TN, name="out_proj_odd")
            io += 1

        u = rmsnorm(x, norm_x[l])
        q = matmul([u], [(w_xq, l, 0)], out_dtype=BF16, tm=TM, tn=TN, name="xattn_q")
        mk = matmul([mem_bf], [(w_xk, l, 0)], tm=bp * n_mem, tn=TN, name="mem_k")
        mv = matmul([mem_bf], [(w_xv, l, 0)], tm=bp * n_mem, tn=TN, name="mem_v")
        p_mem_k.append(mk.reshape(bp, n_mem, xh, xhd))
        p_mem_v.append(mv.reshape(bp, n_mem, xh, xhd))
        o_p = xattn_prompt(q, mk, mv, batch=bp, seq=sp, n_mem=n_mem, n_heads=xh, hd=xhd)
        o_s = xattn_sample(q[tp:].reshape(bs, ss * xh, xhd), cache_mem_k, cache_mem_v, l)
        o = jnp.concatenate([o_p, o_s.reshape(ts, d)], axis=0)
        x = matmul([o], [(w_xo, l, 0)], res=x, tm=TM, tn=TN, name="xattn_o")

        u = rmsnorm(x, norm_ff[l])
        hmid = matmul([u], [(w_ff1, l, 0)], act="relu2", out_dtype=BF16, tm=TM, tn=TN, name="ff1")
        x = matmul_ksplit(hmid, w_ff2, l, x, tm=TM, tn=TN, tk=TK2, name="ff2")

    y_p = rmsnorm(x, norm_final, out_dtype=F32, row0=0, nrows=tp)
    y_s = rmsnorm(x, norm_final, out_dtype=F32, row0=tp, nrows=ts)
    dt = x_prompt.dtype
    st = lambda name: jnp.stack(outs[name]).astype(dt)
    return (y_p.reshape(bp, sp, d), y_s.reshape(bs, ss, d),
            st("pc"), st("pn"), st("pm"), st("ph"), st("pcb"), st("psb"),
            jnp.stack(p_mem_k), jnp.stack(p_mem_v),
            st("sc"), st("sn"), st("sm"), st("sh"), st("scb"), st("ssb"))
```

```python
import functools

import jax
import jax.numpy as jnp
from jax import lax
from jax.experimental import pallas as pl
from jax.experimental.pallas import tpu as pltpu

F32 = jnp.float32
BF16 = jnp.bfloat16
EPS = 1e-6
LRU_C = 8.0

V7X_LANES = 128
V7X_SUBLANES = 8
V7X_VMEM_BUDGET_BYTES = 62 * 1024 * 1024

MLSTM_CHUNK = 128
SCAN_CHUNK = 256


def _cparams(*semantics):
    return pltpu.CompilerParams(dimension_semantics=semantics, vmem_limit_bytes=V7X_VMEM_BUDGET_BYTES)


def _log_sigmoid(x):
    return jnp.minimum(x, 0.0) - jnp.log1p(jnp.exp(-jnp.abs(x)))


def _sigmoid(x):
    return 1.0 / (1.0 + jnp.exp(-x))


def _gelu_tanh(x):
    c = 0.7978845608028654
    return 0.5 * x * (1.0 + jnp.tanh(c * (x + 0.044715 * (x * x * x))))


def _expm1(x):
    u = jnp.exp(x)
    um1 = u - 1.0
    safe = jnp.where(um1 == 0.0, 1.0, jnp.log(u))
    return jnp.where(um1 == 0.0, x, jnp.where(um1 == -1.0, -1.0, um1 * x / safe))


def _split3(x):
    hi = x.astype(BF16)
    r1 = x - hi.astype(F32)
    mid = r1.astype(BF16)
    lo = (r1 - mid.astype(F32)).astype(BF16)
    return hi, mid, lo


def _dot01(mask_bf16, x):
    hi, mid, lo = _split3(x)
    acc = jnp.dot(mask_bf16, lo, preferred_element_type=F32)
    acc = acc + jnp.dot(mask_bf16, mid, preferred_element_type=F32)
    return acc + jnp.dot(mask_bf16, hi, preferred_element_type=F32)


def _rmsnorm_body(x_ref, g_ref, o_ref):
    x = x_ref[...]
    inv = lax.rsqrt(jnp.mean(x * x, axis=-1, keepdims=True) + EPS)
    o_ref[...] = ((x * inv) * g_ref[...]).astype(o_ref.dtype)


def rmsnorm(x, g, out_dtype=BF16, rows=256, row0=0, nrows=None):
    d = x.shape[1]
    nrows = x.shape[0] if nrows is None else nrows
    b0 = row0 // rows
    return pl.pallas_call(
        _rmsnorm_body,
        grid=(nrows // rows,),
        in_specs=[pl.BlockSpec((rows, d), lambda i: (b0 + i, 0)), pl.BlockSpec((1, d), lambda i: (0, 0))],
        out_specs=pl.BlockSpec((rows, d), lambda i: (i, 0)),
        out_shape=jax.ShapeDtypeStruct((nrows, d), out_dtype),
        compiler_params=_cparams("parallel"),
        name="rmsnorm",
    )(x, g.reshape(1, d))


def _mm_body(*refs, n_in, has_res, act, layer, tn, kc, nj, ni):
    x_refs = refs[:n_in]
    w_hbm = refs[n_in]
    res_ref = refs[n_in + 1] if has_res else None
    n_io = n_in + 1 + (1 if has_res else 0)
    o_ref, wbf, stage, sem = refs[n_io], refs[n_io + 1], refs[n_io + 2], refs[n_io + 3]
    j, i = pl.program_id(0), pl.program_id(1)

    def chunk_copy(jb, c):
        rows = pl.ds(pl.multiple_of(c * kc, kc), kc)
        cols = pl.ds(pl.multiple_of(jb * tn, tn), tn)
        src = w_hbm.at[rows, cols] if layer is None else w_hbm.at[layer, rows, cols]
        return pltpu.make_async_copy(src, stage, sem)

    @pl.when((j == 0) & (i == 0))
    def _():
        def first(c, carry):
            cp = chunk_copy(0, c)
            cp.start()
            cp.wait()
            wbf[0, pl.ds(pl.multiple_of(c * kc, kc), kc), :] = stage[...].astype(BF16)
            return carry
        lax.fori_loop(0, ni, first, 0)

    has_next = j + 1 < nj

    @pl.when(has_next)
    def _():
        chunk_copy(j + 1, i).start()

    slot = j % 2
    k0 = 0
    acc = None
    for x_ref in x_refs:
        kw = x_ref.shape[1]
        part = jnp.dot(x_ref[...], wbf[slot, k0:k0 + kw, :], preferred_element_type=F32)
        acc = part if acc is None else acc + part
        k0 += kw
    if act == "relu2":
        r = jnp.maximum(acc, 0.0)
        acc = r * r
    if has_res:
        acc = res_ref[...] + acc
    o_ref[...] = acc.astype(o_ref.dtype)

    @pl.when(has_next)
    def _():
        chunk_copy(j + 1, i).wait()
        wbf[1 - slot, pl.ds(pl.multiple_of(i * kc, kc), kc), :] = stage[...].astype(BF16)


def matmul(xs, w, *, layer=None, n_out=None, res=None, act=None, out_dtype=F32, tm, tn, name):
    m = xs[0].shape[0]
    kk = sum(x.shape[1] for x in xs)
    n = w.shape[-1] if n_out is None else n_out
    ni = m // tm
    kc = kk // ni
    assert w.shape[-2] == kk and kk % ni == 0 and kc % 16 == 0 and n % tn == 0 and m % tm == 0
    n_in = len(xs)
    in_specs = [pl.BlockSpec((tm, x.shape[1]), lambda j, i: (i, 0)) for x in xs]
    in_specs.append(pl.BlockSpec(memory_space=pl.ANY))
    args = list(xs) + [w]
    if res is not None:
        in_specs.append(pl.BlockSpec((tm, tn), lambda j, i: (i, j)))
        args.append(res)
    return pl.pallas_call(
        functools.partial(_mm_body, n_in=n_in, has_res=res is not None, act=act, layer=layer, tn=tn, kc=kc,
                          nj=n // tn, ni=ni),
        grid=(n // tn, ni),
        in_specs=in_specs,
        out_specs=pl.BlockSpec((tm, tn), lambda j, i: (i, j)),
        out_shape=jax.ShapeDtypeStruct((m, n), out_dtype),
        scratch_shapes=[pltpu.VMEM((2, kk, tn), BF16), pltpu.VMEM((kc, tn), F32), pltpu.SemaphoreType.DMA(())],
        compiler_params=_cparams("arbitrary", "arbitrary"),
        name=name,
    )(*args)


def _mm_acc_body(x_ref, w_ref, res_ref, o_ref):
    k = pl.program_id(2)

    @pl.when(k == 0)
    def _():
        o_ref[...] = res_ref[...] + jnp.dot(x_ref[...], w_ref[...], preferred_element_type=F32)

    @pl.when(k != 0)
    def _():
        o_ref[...] += jnp.dot(x_ref[...], w_ref[...], preferred_element_type=F32)


def matmul_ksplit(x, w, layer, res, *, tm, tn, tk, name):
    m, kk = x.shape
    n = w.shape[-1]
    return pl.pallas_call(
        _mm_acc_body,
        grid=(m // tm, n // tn, kk // tk),
        in_specs=[
            pl.BlockSpec((tm, tk), lambda i, j, k: (i, k)),
            pl.BlockSpec((None, tk, tn), lambda i, j, k: (layer, k, j)),
            pl.BlockSpec((tm, tn), lambda i, j, k: (i, j)),
        ],
        out_specs=pl.BlockSpec((tm, tn), lambda i, j, k: (i, j)),
        out_shape=jax.ShapeDtypeStruct((m, n), F32),
        compiler_params=_cparams("parallel", "parallel", "arbitrary"),
        name=name,
    )(x, w, res)


def _gates_body(u_ref, w_ref, b_ref, gc_ref, gr_ref, *, tm, n_prompt_tiles, seg_sample):
    L = MLSTM_CHUNK
    g = jnp.dot(u_ref[...], w_ref[...], preferred_element_type=F32) + b_ref[...]
    lane = lax.broadcasted_iota(jnp.int32, (tm, V7X_LANES), 1)
    val = jnp.where(lane < 8, g, _log_sigmoid(g))
    row = lax.broadcasted_iota(jnp.int32, (L, L), 0)
    col = lax.broadcasted_iota(jnp.int32, (L, L), 1)
    is_sample = pl.program_id(0) >= n_prompt_tiles
    tril = jnp.where(col <= row, 1.0, 0.0)
    tril_seg = jnp.where((col <= row) & ((row // seg_sample) == (col // seg_sample)), 1.0, 0.0)
    mask = jnp.where(is_sample, tril_seg, tril).astype(BF16)
    lane_l = lax.broadcasted_iota(jnp.int32, (L, V7X_LANES), 1)
    for s in range(tm // L):
        x = val[s * L:(s + 1) * L]
        out = jnp.where(lane_l < 8, x, _dot01(mask, x))
        gr_ref[:, s * L:(s + 1) * L] = out.T[:16]
        for c in range(16):
            gc_ref[c, s * L:(s + 1) * L, :] = jnp.broadcast_to(out[:, c:c + 1], (L, V7X_LANES))


def mlstm_gates(u, w_if, b_if, *, n_prompt_rows, seg_sample, tm=512):
    t, d = u.shape
    return pl.pallas_call(
        functools.partial(_gates_body, tm=tm, n_prompt_tiles=n_prompt_rows // tm, seg_sample=seg_sample),
        grid=(t // tm,),
        in_specs=[
            pl.BlockSpec((tm, d), lambda i: (i, 0)),
            pl.BlockSpec((d, V7X_LANES), lambda i: (0, 0)),
            pl.BlockSpec((1, V7X_LANES), lambda i: (0, 0)),
        ],
        out_specs=[
            pl.BlockSpec((16, tm, V7X_LANES), lambda i: (0, i, 0)),
            pl.BlockSpec((16, tm), lambda i: (0, i)),
        ],
        out_shape=[
            jax.ShapeDtypeStruct((16, t, V7X_LANES), F32),
            jax.ShapeDtypeStruct((16, t), F32),
        ],
        compiler_params=_cparams("parallel"),
        name="mlstm_gates",
    )(u, w_if, b_if)


def _head_out(hh, o, mh):
    hn = hh * lax.rsqrt(jnp.mean(hh * hh, axis=-1, keepdims=True) + EPS)
    return _sigmoid(o) * (hn * mh)


def _mlstm_prompt_body(q_ref, k_ref, v_ref, o_ref, ic_ref, bc_ref, ir_ref, br_ref, mh_ref,
                       ya_ref, c_ref, n_ref, m_ref, c_scr, n_scr, m_scr, *, n_heads, seq, scale):
    L = MLSTM_CHUNK
    c_scr[...] = jnp.zeros_like(c_scr)
    n_scr[...] = jnp.zeros_like(n_scr)
    m_scr[...] = jnp.zeros_like(m_scr)
    row = lax.broadcasted_iota(jnp.int32, (L, L), 0)
    col = lax.broadcasted_iota(jnp.int32, (L, L), 1)
    causal = col <= row

    def chunk(ci, carry):
        r0 = pl.multiple_of(ci * L, L)
        q = q_ref[pl.ds(r0, L), :] * scale
        k = k_ref[pl.ds(r0, L), :]
        v = v_ref[pl.ds(r0, L), :]
        bc = bc_ref[0, pl.ds(r0, L), :]
        ic = ic_ref[0, pl.ds(r0, L), :]
        i_row = ir_ref[0, :, pl.ds(r0, L)]
        b_row = br_ref[0, :, pl.ds(r0, L)]
        m_prev = m_scr[...]
        c_prev = c_scr[...]
        n_prev = n_scr[...]
        dmat = jnp.where(causal, bc - b_row + i_row, -jnp.inf)
        g = bc + m_prev
        mt = jnp.maximum(g, jnp.max(dmat, axis=-1, keepdims=True))
        w_inter = jnp.exp(g - mt)
        qb = q.astype(BF16)
        kb = k.astype(BF16)
        vb = v.astype(BF16)
        sc = lax.dot_general(qb, kb, (((1,), (1,)), ((), ())), preferred_element_type=F32) * jnp.exp(dmat - mt)
        inter = jnp.dot(qb, c_prev.astype(BF16), preferred_element_type=F32)
        num = jnp.concatenate([w_inter, w_inter], axis=1) * inter + jnp.dot(sc.astype(BF16), vb, preferred_element_type=F32)
        qn = jnp.sum(q * n_prev, axis=-1, keepdims=True)
        den = w_inter[:, :1] * qn + jnp.sum(sc, axis=-1, keepdims=True)
        hh = num / jnp.maximum(jnp.abs(den), jnp.exp(-mt[:, :1]))
        ya_ref[pl.ds(r0, L), :] = _head_out(hh, o_ref[pl.ds(r0, L), :], mh_ref[...]).astype(ya_ref.dtype)
        m_new = mt[L - 1:L, :1]
        b_last = b_row[:, L - 1:L]
        decay = jnp.exp(b_last + m_prev - m_new)
        kw = k * jnp.exp(b_last - bc + ic - m_new)
        c_scr[...] = decay * c_prev + lax.dot_general(kw.astype(BF16), vb, (((0,), (0,)), ((), ())), preferred_element_type=F32)
        n_scr[...] = decay * n_prev + jnp.sum(kw, axis=0, keepdims=True)
        m_scr[...] = m_new
        return carry

    lax.fori_loop(0, seq // L, chunk, 0)
    c_ref[0, 0] = c_scr[...]
    n_ref[0, 0] = n_scr[...]
    m_ref[0, 0] = m_scr[...]


def mlstm_prompt(proj, gc, gr, mh_norm, *, batch, seq, n_heads, dk, dv, col_q, col_k, col_v, col_o):
    L = MLSTM_CHUNK
    assert dk == L and dv == 2 * L
    rows = batch * seq
    return pl.pallas_call(
        functools.partial(_mlstm_prompt_body, n_heads=n_heads, seq=seq, scale=dk ** -0.5),
        grid=(batch, n_heads),
        in_specs=[
            pl.BlockSpec((seq, dk), lambda b, h: (b, col_q // dk + h)),
            pl.BlockSpec((seq, dk), lambda b, h: (b, col_k // dk + h)),
            pl.BlockSpec((seq, dv), lambda b, h: (b, col_v // dv + h)),
            pl.BlockSpec((seq, dv), lambda b, h: (b, col_o // dv + h)),
            pl.BlockSpec((1, seq, V7X_LANES), lambda b, h: (h, b, 0)),
            pl.BlockSpec((1, seq, V7X_LANES), lambda b, h: (h + n_heads, b, 0)),
            pl.BlockSpec((1, 1, seq), lambda b, h: (h, 0, b)),
            pl.BlockSpec((1, 1, seq), lambda b, h: (h + n_heads, 0, b)),
            pl.BlockSpec((1, dv), lambda b, h: (0, h)),
        ],
        out_specs=[
            pl.BlockSpec((seq, dv), lambda b, h: (b, h)),
            pl.BlockSpec((1, 1, dk, dv), lambda b, h: (b, h, 0, 0)),
            pl.BlockSpec((1, 1, 1, dk), lambda b, h: (b, h, 0, 0)),
            pl.BlockSpec((1, 1, 1, 1), lambda b, h: (b, h, 0, 0)),
        ],
        out_shape=[
            jax.ShapeDtypeStruct((rows, n_heads * dv), BF16),
            jax.ShapeDtypeStruct((batch, n_heads, dk, dv), F32),
            jax.ShapeDtypeStruct((batch, n_heads, 1, dk), F32),
            jax.ShapeDtypeStruct((batch, n_heads, 1, 1), F32),
        ],
        scratch_shapes=[pltpu.VMEM((dk, dv), F32), pltpu.VMEM((1, dk), F32), pltpu.VMEM((1, 1), F32)],
        compiler_params=_cparams("parallel", "parallel"),
        name="mlstm_prompt",
    )(proj, proj, proj, proj, gc, gc, gr, gr, mh_norm)


def _mlstm_sample_body(q_ref, k_ref, v_ref, o_ref, ic_ref, bc_ref, ir_ref, br_ref, mh_ref, c0_ref, n0_ref, m0_ref,
                       ya_ref, c_ref, n_ref, m_ref, inter_scr, *, n_heads, seg, scale):
    L = MLSTM_CHUNK
    nb = L // seg
    row = lax.broadcasted_iota(jnp.int32, (L, L), 0)
    col = lax.broadcasted_iota(jnp.int32, (L, L), 1)
    same = (row // seg) == (col // seg)
    causal = (col <= row) & same
    last = col == (row // seg) * seg + (seg - 1)
    q = q_ref[...] * scale
    k = k_ref[...]
    v = v_ref[...]
    bc = bc_ref[0]
    ic = ic_ref[0]
    i_row = ir_ref[0]
    b_row = br_ref[0]
    m_prev = m0_ref[0]
    n_prev = n0_ref[0]
    dmat = jnp.where(causal, bc - b_row + i_row, -jnp.inf)
    g = bc + m_prev
    mt = jnp.maximum(g, jnp.max(dmat, axis=-1, keepdims=True))
    w_inter = jnp.exp(g - mt)
    qb = q.astype(BF16)
    kb = k.astype(BF16)
    vb = v.astype(BF16)
    sc = lax.dot_general(qb, kb, (((1,), (1,)), ((), ())), preferred_element_type=F32) * jnp.exp(dmat - mt)

    per_group = V7X_SUBLANES // seg
    sub = lax.broadcasted_iota(jnp.int32, (V7X_SUBLANES, 2 * L), 0)
    for gi in range(L // V7X_SUBLANES):
        q8 = qb[gi * V7X_SUBLANES:(gi + 1) * V7X_SUBLANES]
        acc = jnp.zeros((V7X_SUBLANES, 2 * L), F32)
        for j in range(per_group):
            r = jnp.dot(q8, c0_ref[0, gi * per_group + j, 0].astype(BF16), preferred_element_type=F32)
            acc = jnp.where((sub // seg) == j, r, acc)
        inter_scr[gi * V7X_SUBLANES:(gi + 1) * V7X_SUBLANES, :] = acc
    inter = inter_scr[...]

    num = jnp.concatenate([w_inter, w_inter], axis=1) * inter + jnp.dot(sc.astype(BF16), vb, preferred_element_type=F32)
    qn = jnp.sum(q * n_prev, axis=-1, keepdims=True)
    den = w_inter[:, :1] * qn + jnp.sum(sc, axis=-1, keepdims=True)
    hh = num / jnp.maximum(jnp.abs(den), jnp.exp(-mt[:, :1]))
    ya_ref[...] = _head_out(hh, o_ref[...], mh_ref[...]).astype(ya_ref.dtype)

    b_last = jnp.sum(jnp.where(last, jnp.broadcast_to(b_row, (L, L)), 0.0), axis=-1, keepdims=True)
    mt_row = mt.T
    m_new = jnp.sum(jnp.where(last, mt_row, 0.0), axis=-1, keepdims=True)
    decay = jnp.exp(b_last + m_prev - m_new)
    kw = k * jnp.exp(b_last - bc + ic - m_new)
    seg_ones = jnp.where(same, 1.0, 0.0).astype(BF16)
    n_ref[0] = decay * n_prev + _dot01(seg_ones, kw)
    m_ref[0] = jnp.broadcast_to(m_new, (L, L))
    rowk = lax.broadcasted_iota(jnp.int32, (L, L), 0)
    for bi in range(nb):
        kw_b = jnp.where((rowk // seg) == bi, kw, 0.0).astype(BF16)
        upd = lax.dot_general(kw_b, vb, (((0,), (0,)), ((), ())), preferred_element_type=F32)
        d_b = decay[bi * seg:bi * seg + 1, :1]
        c_ref[0, bi, 0] = d_b * c0_ref[0, bi, 0] + upd


def mlstm_sample(proj, gc, gr, mh_norm, c0, n0_tok, m0_tok, *, row0, n_seq, seg, n_heads, dk, dv,
                 col_q, col_k, col_v, col_o):
    L = MLSTM_CHUNK
    assert dk == L and dv == 2 * L and V7X_SUBLANES % seg == 0 and row0 % L == 0
    rows = n_seq * seg
    nb = L // seg
    t0 = row0 // L
    return pl.pallas_call(
        functools.partial(_mlstm_sample_body, n_heads=n_heads, seg=seg, scale=dk ** -0.5),
        grid=(rows // L, n_heads),
        in_specs=[
            pl.BlockSpec((L, dk), lambda j, h: (t0 + j, col_q // dk + h)),
            pl.BlockSpec((L, dk), lambda j, h: (t0 + j, col_k // dk + h)),
            pl.BlockSpec((L, dv), lambda j, h: (t0 + j, col_v // dv + h)),
            pl.BlockSpec((L, dv), lambda j, h: (t0 + j, col_o // dv + h)),
            pl.BlockSpec((1, L, V7X_LANES), lambda j, h: (h, t0 + j, 0)),
            pl.BlockSpec((1, L, V7X_LANES), lambda j, h: (h + n_heads, t0 + j, 0)),
            pl.BlockSpec((1, 1, L), lambda j, h: (h, 0, t0 + j)),
            pl.BlockSpec((1, 1, L), lambda j, h: (h + n_heads, 0, t0 + j)),
            pl.BlockSpec((1, dv), lambda j, h: (0, h)),
            pl.BlockSpec((1, nb, 1, dk, dv), lambda j, h: (0, j, h, 0, 0)),
            pl.BlockSpec((1, L, dk), lambda j, h: (h, j, 0)),
            pl.BlockSpec((1, L, L), lambda j, h: (h, j, 0)),
        ],
        out_specs=[
            pl.BlockSpec((L, dv), lambda j, h: (j, h)),
            pl.BlockSpec((1, nb, 1, dk, dv), lambda j, h: (0, j, h, 0, 0)),
            pl.BlockSpec((1, L, dk), lambda j, h: (h, j, 0)),
            pl.BlockSpec((1, L, L), lambda j, h: (h, j, 0)),
        ],
        out_shape=[
            jax.ShapeDtypeStruct((rows, n_heads * dv), BF16),
            jax.ShapeDtypeStruct((1, n_seq, n_heads, dk, dv), F32),
            jax.ShapeDtypeStruct((n_heads, rows, dk), F32),
            jax.ShapeDtypeStruct((n_heads, rows, L), F32),
        ],
        scratch_shapes=[pltpu.VMEM((L, dv), F32)],
        compiler_params=_cparams("parallel", "parallel"),
        name="mlstm_sample",
    )(proj, proj, proj, proj, gc, gc, gr, gr, mh_norm, c0, n0_tok, m0_tok)


def _lru_gates(xc, wa_ref, wi_ref, ba, bi, lam):
    nblk = wa_ref.shape[0]
    blk = wa_ref.shape[1]
    rs, igs = [], []
    for n in range(nblk):
        xb = xc[:, n * blk:(n + 1) * blk].astype(BF16)
        rs.append(jnp.dot(xb, wa_ref[n], preferred_element_type=F32))
        igs.append(jnp.dot(xb, wi_ref[n], preferred_element_type=F32))
    r = _sigmoid(jnp.concatenate(rs, axis=1) + ba)
    ig = _sigmoid(jnp.concatenate(igs, axis=1) + bi)
    log_a = LRU_C * r * _log_sigmoid(lam)
    a = jnp.exp(log_a)
    u = jnp.sqrt(-_expm1(2.0 * log_a)) * (ig * xc)
    return a, u


def _lru_prompt_body(x_ref, g_ref, halo_ref, h0_ref, cw_ref, cb_ref, wa_ref, wi_ref, ba_ref, bi_ref, lam_ref,
                     y_ref, hl_ref, h_scr, *, seq):
    Lc = SCAN_CHUNK
    cbw = x_ref.shape[1]
    h_scr[...] = h0_ref[0]
    row = lax.broadcasted_iota(jnp.int32, (Lc, cbw), 0)
    cw = cw_ref[...]
    width = cw.shape[0]

    def chunk(ci, carry):
        r0 = pl.multiple_of(ci * Lc, Lc)
        x = x_ref[pl.ds(r0, Lc), :]
        prev = x_ref[pl.ds(pl.multiple_of(jnp.maximum(r0 - V7X_SUBLANES, 0), V7X_SUBLANES), V7X_SUBLANES), :]
        prev = jnp.where(ci == 0, halo_ref[0], prev)
        xfull = jnp.concatenate([prev, x], axis=0)
        xc = None
        for j in range(width - 1, 0, -1):
            term = pltpu.roll(xfull, j, 0)[V7X_SUBLANES:] * cw[width - 1 - j:width - j]
            xc = term if xc is None else xc + term
        xc = xc + x * cw[width - 1:width] + cb_ref[...]
        a, u = _lru_gates(xc, wa_ref, wi_ref, ba_ref[...], bi_ref[...], lam_ref[...])
        d = 1
        while d < Lc:
            ok = row >= d
            a_sh = jnp.where(ok, pltpu.roll(a, d, 0), 1.0)
            u_sh = jnp.where(ok, pltpu.roll(u, d, 0), 0.0)
            u = a * u_sh + u
            a = a * a_sh
            d *= 2
        hs = u + a * h_scr[...]
        h_scr[...] = hs[Lc - 1:Lc]
        y_ref[pl.ds(r0, Lc), :] = (hs * _gelu_tanh(g_ref[pl.ds(r0, Lc), :])).astype(y_ref.dtype)
        return carry

    lax.fori_loop(0, seq // Lc, chunk, 0)
    hl_ref[0] = h_scr[...]


def lru_prompt(proj, halo, h0, cw, cb, wa, wi, ba, bi, lam, *, batch, seq, width, col_x, col_g, cbw=256):
    blk = wa.shape[1]
    nper = cbw // blk
    vec = lambda b, c: (0, c)
    return pl.pallas_call(
        functools.partial(_lru_prompt_body, seq=seq),
        grid=(batch, width // cbw),
        in_specs=[
            pl.BlockSpec((seq, cbw), lambda b, c: (b, col_x // cbw + c)),
            pl.BlockSpec((seq, cbw), lambda b, c: (b, col_g // cbw + c)),
            pl.BlockSpec((1, V7X_SUBLANES, cbw), lambda b, c: (b, 0, c)),
            pl.BlockSpec((1, 1, cbw), lambda b, c: (b, 0, c)),
            pl.BlockSpec((cw.shape[0], cbw), vec),
            pl.BlockSpec((1, cbw), vec),
            pl.BlockSpec((nper, blk, blk), lambda b, c: (c, 0, 0)),
            pl.BlockSpec((nper, blk, blk), lambda b, c: (c, 0, 0)),
            pl.BlockSpec((1, cbw), vec),
            pl.BlockSpec((1, cbw), vec),
            pl.BlockSpec((1, cbw), vec),
        ],
        out_specs=[
            pl.BlockSpec((seq, cbw), lambda b, c: (b, c)),
            pl.BlockSpec((1, 1, cbw), lambda b, c: (b, 0, c)),
        ],
        out_shape=[
            jax.ShapeDtypeStruct((batch * seq, width), BF16),
            jax.ShapeDtypeStruct((batch, 1, width), F32),
        ],
        scratch_shapes=[pltpu.VMEM((1, cbw), F32)],
        compiler_params=_cparams("parallel", "parallel"),
        name="lru_prompt",
    )(proj, proj, halo, h0, cw, cb, wa, wi, ba, bi, lam)


def _lru_sample_body(x_ref, g_ref, st_ref, h0_ref, cw_ref, cb_ref, wa_ref, wi_ref, ba_ref, bi_ref, lam_ref,
                     y_ref, hl_ref, *, seg):
    cw = cw_ref[...]
    width = cw.shape[0]
    hist = [st_ref[j] for j in range(width - 1)]
    h = h0_ref[...]
    for t in range(seg):
        x = x_ref[t]
        taps = hist + [x]
        xc = taps[0] * cw[0:1]
        for j in range(1, width):
            xc = xc + taps[j] * cw[j:j + 1]
        xc = xc + cb_ref[...]
        a, u = _lru_gates(xc, wa_ref, wi_ref, ba_ref[...], bi_ref[...], lam_ref[...])
        h = a * h + u
        y_ref[t] = (h * _gelu_tanh(g_ref[t])).astype(y_ref.dtype)
        hist = hist[1:] + [x]
    hl_ref[...] = h


def lru_sample(x_t, g_t, st, h0, cw, cb, wa, wi, ba, bi, lam, *, cbw=256):
    seg, n_seq, width = x_t.shape
    blk = wa.shape[1]
    nper = cbw // blk
    vec = lambda c: (0, c)
    return pl.pallas_call(
        functools.partial(_lru_sample_body, seg=seg),
        grid=(width // cbw,),
        in_specs=[
            pl.BlockSpec((seg, n_seq, cbw), lambda c: (0, 0, c)),
            pl.BlockSpec((seg, n_seq, cbw), lambda c: (0, 0, c)),
            pl.BlockSpec((st.shape[0], n_seq, cbw), lambda c: (0, 0, c)),
            pl.BlockSpec((n_seq, cbw), vec),
            pl.BlockSpec((cw.shape[0], cbw), vec),
            pl.BlockSpec((1, cbw), vec),
            pl.BlockSpec((nper, blk, blk), lambda c: (c, 0, 0)),
            pl.BlockSpec((nper, blk, blk), lambda c: (c, 0, 0)),
            pl.BlockSpec((1, cbw), vec),
            pl.BlockSpec((1, cbw), vec),
            pl.BlockSpec((1, cbw), vec),
        ],
        out_specs=[
            pl.BlockSpec((seg, n_seq, cbw), lambda c: (0, 0, c)),
            pl.BlockSpec((n_seq, cbw), vec),
        ],
        out_shape=[
            jax.ShapeDtypeStruct((seg, n_seq, width), BF16),
            jax.ShapeDtypeStruct((n_seq, width), F32),
        ],
        compiler_params=_cparams("parallel"),
        name="lru_sample",
    )(x_t, g_t, st, h0, cw, cb, wa, wi, ba, bi, lam)


def _sconv_prompt_body(bg_ref, cg_ref, hx_ref, halo_ref, cw_ref, y_ref, tail_ref, *, seq):
    Lc = SCAN_CHUNK
    cw = cw_ref[...]
    width = cw.shape[0]

    def chunk(ci, carry):
        r0 = pl.multiple_of(ci * Lc, Lc)
        p = cg_ref[pl.ds(r0, Lc), :] * hx_ref[pl.ds(r0, Lc), :]
        rp = pl.multiple_of(jnp.maximum(r0 - V7X_SUBLANES, 0), V7X_SUBLANES)
        prev = cg_ref[pl.ds(rp, V7X_SUBLANES), :] * hx_ref[pl.ds(rp, V7X_SUBLANES), :]
        prev = jnp.where(ci == 0, halo_ref[0], prev)
        pfull = jnp.concatenate([prev, p], axis=0)
        z = None
        for j in range(width - 1, 0, -1):
            term = pltpu.roll(pfull, j, 0)[V7X_SUBLANES:] * cw[width - 1 - j:width - j]
            z = term if z is None else z + term
        z = z + p * cw[width - 1:width]
        y_ref[pl.ds(r0, Lc), :] = (bg_ref[pl.ds(r0, Lc), :] * z).astype(y_ref.dtype)
        return carry

    lax.fori_loop(0, seq // Lc, chunk, 0)
    rt = seq - V7X_SUBLANES
    tail_ref[0] = cg_ref[pl.ds(rt, V7X_SUBLANES), :] * hx_ref[pl.ds(rt, V7X_SUBLANES), :]


def sconv_prompt(proj, halo, cw, *, batch, seq, width, cbw=512):
    nb = width // cbw
    return pl.pallas_call(
        functools.partial(_sconv_prompt_body, seq=seq),
        grid=(batch, nb),
        in_specs=[
            pl.BlockSpec((seq, cbw), lambda b, c: (b, c)),
            pl.BlockSpec((seq, cbw), lambda b, c: (b, nb + c)),
            pl.BlockSpec((seq, cbw), lambda b, c: (b, 2 * nb + c)),
            pl.BlockSpec((1, V7X_SUBLANES, cbw), lambda b, c: (b, 0, c)),
            pl.BlockSpec((cw.shape[0], cbw), lambda b, c: (0, c)),
        ],
        out_specs=[
            pl.BlockSpec((seq, cbw), lambda b, c: (b, c)),
            pl.BlockSpec((1, V7X_SUBLANES, cbw), lambda b, c: (b, 0, c)),
        ],
        out_shape=[
            jax.ShapeDtypeStruct((batch * seq, width), BF16),
            jax.ShapeDtypeStruct((batch, V7X_SUBLANES, width), F32),
        ],
        compiler_params=_cparams("parallel", "parallel"),
        name="sconv_prompt",
    )(proj, proj, proj, halo, cw)


def _sconv_sample_body(bg_ref, cg_ref, hx_ref, st_ref, cw_ref, y_ref, ps_ref, *, seg):
    cw = cw_ref[...]
    width = cw.shape[0]
    hist = [st_ref[j] for j in range(width - 1)]
    for t in range(seg):
        p = cg_ref[t] * hx_ref[t]
        ps_ref[t] = p
        taps = hist + [p]
        z = taps[0] * cw[0:1]
        for j in range(1, width):
            z = z + taps[j] * cw[j:j + 1]
        y_ref[t] = (bg_ref[t] * z).astype(y_ref.dtype)
        hist = hist[1:] + [p]


def sconv_sample(proj_t, st, cw, *, width, cbw=512):
    seg, n_seq, _ = proj_t.shape
    nb = width // cbw
    return pl.pallas_call(
        functools.partial(_sconv_sample_body, seg=seg),
        grid=(nb,),
        in_specs=[
            pl.BlockSpec((seg, n_seq, cbw), lambda c: (0, 0, c)),
            pl.BlockSpec((seg, n_seq, cbw), lambda c: (0, 0, nb + c)),
            pl.BlockSpec((seg, n_seq, cbw), lambda c: (0, 0, 2 * nb + c)),
            pl.BlockSpec((st.shape[0], n_seq, cbw), lambda c: (0, 0, c)),
            pl.BlockSpec((cw.shape[0], cbw), lambda c: (0, c)),
        ],
        out_specs=[
            pl.BlockSpec((seg, n_seq, cbw), lambda c: (0, 0, c)),
            pl.BlockSpec((seg, n_seq, cbw), lambda c: (0, 0, c)),
        ],
        out_shape=[
            jax.ShapeDtypeStruct((seg, n_seq, width), BF16),
            jax.ShapeDtypeStruct((seg, n_seq, width), F32),
        ],
        compiler_params=_cparams("parallel"),
        name="sconv_sample",
    )(proj_t, proj_t, proj_t, st, cw)


def _softmax_rows(s):
    e = jnp.exp(s - jnp.max(s, axis=-1, keepdims=True))
    return e / jnp.sum(e, axis=-1, keepdims=True)


def _xattn_prompt_body(q_ref, k_ref, v_ref, o_ref, *, scale):
    s = lax.dot_general(q_ref[...], k_ref[...].astype(BF16), (((1,), (1,)), ((), ())), preferred_element_type=F32) * scale
    p = _softmax_rows(s).astype(BF16)
    o_ref[...] = jnp.dot(p, v_ref[...].astype(BF16), preferred_element_type=F32).astype(o_ref.dtype)


def xattn_prompt(q, mk, mv, *, batch, seq, n_mem, n_heads, hd, tq=1024):
    nq = seq // tq
    return pl.pallas_call(
        functools.partial(_xattn_prompt_body, scale=hd ** -0.5),
        grid=(batch, n_heads, nq),
        in_specs=[
            pl.BlockSpec((tq, hd), lambda b, h, i: (b * nq + i, h)),
            pl.BlockSpec((n_mem, hd), lambda b, h, i: (b, h)),
            pl.BlockSpec((n_mem, hd), lambda b, h, i: (b, h)),
        ],
        out_specs=pl.BlockSpec((tq, hd), lambda b, h, i: (b * nq + i, h)),
        out_shape=jax.ShapeDtypeStruct((batch * seq, n_heads * hd), BF16),
        compiler_params=_cparams("parallel", "parallel", "parallel"),
        name="xattn_prompt",
    )(q, mk, mv)


def _xattn_sample_body(q_ref, k_ref, v_ref, o_ref, *, n_heads, scale):
    n_mem, hd = k_ref.shape[2], k_ref.shape[4]
    k2 = k_ref[0, 0].reshape(n_mem * n_heads, hd).astype(BF16)
    v2 = v_ref[0, 0].reshape(n_mem * n_heads, hd).astype(BF16)
    s = lax.dot_general(q_ref[0], k2, (((1,), (1,)), ((), ())), preferred_element_type=F32) * scale
    row = lax.broadcasted_iota(jnp.int32, s.shape, 0)
    col = lax.broadcasted_iota(jnp.int32, s.shape, 1)
    s = jnp.where((row % n_heads) == (col % n_heads), s, -jnp.inf)
    p = _softmax_rows(s).astype(BF16)
    o_ref[0] = jnp.dot(p, v2, preferred_element_type=F32).astype(o_ref.dtype)


def xattn_sample(q3, cache_k, cache_v, layer):
    n_seq, rows, hd = q3.shape
    _, _, n_mem, n_heads, _ = cache_k.shape
    kv_spec = pl.BlockSpec((1, 1, n_mem, n_heads, hd), lambda b: (layer, b, 0, 0, 0))
    return pl.pallas_call(
        functools.partial(_xattn_sample_body, n_heads=n_heads, scale=hd ** -0.5),
        grid=(n_seq,),
        in_specs=[pl.BlockSpec((1, rows, hd), lambda b: (b, 0, 0)), kv_spec, kv_spec],
        out_specs=pl.BlockSpec((1, rows, hd), lambda b: (b, 0, 0)),
        out_shape=jax.ShapeDtypeStruct((n_seq, rows, hd), BF16),
        compiler_params=_cparams("parallel"),
        name="xattn_sample",
    )(q3, cache_k, cache_v)


def kernel(x_prompt, x_sample, mem_prompt, cache_mem_k, cache_mem_v, state_mlstm_c, state_mlstm_n, state_mlstm_m, state_lru_h, state_lru_conv, state_sconv, norm_mix, norm_x, norm_ff, norm_final, w_in_e, b_if, mh_norm, conv_b_w, conv_b_b, lru_wa, lru_ba, lru_wi, lru_bi, lru_lam, w_out_e, w_in_o, conv_c_w, w_out_o, w_xq, w_xk, w_xv, w_xo, w_ff1, w_ff2):
    bp, sp, d = x_prompt.shape
    bs, ss, _ = x_sample.shape
    n_mem = mem_prompt.shape[1]
    depth, _, _, xh, xhd = cache_mem_k.shape
    _, _, ah, dk, dv = state_mlstm_c.shape
    bw = state_lru_h.shape[-1]
    bconv = state_lru_conv.shape[2] + 1
    cconv = state_sconv.shape[2] + 1
    cwid = state_sconv.shape[-1]
    aw = ah * dv
    aqk = ah * dk
    tp = bp * sp
    ts = bs * ss
    t = tp + ts
    TM = 1088
    TN = 1024
    TK2 = 4096
    TM_MEM = 128
    assert t % TM == 0 and tp % 512 == 0 and ts == 512
    w_ff2 = w_ff2.astype(BF16)

    x = jnp.concatenate([x_prompt.reshape(tp, d), x_sample.reshape(ts, d)], axis=0)
    mem_bf = mem_prompt.reshape(bp * n_mem, d).astype(BF16)

    outs = {}
    ie = io = 0
    p_mem_k, p_mem_v = [], []
    for l in range(depth):
        u = rmsnorm(x, norm_mix[l])
        if l % 2 == 0:
            w = w_in_e[ie]
            o_qkvo = 2 * aqk + 2 * aw
            o_if = o_qkvo + 2 * ah
            w_if = jnp.pad(w[:, o_qkvo:o_if], ((0, 0), (0, V7X_LANES - 2 * ah))).astype(BF16)
            bias_if = jnp.pad(b_if[ie], (0, V7X_LANES - 2 * ah)).reshape(1, V7X_LANES)
            col_q, col_k, col_v, col_o = 0, aqk, 2 * aqk, 2 * aqk + aw
            col_x, col_g = 0, bw
            proj = matmul([u], w_in_e, layer=ie, n_out=o_qkvo, tm=TM, tn=TN, name="in_proj_even_qkvo")
            proj_b = matmul([u], w[:, o_if:], tm=TM, tn=TN, name="in_proj_even_lru")
            gc, gr = mlstm_gates(u, w_if, bias_if, n_prompt_rows=tp, seg_sample=ss)
            gr = gr.reshape(16, 1, t)
            mh = mh_norm[ie].reshape(1, aw)
            mkw = dict(n_heads=ah, dk=dk, dv=dv, col_q=col_q, col_k=col_k, col_v=col_v, col_o=col_o)
            ya_p, pc, pn, pm = mlstm_prompt(proj, gc, gr, mh, batch=bp, seq=sp, **mkw)
            n0_tok = jnp.repeat(jnp.transpose(state_mlstm_n[ie], (1, 0, 2)), ss, axis=1)
            m0_tok = jnp.broadcast_to(jnp.repeat(state_mlstm_m[ie].T, ss, axis=1)[:, :, None], (ah, ts, MLSTM_CHUNK))
            ya_s, sc_, sn_tok, sm_tok = mlstm_sample(proj, gc, gr, mh, state_mlstm_c[ie:ie + 1], n0_tok, m0_tok,
                                                     row0=tp, n_seq=bs, seg=ss, **mkw)
            outs.setdefault("pc", []).append(pc.reshape(bp, ah, dk, dv))
            outs.setdefault("pn", []).append(pn.reshape(bp, ah, dk))
            outs.setdefault("pm", []).append(pm.reshape(bp, ah))
            outs.setdefault("sc", []).append(sc_[0])
            outs.setdefault("sn", []).append(jnp.transpose(sn_tok[:, ss - 1::ss, :], (1, 0, 2)))
            outs.setdefault("sm", []).append(sm_tok[:, ss - 1::ss, 0].T)

            cw = conv_b_w[ie]
            cbias = conv_b_b[ie].reshape(1, bw)
            wa = lru_wa[ie].astype(BF16)
            wi = lru_wi[ie].astype(BF16)
            ba = lru_ba[ie].reshape(1, bw)
            bi = lru_bi[ie].reshape(1, bw)
            lam = lru_lam[ie].reshape(1, bw)
            halo_p = jnp.zeros((bp, V7X_SUBLANES, bw), F32)
            h0_p = jnp.zeros((bp, 1, bw), F32)
            yb_p, ph = lru_prompt(proj_b, halo_p, h0_p, cw, cbias, wa, wi, ba, bi, lam,
                                  batch=bp, seq=sp, width=bw, col_x=col_x, col_g=col_g)
            st_s = jnp.transpose(state_lru_conv[ie], (1, 0, 2))
            xb_p = proj_b[:tp, col_x:col_x + bw].reshape(bp, sp, bw)
            xb_s = proj_b[tp:, col_x:col_x + bw].reshape(bs, ss, bw)
            gb_s = proj_b[tp:, col_g:col_g + bw].reshape(bs, ss, bw)
            yb_s, sh = lru_sample(jnp.transpose(xb_s, (1, 0, 2)), jnp.transpose(gb_s, (1, 0, 2)), st_s,
                                  state_lru_h[ie], cw, cbias, wa, wi, ba, bi, lam)
            yb_s = jnp.transpose(yb_s, (1, 0, 2)).reshape(ts, bw)
            outs.setdefault("ph", []).append(ph.reshape(bp, bw))
            outs.setdefault("pcb", []).append(xb_p[:, sp - (bconv - 1):])
            outs.setdefault("sh", []).append(sh)
            outs.setdefault("scb", []).append(jnp.concatenate([state_lru_conv[ie], xb_s], axis=1)[:, ss:])

            ya = jnp.concatenate([ya_p, ya_s], axis=0)
            yb = jnp.concatenate([yb_p, yb_s], axis=0)
            x = matmul([ya, yb], w_out_e, layer=ie, res=x, tm=TM, tn=TN, name="out_proj_even")
            ie += 1
        else:
            proj = matmul([u], w_in_o, layer=io, tm=TM, tn=TN, name="in_proj_odd")
            cw = conv_c_w[io]
            halo_p = jnp.zeros((bp, V7X_SUBLANES, cwid), F32)
            y_p, tail = sconv_prompt(proj, halo_p, cw, batch=bp, seq=sp, width=cwid)
            st_s = jnp.transpose(state_sconv[io], (1, 0, 2))
            proj_t = jnp.transpose(proj[tp:].reshape(bs, ss, 3 * cwid), (1, 0, 2))
            y_s, ps = sconv_sample(proj_t, st_s, cw, width=cwid)
            y_s = jnp.transpose(y_s, (1, 0, 2)).reshape(ts, cwid)
            outs.setdefault("psb", []).append(tail[:, V7X_SUBLANES - (cconv - 1):])
            ps_b = jnp.transpose(ps, (1, 0, 2))
            outs.setdefault("ssb", []).append(jnp.concatenate([state_sconv[io], ps_b], axis=1)[:, ss:])
            y = jnp.concatenate([y_p, y_s], axis=0)
            x = matmul([y], w_out_o, layer=io, res=x, tm=TM, tn=TN, name="out_proj_odd")
            io += 1

        u = rmsnorm(x, norm_x[l])
        q = matmul([u], w_xq, layer=l, out_dtype=BF16, tm=TM, tn=TN, name="xattn_q")
        mk = matmul([mem_bf], w_xk, layer=l, tm=TM_MEM, tn=TN, name="mem_k")
        mv = matmul([mem_bf], w_xv, layer=l, tm=TM_MEM, tn=TN, name="mem_v")
        p_mem_k.append(mk.reshape(bp, n_mem, xh, xhd))
        p_mem_v.append(mv.reshape(bp, n_mem, xh, xhd))
        o_p = xattn_prompt(q, mk, mv, batch=bp, seq=sp, n_mem=n_mem, n_heads=xh, hd=xhd)
        o_s = xattn_sample(q[tp:].reshape(bs, ss * xh, xhd), cache_mem_k, cache_mem_v, l)
        o = jnp.concatenate([o_p, o_s.reshape(ts, d)], axis=0)
        x = matmul([o], w_xo, layer=l, res=x, tm=TM, tn=TN, name="xattn_o")

        u = rmsnorm(x, norm_ff[l])
        hmid = matmul([u], w_ff1, layer=l, act="relu2", out_dtype=BF16, tm=TM, tn=TN, name="ff1")
        x = matmul_ksplit(hmid, w_ff2, l, x, tm=TM, tn=TN, tk=TK2, name="ff2")

    y_p = rmsnorm(x, norm_final, out_dtype=F32, row0=0, nrows=tp)
    y_s = rmsnorm(x, norm_final, out_dtype=F32, row0=tp, nrows=ts)
    dt = x_prompt.dtype
    st = lambda name: jnp.stack(outs[name]).astype(dt)
    return (y_p.reshape(bp, sp, d), y_s.reshape(bs, ss, d),
            st("pc"), st("pn"), st("pm"), st("ph"), st("pcb"), st("psb"),
            jnp.stack(p_mem_k), jnp.stack(p_mem_v),
            st("sc"), st("sn"), st("sm"), st("sh"), st("scb"), st("ssb"))
```

```python
import functools

import jax
import jax.numpy as jnp
from jax import lax
from jax.experimental import pallas as pl
from jax.experimental.pallas import tpu as pltpu

F32 = jnp.float32
BF16 = jnp.bfloat16
EPS = 1e-6
LRU_C = 8.0

V7X_LANES = 128
V7X_SUBLANES = 8
V7X_VMEM_BUDGET_BYTES = 62 * 1024 * 1024

MLSTM_CHUNK = 128
SCAN_CHUNK = 256


def _cparams(*semantics):
    return pltpu.CompilerParams(dimension_semantics=semantics, vmem_limit_bytes=V7X_VMEM_BUDGET_BYTES)


def _log_sigmoid(x):
    return jnp.minimum(x, 0.0) - jnp.log1p(jnp.exp(-jnp.abs(x)))


def _sigmoid(x):
    return 1.0 / (1.0 + jnp.exp(-x))


def _gelu_tanh(x):
    c = 0.7978845608028654
    return 0.5 * x * (1.0 + jnp.tanh(c * (x + 0.044715 * (x * x * x))))


def _expm1(x):
    u = jnp.exp(x)
    um1 = u - 1.0
    safe = jnp.where(um1 == 0.0, 1.0, jnp.log(u))
    return jnp.where(um1 == 0.0, x, jnp.where(um1 == -1.0, -1.0, um1 * x / safe))


def _split3(x):
    hi = x.astype(BF16)
    r1 = x - hi.astype(F32)
    mid = r1.astype(BF16)
    lo = (r1 - mid.astype(F32)).astype(BF16)
    return hi, mid, lo


def _dot01(mask_bf16, x):
    hi, mid, lo = _split3(x)
    acc = jnp.dot(mask_bf16, lo, preferred_element_type=F32)
    acc = acc + jnp.dot(mask_bf16, mid, preferred_element_type=F32)
    return acc + jnp.dot(mask_bf16, hi, preferred_element_type=F32)


def _rmsnorm_body(x_ref, g_ref, o_ref):
    x = x_ref[...]
    inv = lax.rsqrt(jnp.mean(x * x, axis=-1, keepdims=True) + EPS)
    o_ref[...] = ((x * inv) * g_ref[...]).astype(o_ref.dtype)


def rmsnorm(x, g, out_dtype=BF16, rows=256, row0=0, nrows=None):
    d = x.shape[1]
    nrows = x.shape[0] if nrows is None else nrows
    b0 = row0 // rows
    return pl.pallas_call(
        _rmsnorm_body,
        grid=(nrows // rows,),
        in_specs=[pl.BlockSpec((rows, d), lambda i: (b0 + i, 0)), pl.BlockSpec((1, d), lambda i: (0, 0))],
        out_specs=pl.BlockSpec((rows, d), lambda i: (i, 0)),
        out_shape=jax.ShapeDtypeStruct((nrows, d), out_dtype),
        compiler_params=_cparams("parallel"),
        name="rmsnorm",
    )(x, g.reshape(1, d))


def _mm_body(*refs, n_in, has_res, act, layer, tn, kc, nj, ni):
    x_refs = refs[:n_in]
    w_hbm = refs[n_in]
    res_ref = refs[n_in + 1] if has_res else None
    n_io = n_in + 1 + (1 if has_res else 0)
    o_ref, wbf, stage, sem = refs[n_io], refs[n_io + 1], refs[n_io + 2], refs[n_io + 3]
    j, i = pl.program_id(0), pl.program_id(1)

    def chunk_copy(jb, c):
        rows = pl.ds(pl.multiple_of(c * kc, kc), kc)
        cols = pl.ds(pl.multiple_of(jb * tn, tn), tn)
        src = w_hbm.at[rows, cols] if layer is None else w_hbm.at[layer, rows, cols]
        return pltpu.make_async_copy(src, stage, sem)

    @pl.when((j == 0) & (i == 0))
    def _():
        def first(c, carry):
            cp = chunk_copy(0, c)
            cp.start()
            cp.wait()
            wbf[0, pl.ds(pl.multiple_of(c * kc, kc), kc), :] = stage[...].astype(BF16)
            return carry
        lax.fori_loop(0, ni, first, 0)

    has_next = j + 1 < nj

    @pl.when(has_next)
    def _():
        chunk_copy(j + 1, i).start(priority=1)

    slot = j % 2
    k0 = 0
    acc = None
    for x_ref in x_refs:
        kw = x_ref.shape[1]
        part = jnp.dot(x_ref[...], wbf[slot, k0:k0 + kw, :], preferred_element_type=F32)
        acc = part if acc is None else acc + part
        k0 += kw
    if act == "relu2":
        r = jnp.maximum(acc, 0.0)
        acc = r * r
    if has_res:
        acc = res_ref[...] + acc
    o_ref[...] = acc.astype(o_ref.dtype)

    @pl.when(has_next)
    def _():
        chunk_copy(j + 1, i).wait()
        wbf[1 - slot, pl.ds(pl.multiple_of(i * kc, kc), kc), :] = stage[...].astype(BF16)


def matmul(xs, w, *, layer=None, n_out=None, res=None, act=None, out_dtype=F32, tm, tn, name):
    m = xs[0].shape[0]
    kk = sum(x.shape[1] for x in xs)
    n = w.shape[-1] if n_out is None else n_out
    ni = m // tm
    kc = kk // ni
    assert w.shape[-2] == kk and kk % ni == 0 and kc % 16 == 0 and n % tn == 0 and m % tm == 0
    n_in = len(xs)
    in_specs = [pl.BlockSpec((tm, x.shape[1]), lambda j, i: (i, 0)) for x in xs]
    in_specs.append(pl.BlockSpec(memory_space=pl.ANY))
    args = list(xs) + [w]
    if res is not None:
        in_specs.append(pl.BlockSpec((tm, tn), lambda j, i: (i, j)))
        args.append(res)
    return pl.pallas_call(
        functools.partial(_mm_body, n_in=n_in, has_res=res is not None, act=act, layer=layer, tn=tn, kc=kc,
                          nj=n // tn, ni=ni),
        grid=(n // tn, ni),
        in_specs=in_specs,
        out_specs=pl.BlockSpec((tm, tn), lambda j, i: (i, j)),
        out_shape=jax.ShapeDtypeStruct((m, n), out_dtype),
        scratch_shapes=[pltpu.VMEM((2, kk, tn), BF16), pltpu.VMEM((kc, tn), F32), pltpu.SemaphoreType.DMA(())],
        compiler_params=_cparams("arbitrary", "arbitrary"),
        name=name,
    )(*args)


def _mm_acc_body(x_ref, w_ref, res_ref, o_ref):
    k = pl.program_id(2)

    @pl.when(k == 0)
    def _():
        o_ref[...] = res_ref[...] + jnp.dot(x_ref[...], w_ref[...], preferred_element_type=F32)

    @pl.when(k != 0)
    def _():
        o_ref[...] += jnp.dot(x_ref[...], w_ref[...], preferred_element_type=F32)


def matmul_ksplit(x, w, layer, res, *, tm, tn, tk, name):
    m, kk = x.shape
    n = w.shape[-1]
    return pl.pallas_call(
        _mm_acc_body,
        grid=(m // tm, n // tn, kk // tk),
        in_specs=[
            pl.BlockSpec((tm, tk), lambda i, j, k: (i, k)),
            pl.BlockSpec((None, tk, tn), lambda i, j, k: (layer, k, j)),
            pl.BlockSpec((tm, tn), lambda i, j, k: (i, j)),
        ],
        out_specs=pl.BlockSpec((tm, tn), lambda i, j, k: (i, j)),
        out_shape=jax.ShapeDtypeStruct((m, n), F32),
        compiler_params=_cparams("parallel", "parallel", "arbitrary"),
        name=name,
    )(x, w, res)


def _gates_body(u_ref, w_ref, b_ref, gc_ref, gr_ref, *, tm, n_prompt_tiles, seg_sample):
    L = MLSTM_CHUNK
    g = jnp.dot(u_ref[...], w_ref[...], preferred_element_type=F32) + b_ref[...]
    lane = lax.broadcasted_iota(jnp.int32, (tm, V7X_LANES), 1)
    val = jnp.where(lane < 8, g, _log_sigmoid(g))
    row = lax.broadcasted_iota(jnp.int32, (L, L), 0)
    col = lax.broadcasted_iota(jnp.int32, (L, L), 1)
    is_sample = pl.program_id(0) >= n_prompt_tiles
    tril = jnp.where(col <= row, 1.0, 0.0)
    tril_seg = jnp.where((col <= row) & ((row // seg_sample) == (col // seg_sample)), 1.0, 0.0)
    mask = jnp.where(is_sample, tril_seg, tril).astype(BF16)
    lane_l = lax.broadcasted_iota(jnp.int32, (L, V7X_LANES), 1)
    for s in range(tm // L):
        x = val[s * L:(s + 1) * L]
        out = jnp.where(lane_l < 8, x, _dot01(mask, x))
        gr_ref[:, s * L:(s + 1) * L] = out.T[:16]
        for c in range(16):
            gc_ref[c, s * L:(s + 1) * L, :] = jnp.broadcast_to(out[:, c:c + 1], (L, V7X_LANES))


def mlstm_gates(u, w_if, b_if, *, n_prompt_rows, seg_sample, tm=512):
    t, d = u.shape
    return pl.pallas_call(
        functools.partial(_gates_body, tm=tm, n_prompt_tiles=n_prompt_rows // tm, seg_sample=seg_sample),
        grid=(t // tm,),
        in_specs=[
            pl.BlockSpec((tm, d), lambda i: (i, 0)),
            pl.BlockSpec((d, V7X_LANES), lambda i: (0, 0)),
            pl.BlockSpec((1, V7X_LANES), lambda i: (0, 0)),
        ],
        out_specs=[
            pl.BlockSpec((16, tm, V7X_LANES), lambda i: (0, i, 0)),
            pl.BlockSpec((16, tm), lambda i: (0, i)),
        ],
        out_shape=[
            jax.ShapeDtypeStruct((16, t, V7X_LANES), F32),
            jax.ShapeDtypeStruct((16, t), F32),
        ],
        compiler_params=_cparams("parallel"),
        name="mlstm_gates",
    )(u, w_if, b_if)


def _head_out(hh, o, mh):
    hn = hh * lax.rsqrt(jnp.mean(hh * hh, axis=-1, keepdims=True) + EPS)
    return _sigmoid(o) * (hn * mh)


def _mlstm_prompt_body(q_ref, k_ref, v_ref, o_ref, ic_ref, bc_ref, ir_ref, br_ref, mh_ref,
                       ya_ref, c_ref, n_ref, m_ref, c_scr, n_scr, m_scr, *, n_heads, seq, scale):
    L = MLSTM_CHUNK
    c_scr[...] = jnp.zeros_like(c_scr)
    n_scr[...] = jnp.zeros_like(n_scr)
    m_scr[...] = jnp.zeros_like(m_scr)
    row = lax.broadcasted_iota(jnp.int32, (L, L), 0)
    col = lax.broadcasted_iota(jnp.int32, (L, L), 1)
    causal = col <= row

    def chunk(ci, carry):
        r0 = pl.multiple_of(ci * L, L)
        q = q_ref[pl.ds(r0, L), :] * scale
        k = k_ref[pl.ds(r0, L), :]
        v = v_ref[pl.ds(r0, L), :]
        bc = bc_ref[0, pl.ds(r0, L), :]
        ic = ic_ref[0, pl.ds(r0, L), :]
        i_row = ir_ref[0, :, pl.ds(r0, L)]
        b_row = br_ref[0, :, pl.ds(r0, L)]
        m_prev = m_scr[...]
        c_prev = c_scr[...]
        n_prev = n_scr[...]
        dmat = jnp.where(causal, bc - b_row + i_row, -jnp.inf)
        g = bc + m_prev
        mt = jnp.maximum(g, jnp.max(dmat, axis=-1, keepdims=True))
        w_inter = jnp.exp(g - mt)
        qb = q.astype(BF16)
        kb = k.astype(BF16)
        vb = v.astype(BF16)
        sc = lax.dot_general(qb, kb, (((1,), (1,)), ((), ())), preferred_element_type=F32) * jnp.exp(dmat - mt)
        inter = jnp.dot(qb, c_prev.astype(BF16), preferred_element_type=F32)
        num = jnp.concatenate([w_inter, w_inter], axis=1) * inter + jnp.dot(sc.astype(BF16), vb, preferred_element_type=F32)
        qn = jnp.sum(q * n_prev, axis=-1, keepdims=True)
        den = w_inter[:, :1] * qn + jnp.sum(sc, axis=-1, keepdims=True)
        hh = num / jnp.maximum(jnp.abs(den), jnp.exp(-mt[:, :1]))
        ya_ref[pl.ds(r0, L), :] = _head_out(hh, o_ref[pl.ds(r0, L), :], mh_ref[...]).astype(ya_ref.dtype)
        m_new = mt[L - 1:L, :1]
        b_last = b_row[:, L - 1:L]
        decay = jnp.exp(b_last + m_prev - m_new)
        kw = k * jnp.exp(b_last - bc + ic - m_new)
        c_scr[...] = decay * c_prev + lax.dot_general(kw.astype(BF16), vb, (((0,), (0,)), ((), ())), preferred_element_type=F32)
        n_scr[...] = decay * n_prev + jnp.sum(kw, axis=0, keepdims=True)
        m_scr[...] = m_new
        return carry

    lax.fori_loop(0, seq // L, chunk, 0)
    c_ref[0, 0] = c_scr[...]
    n_ref[0, 0] = n_scr[...]
    m_ref[0, 0] = m_scr[...]


def mlstm_prompt(proj, gc, gr, mh_norm, *, batch, seq, n_heads, dk, dv, col_q, col_k, col_v, col_o):
    L = MLSTM_CHUNK
    assert dk == L and dv == 2 * L
    rows = batch * seq
    return pl.pallas_call(
        functools.partial(_mlstm_prompt_body, n_heads=n_heads, seq=seq, scale=dk ** -0.5),
        grid=(batch, n_heads),
        in_specs=[
            pl.BlockSpec((seq, dk), lambda b, h: (b, col_q // dk + h)),
            pl.BlockSpec((seq, dk), lambda b, h: (b, col_k // dk + h)),
            pl.BlockSpec((seq, dv), lambda b, h: (b, col_v // dv + h)),
            pl.BlockSpec((seq, dv), lambda b, h: (b, col_o // dv + h)),
            pl.BlockSpec((1, seq, V7X_LANES), lambda b, h: (h, b, 0)),
            pl.BlockSpec((1, seq, V7X_LANES), lambda b, h: (h + n_heads, b, 0)),
            pl.BlockSpec((1, 1, seq), lambda b, h: (h, 0, b)),
            pl.BlockSpec((1, 1, seq), lambda b, h: (h + n_heads, 0, b)),
            pl.BlockSpec((1, dv), lambda b, h: (0, h)),
        ],
        out_specs=[
            pl.BlockSpec((seq, dv), lambda b, h: (b, h)),
            pl.BlockSpec((1, 1, dk, dv), lambda b, h: (b, h, 0, 0)),
            pl.BlockSpec((1, 1, 1, dk), lambda b, h: (b, h, 0, 0)),
            pl.BlockSpec((1, 1, 1, 1), lambda b, h: (b, h, 0, 0)),
        ],
        out_shape=[
            jax.ShapeDtypeStruct((rows, n_heads * dv), BF16),
            jax.ShapeDtypeStruct((batch, n_heads, dk, dv), F32),
            jax.ShapeDtypeStruct((batch, n_heads, 1, dk), F32),
            jax.ShapeDtypeStruct((batch, n_heads, 1, 1), F32),
        ],
        scratch_shapes=[pltpu.VMEM((dk, dv), F32), pltpu.VMEM((1, dk), F32), pltpu.VMEM((1, 1), F32)],
        compiler_params=_cparams("parallel", "parallel"),
        name="mlstm_prompt",
    )(proj, proj, proj, proj, gc, gc, gr, gr, mh_norm)


def _mlstm_sample_body(q_ref, k_ref, v_ref, o_ref, ic_ref, bc_ref, ir_ref, br_ref, mh_ref, c0_ref, n0_ref, m0_ref,
                       ya_ref, c_ref, n_ref, m_ref, inter_scr, *, n_heads, seg, scale):
    L = MLSTM_CHUNK
    nb = L // seg
    row = lax.broadcasted_iota(jnp.int32, (L, L), 0)
    col = lax.broadcasted_iota(jnp.int32, (L, L), 1)
    same = (row // seg) == (col // seg)
    causal = (col <= row) & same
    last = col == (row // seg) * seg + (seg - 1)
    q = q_ref[...] * scale
    k = k_ref[...]
    v = v_ref[...]
    bc = bc_ref[0]
    ic = ic_ref[0]
    i_row = ir_ref[0]
    b_row = br_ref[0]
    m_prev = m0_ref[0]
    n_prev = n0_ref[0]
    dmat = jnp.where(causal, bc - b_row + i_row, -jnp.inf)
    g = bc + m_prev
    mt = jnp.maximum(g, jnp.max(dmat, axis=-1, keepdims=True))
    w_inter = jnp.exp(g - mt)
    qb = q.astype(BF16)
    kb = k.astype(BF16)
    vb = v.astype(BF16)
    sc = lax.dot_general(qb, kb, (((1,), (1,)), ((), ())), preferred_element_type=F32) * jnp.exp(dmat - mt)

    per_group = V7X_SUBLANES // seg
    sub = lax.broadcasted_iota(jnp.int32, (V7X_SUBLANES, 2 * L), 0)
    for gi in range(L // V7X_SUBLANES):
        q8 = qb[gi * V7X_SUBLANES:(gi + 1) * V7X_SUBLANES]
        acc = jnp.zeros((V7X_SUBLANES, 2 * L), F32)
        for j in range(per_group):
            r = jnp.dot(q8, c0_ref[0, gi * per_group + j, 0].astype(BF16), preferred_element_type=F32)
            acc = jnp.where((sub // seg) == j, r, acc)
        inter_scr[gi * V7X_SUBLANES:(gi + 1) * V7X_SUBLANES, :] = acc
    inter = inter_scr[...]

    num = jnp.concatenate([w_inter, w_inter], axis=1) * inter + jnp.dot(sc.astype(BF16), vb, preferred_element_type=F32)
    qn = jnp.sum(q * n_prev, axis=-1, keepdims=True)
    den = w_inter[:, :1] * qn + jnp.sum(sc, axis=-1, keepdims=True)
    hh = num / jnp.maximum(jnp.abs(den), jnp.exp(-mt[:, :1]))
    ya_ref[...] = _head_out(hh, o_ref[...], mh_ref[...]).astype(ya_ref.dtype)

    b_last = jnp.sum(jnp.where(last, jnp.broadcast_to(b_row, (L, L)), 0.0), axis=-1, keepdims=True)
    mt_row = mt.T
    m_new = jnp.sum(jnp.where(last, mt_row, 0.0), axis=-1, keepdims=True)
    decay = jnp.exp(b_last + m_prev - m_new)
    kw = k * jnp.exp(b_last - bc + ic - m_new)
    seg_ones = jnp.where(same, 1.0, 0.0).astype(BF16)
    n_ref[0] = decay * n_prev + _dot01(seg_ones, kw)
    m_ref[0] = jnp.broadcast_to(m_new, (L, L))
    rowk = lax.broadcasted_iota(jnp.int32, (L, L), 0)
    for bi in range(nb):
        kw_b = jnp.where((rowk // seg) == bi, kw, 0.0).astype(BF16)
        upd = lax.dot_general(kw_b, vb, (((0,), (0,)), ((), ())), preferred_element_type=F32)
        d_b = decay[bi * seg:bi * seg + 1, :1]
        c_ref[0, bi, 0] = d_b * c0_ref[0, bi, 0] + upd


def mlstm_sample(proj, gc, gr, mh_norm, c0, n0_tok, m0_tok, *, row0, n_seq, seg, n_heads, dk, dv,
                 col_q, col_k, col_v, col_o):
    L = MLSTM_CHUNK
    assert dk == L and dv == 2 * L and V7X_SUBLANES % seg == 0 and row0 % L == 0
    rows = n_seq * seg
    nb = L // seg
    t0 = row0 // L
    return pl.pallas_call(
        functools.partial(_mlstm_sample_body, n_heads=n_heads, seg=seg, scale=dk ** -0.5),
        grid=(rows // L, n_heads),
        in_specs=[
            pl.BlockSpec((L, dk), lambda j, h: (t0 + j, col_q // dk + h)),
            pl.BlockSpec((L, dk), lambda j, h: (t0 + j, col_k // dk + h)),
            pl.BlockSpec((L, dv), lambda j, h: (t0 + j, col_v // dv + h)),
            pl.BlockSpec((L, dv), lambda j, h: (t0 + j, col_o // dv + h)),
            pl.BlockSpec((1, L, V7X_LANES), lambda j, h: (h, t0 + j, 0)),
            pl.BlockSpec((1, L, V7X_LANES), lambda j, h: (h + n_heads, t0 + j, 0)),
            pl.BlockSpec((1, 1, L), lambda j, h: (h, 0, t0 + j)),
            pl.BlockSpec((1, 1, L), lambda j, h: (h + n_heads, 0, t0 + j)),
            pl.BlockSpec((1, dv), lambda j, h: (0, h)),
            pl.BlockSpec((1, nb, 1, dk, dv), lambda j, h: (0, j, h, 0, 0)),
            pl.BlockSpec((1, L, dk), lambda j, h: (h, j, 0)),
            pl.BlockSpec((1, L, L), lambda j, h: (h, j, 0)),
        ],
        out_specs=[
            pl.BlockSpec((L, dv), lambda j, h: (j, h)),
            pl.BlockSpec((1, nb, 1, dk, dv), lambda j, h: (0, j, h, 0, 0)),
            pl.BlockSpec((1, L, dk), lambda j, h: (h, j, 0)),
            pl.BlockSpec((1, L, L), lambda j, h: (h, j, 0)),
        ],
        out_shape=[
            jax.ShapeDtypeStruct((rows, n_heads * dv), BF16),
            jax.ShapeDtypeStruct((1, n_seq, n_heads, dk, dv), F32),
            jax.ShapeDtypeStruct((n_heads, rows, dk), F32),
            jax.ShapeDtypeStruct((n_heads, rows, L), F32),
        ],
        scratch_shapes=[pltpu.VMEM((L, dv), F32)],
        compiler_params=_cparams("parallel", "parallel"),
        name="mlstm_sample",
    )(proj, proj, proj, proj, gc, gc, gr, gr, mh_norm, c0, n0_tok, m0_tok)


def _lru_gates(xc, wa_ref, wi_ref, ba, bi, lam):
    nblk = wa_ref.shape[0]
    blk = wa_ref.shape[1]
    rs, igs = [], []
    for n in range(nblk):
        xb = xc[:, n * blk:(n + 1) * blk].astype(BF16)
        rs.append(jnp.dot(xb, wa_ref[n], preferred_element_type=F32))
        igs.append(jnp.dot(xb, wi_ref[n], preferred_element_type=F32))
    r = _sigmoid(jnp.concatenate(rs, axis=1) + ba)
    ig = _sigmoid(jnp.concatenate(igs, axis=1) + bi)
    log_a = LRU_C * r * _log_sigmoid(lam)
    a = jnp.exp(log_a)
    u = jnp.sqrt(-_expm1(2.0 * log_a)) * (ig * xc)
    return a, u


def _lru_prompt_body(x_ref, g_ref, halo_ref, h0_ref, cw_ref, cb_ref, wa_ref, wi_ref, ba_ref, bi_ref, lam_ref,
                     y_ref, hl_ref, h_scr, *, seq):
    Lc = SCAN_CHUNK
    cbw = x_ref.shape[1]
    h_scr[...] = h0_ref[0]
    row = lax.broadcasted_iota(jnp.int32, (Lc, cbw), 0)
    cw = cw_ref[...]
    width = cw.shape[0]

    def chunk(ci, carry):
        r0 = pl.multiple_of(ci * Lc, Lc)
        x = x_ref[pl.ds(r0, Lc), :]
        prev = x_ref[pl.ds(pl.multiple_of(jnp.maximum(r0 - V7X_SUBLANES, 0), V7X_SUBLANES), V7X_SUBLANES), :]
        prev = jnp.where(ci == 0, halo_ref[0], prev)
        xfull = jnp.concatenate([prev, x], axis=0)
        xc = None
        for j in range(width - 1, 0, -1):
            term = pltpu.roll(xfull, j, 0)[V7X_SUBLANES:] * cw[width - 1 - j:width - j]
            xc = term if xc is None else xc + term
        xc = xc + x * cw[width - 1:width] + cb_ref[...]
        a, u = _lru_gates(xc, wa_ref, wi_ref, ba_ref[...], bi_ref[...], lam_ref[...])
        d = 1
        while d < Lc:
            ok = row >= d
            a_sh = jnp.where(ok, pltpu.roll(a, d, 0), 1.0)
            u_sh = jnp.where(ok, pltpu.roll(u, d, 0), 0.0)
            u = a * u_sh + u
            a = a * a_sh
            d *= 2
        hs = u + a * h_scr[...]
        h_scr[...] = hs[Lc - 1:Lc]
        y_ref[pl.ds(r0, Lc), :] = (hs * _gelu_tanh(g_ref[pl.ds(r0, Lc), :])).astype(y_ref.dtype)
        return carry

    lax.fori_loop(0, seq // Lc, chunk, 0)
    hl_ref[0] = h_scr[...]


def lru_prompt(proj, halo, h0, cw, cb, wa, wi, ba, bi, lam, *, batch, seq, width, col_x, col_g, cbw=256):
    blk = wa.shape[1]
    nper = cbw // blk
    vec = lambda b, c: (0, c)
    return pl.pallas_call(
        functools.partial(_lru_prompt_body, seq=seq),
        grid=(batch, width // cbw),
        in_specs=[
            pl.BlockSpec((seq, cbw), lambda b, c: (b, col_x // cbw + c)),
            pl.BlockSpec((seq, cbw), lambda b, c: (b, col_g // cbw + c)),
            pl.BlockSpec((1, V7X_SUBLANES, cbw), lambda b, c: (b, 0, c)),
            pl.BlockSpec((1, 1, cbw), lambda b, c: (b, 0, c)),
            pl.BlockSpec((cw.shape[0], cbw), vec),
            pl.BlockSpec((1, cbw), vec),
            pl.BlockSpec((nper, blk, blk), lambda b, c: (c, 0, 0)),
            pl.BlockSpec((nper, blk, blk), lambda b, c: (c, 0, 0)),
            pl.BlockSpec((1, cbw), vec),
            pl.BlockSpec((1, cbw), vec),
            pl.BlockSpec((1, cbw), vec),
        ],
        out_specs=[
            pl.BlockSpec((seq, cbw), lambda b, c: (b, c)),
            pl.BlockSpec((1, 1, cbw), lambda b, c: (b, 0, c)),
        ],
        out_shape=[
            jax.ShapeDtypeStruct((batch * seq, width), BF16),
            jax.ShapeDtypeStruct((batch, 1, width), F32),
        ],
        scratch_shapes=[pltpu.VMEM((1, cbw), F32)],
        compiler_params=_cparams("parallel", "parallel"),
        name="lru_prompt",
    )(proj, proj, halo, h0, cw, cb, wa, wi, ba, bi, lam)


def _lru_sample_body(x_ref, g_ref, st_ref, h0_ref, cw_ref, cb_ref, wa_ref, wi_ref, ba_ref, bi_ref, lam_ref,
                     y_ref, hl_ref, *, seg):
    cw = cw_ref[...]
    width = cw.shape[0]
    hist = [st_ref[j] for j in range(width - 1)]
    h = h0_ref[...]
    for t in range(seg):
        x = x_ref[t]
        taps = hist + [x]
        xc = taps[0] * cw[0:1]
        for j in range(1, width):
            xc = xc + taps[j] * cw[j:j + 1]
        xc = xc + cb_ref[...]
        a, u = _lru_gates(xc, wa_ref, wi_ref, ba_ref[...], bi_ref[...], lam_ref[...])
        h = a * h + u
        y_ref[t] = (h * _gelu_tanh(g_ref[t])).astype(y_ref.dtype)
        hist = hist[1:] + [x]
    hl_ref[...] = h


def lru_sample(x_t, g_t, st, h0, cw, cb, wa, wi, ba, bi, lam, *, cbw=256):
    seg, n_seq, width = x_t.shape
    blk = wa.shape[1]
    nper = cbw // blk
    vec = lambda c: (0, c)
    return pl.pallas_call(
        functools.partial(_lru_sample_body, seg=seg),
        grid=(width // cbw,),
        in_specs=[
            pl.BlockSpec((seg, n_seq, cbw), lambda c: (0, 0, c)),
            pl.BlockSpec((seg, n_seq, cbw), lambda c: (0, 0, c)),
            pl.BlockSpec((st.shape[0], n_seq, cbw), lambda c: (0, 0, c)),
            pl.BlockSpec((n_seq, cbw), vec),
            pl.BlockSpec((cw.shape[0], cbw), vec),
            pl.BlockSpec((1, cbw), vec),
            pl.BlockSpec((nper, blk, blk), lambda c: (c, 0, 0)),
            pl.BlockSpec((nper, blk, blk), lambda c: (c, 0, 0)),
            pl.BlockSpec((1, cbw), vec),
            pl.BlockSpec((1, cbw), vec),
            pl.BlockSpec((1, cbw), vec),
        ],
        out_specs=[
            pl.BlockSpec((seg, n_seq, cbw), lambda c: (0, 0, c)),
            pl.BlockSpec((n_seq, cbw), vec),
        ],
        out_shape=[
            jax.ShapeDtypeStruct((seg, n_seq, width), BF16),
            jax.ShapeDtypeStruct((n_seq, width), F32),
        ],
        compiler_params=_cparams("parallel"),
        name="lru_sample",
    )(x_t, g_t, st, h0, cw, cb, wa, wi, ba, bi, lam)


def _sconv_prompt_body(bg_ref, cg_ref, hx_ref, halo_ref, cw_ref, y_ref, tail_ref, *, seq):
    Lc = SCAN_CHUNK
    cw = cw_ref[...]
    width = cw.shape[0]

    def chunk(ci, carry):
        r0 = pl.multiple_of(ci * Lc, Lc)
        p = cg_ref[pl.ds(r0, Lc), :] * hx_ref[pl.ds(r0, Lc), :]
        rp = pl.multiple_of(jnp.maximum(r0 - V7X_SUBLANES, 0), V7X_SUBLANES)
        prev = cg_ref[pl.ds(rp, V7X_SUBLANES), :] * hx_ref[pl.ds(rp, V7X_SUBLANES), :]
        prev = jnp.where(ci == 0, halo_ref[0], prev)
        pfull = jnp.concatenate([prev, p], axis=0)
        z = None
        for j in range(width - 1, 0, -1):
            term = pltpu.roll(pfull, j, 0)[V7X_SUBLANES:] * cw[width - 1 - j:width - j]
            z = term if z is None else z + term
        z = z + p * cw[width - 1:width]
        y_ref[pl.ds(r0, Lc), :] = (bg_ref[pl.ds(r0, Lc), :] * z).astype(y_ref.dtype)
        return carry

    lax.fori_loop(0, seq // Lc, chunk, 0)
    rt = seq - V7X_SUBLANES
    tail_ref[0] = cg_ref[pl.ds(rt, V7X_SUBLANES), :] * hx_ref[pl.ds(rt, V7X_SUBLANES), :]


def sconv_prompt(proj, halo, cw, *, batch, seq, width, cbw=512):
    nb = width // cbw
    return pl.pallas_call(
        functools.partial(_sconv_prompt_body, seq=seq),
        grid=(batch, nb),
        in_specs=[
            pl.BlockSpec((seq, cbw), lambda b, c: (b, c)),
            pl.BlockSpec((seq, cbw), lambda b, c: (b, nb + c)),
            pl.BlockSpec((seq, cbw), lambda b, c: (b, 2 * nb + c)),
            pl.BlockSpec((1, V7X_SUBLANES, cbw), lambda b, c: (b, 0, c)),
            pl.BlockSpec((cw.shape[0], cbw), lambda b, c: (0, c)),
        ],
        out_specs=[
            pl.BlockSpec((seq, cbw), lambda b, c: (b, c)),
            pl.BlockSpec((1, V7X_SUBLANES, cbw), lambda b, c: (b, 0, c)),
        ],
        out_shape=[
            jax.ShapeDtypeStruct((batch * seq, width), BF16),
            jax.ShapeDtypeStruct((batch, V7X_SUBLANES, width), F32),
        ],
        compiler_params=_cparams("parallel", "parallel"),
        name="sconv_prompt",
    )(proj, proj, proj, halo, cw)


def _sconv_sample_body(bg_ref, cg_ref, hx_ref, st_ref, cw_ref, y_ref, ps_ref, *, seg):
    cw = cw_ref[...]
    width = cw.shape[0]
    hist = [st_ref[j] for j in range(width - 1)]
    for t in range(seg):
        p = cg_ref[t] * hx_ref[t]
        ps_ref[t] = p
        taps = hist + [p]
        z = taps[0] * cw[0:1]
        for j in range(1, width):
            z = z + taps[j] * cw[j:j + 1]
        y_ref[t] = (bg_ref[t] * z).astype(y_ref.dtype)
        hist = hist[1:] + [p]


def sconv_sample(proj_t, st, cw, *, width, cbw=512):
    seg, n_seq, _ = proj_t.shape
    nb = width // cbw
    return pl.pallas_call(
        functools.partial(_sconv_sample_body, seg=seg),
        grid=(nb,),
        in_specs=[
            pl.BlockSpec((seg, n_seq, cbw), lambda c: (0, 0, c)),
            pl.BlockSpec((seg, n_seq, cbw), lambda c: (0, 0, nb + c)),
            pl.BlockSpec((seg, n_seq, cbw), lambda c: (0, 0, 2 * nb + c)),
            pl.BlockSpec((st.shape[0], n_seq, cbw), lambda c: (0, 0, c)),
            pl.BlockSpec((cw.shape[0], cbw), lambda c: (0, c)),
        ],
        out_specs=[
            pl.BlockSpec((seg, n_seq, cbw), lambda c: (0, 0, c)),
            pl.BlockSpec((seg, n_seq, cbw), lambda c: (0, 0, c)),
        ],
        out_shape=[
            jax.ShapeDtypeStruct((seg, n_seq, width), BF16),
            jax.ShapeDtypeStruct((seg, n_seq, width), F32),
        ],
        compiler_params=_cparams("parallel"),
        name="sconv_sample",
    )(proj_t, proj_t, proj_t, st, cw)


def _softmax_rows(s):
    e = jnp.exp(s - jnp.max(s, axis=-1, keepdims=True))
    return e / jnp.sum(e, axis=-1, keepdims=True)


def _xattn_prompt_body(q_ref, k_ref, v_ref, o_ref, *, scale):
    s = lax.dot_general(q_ref[...], k_ref[...].astype(BF16), (((1,), (1,)), ((), ())), preferred_element_type=F32) * scale
    p = _softmax_rows(s).astype(BF16)
    o_ref[...] = jnp.dot(p, v_ref[...].astype(BF16), preferred_element_type=F32).astype(o_ref.dtype)


def xattn_prompt(q, mk, mv, *, batch, seq, n_mem, n_heads, hd, tq=1024):
    nq = seq // tq
    return pl.pallas_call(
        functools.partial(_xattn_prompt_body, scale=hd ** -0.5),
        grid=(batch, n_heads, nq),
        in_specs=[
            pl.BlockSpec((tq, hd), lambda b, h, i: (b * nq + i, h)),
            pl.BlockSpec((n_mem, hd), lambda b, h, i: (b, h)),
            pl.BlockSpec((n_mem, hd), lambda b, h, i: (b, h)),
        ],
        out_specs=pl.BlockSpec((tq, hd), lambda b, h, i: (b * nq + i, h)),
        out_shape=jax.ShapeDtypeStruct((batch * seq, n_heads * hd), BF16),
        compiler_params=_cparams("parallel", "parallel", "parallel"),
        name="xattn_prompt",
    )(q, mk, mv)


def _xattn_sample_body(q_ref, k_ref, v_ref, o_ref, *, n_heads, scale):
    n_mem, hd = k_ref.shape[2], k_ref.shape[4]
    k2 = k_ref[0, 0].reshape(n_mem * n_heads, hd).astype(BF16)
    v2 = v_ref[0, 0].reshape(n_mem * n_heads, hd).astype(BF16)
    s = lax.dot_general(q_ref[0], k2, (((1,), (1,)), ((), ())), preferred_element_type=F32) * scale
    row = lax.broadcasted_iota(jnp.int32, s.shape, 0)
    col = lax.broadcasted_iota(jnp.int32, s.shape, 1)
    s = jnp.where((row % n_heads) == (col % n_heads), s, -jnp.inf)
    p = _softmax_rows(s).astype(BF16)
    o_ref[0] = jnp.dot(p, v2, preferred_element_type=F32).astype(o_ref.dtype)


def xattn_sample(q3, cache_k, cache_v, layer):
    n_seq, rows, hd = q3.shape
    _, _, n_mem, n_heads, _ = cache_k.shape
    kv_spec = pl.BlockSpec((1, 1, n_mem, n_heads, hd), lambda b: (layer, b, 0, 0, 0))
    return pl.pallas_call(
        functools.partial(_xattn_sample_body, n_heads=n_heads, scale=hd ** -0.5),
        grid=(n_seq,),
        in_specs=[pl.BlockSpec((1, rows, hd), lambda b: (b, 0, 0)), kv_spec, kv_spec],
        out_specs=pl.BlockSpec((1, rows, hd), lambda b: (b, 0, 0)),
        out_shape=jax.ShapeDtypeStruct((n_seq, rows, hd), BF16),
        compiler_params=_cparams("parallel"),
        name="xattn_sample",
    )(q3, cache_k, cache_v)


def kernel(x_prompt, x_sample, mem_prompt, cache_mem_k, cache_mem_v, state_mlstm_c, state_mlstm_n, state_mlstm_m, state_lru_h, state_lru_conv, state_sconv, norm_mix, norm_x, norm_ff, norm_final, w_in_e, b_if, mh_norm, conv_b_w, conv_b_b, lru_wa, lru_ba, lru_wi, lru_bi, lru_lam, w_out_e, w_in_o, conv_c_w, w_out_o, w_xq, w_xk, w_xv, w_xo, w_ff1, w_ff2):
    bp, sp, d = x_prompt.shape
    bs, ss, _ = x_sample.shape
    n_mem = mem_prompt.shape[1]
    depth, _, _, xh, xhd = cache_mem_k.shape
    _, _, ah, dk, dv = state_mlstm_c.shape
    bw = state_lru_h.shape[-1]
    bconv = state_lru_conv.shape[2] + 1
    cconv = state_sconv.shape[2] + 1
    cwid = state_sconv.shape[-1]
    aw = ah * dv
    aqk = ah * dk
    tp = bp * sp
    ts = bs * ss
    t = tp + ts
    TM = 1088
    TN = 1024
    TK2 = 4096
    TM_MEM = 128
    assert t % TM == 0 and tp % 512 == 0 and ts == 512
    w_ff2 = w_ff2.astype(BF16)

    x = jnp.concatenate([x_prompt.reshape(tp, d), x_sample.reshape(ts, d)], axis=0)
    mem_bf = mem_prompt.reshape(bp * n_mem, d).astype(BF16)

    outs = {}
    ie = io = 0
    p_mem_k, p_mem_v = [], []
    for l in range(depth):
        u = rmsnorm(x, norm_mix[l])
        if l % 2 == 0:
            w = w_in_e[ie]
            o_qkvo = 2 * aqk + 2 * aw
            o_if = o_qkvo + 2 * ah
            w_if = jnp.pad(w[:, o_qkvo:o_if], ((0, 0), (0, V7X_LANES - 2 * ah))).astype(BF16)
            bias_if = jnp.pad(b_if[ie], (0, V7X_LANES - 2 * ah)).reshape(1, V7X_LANES)
            col_q, col_k, col_v, col_o = 0, aqk, 2 * aqk, 2 * aqk + aw
            col_x, col_g = 0, bw
            proj = matmul([u], w_in_e, layer=ie, n_out=o_qkvo, tm=TM, tn=TN, name="in_proj_even_qkvo")
            proj_b = matmul([u], w[:, o_if:], tm=TM, tn=TN, name="in_proj_even_lru")
            gc, gr = mlstm_gates(u, w_if, bias_if, n_prompt_rows=tp, seg_sample=ss)
            gr = gr.reshape(16, 1, t)
            mh = mh_norm[ie].reshape(1, aw)
            mkw = dict(n_heads=ah, dk=dk, dv=dv, col_q=col_q, col_k=col_k, col_v=col_v, col_o=col_o)
            ya_p, pc, pn, pm = mlstm_prompt(proj, gc, gr, mh, batch=bp, seq=sp, **mkw)
            n0_tok = jnp.repeat(jnp.transpose(state_mlstm_n[ie], (1, 0, 2)), ss, axis=1)
            m0_tok = jnp.broadcast_to(jnp.repeat(state_mlstm_m[ie].T, ss, axis=1)[:, :, None], (ah, ts, MLSTM_CHUNK))
            ya_s, sc_, sn_tok, sm_tok = mlstm_sample(proj, gc, gr, mh, state_mlstm_c[ie:ie + 1], n0_tok, m0_tok,
                                                     row0=tp, n_seq=bs, seg=ss, **mkw)
            outs.setdefault("pc", []).append(pc.reshape(bp, ah, dk, dv))
            outs.setdefault("pn", []).append(pn.reshape(bp, ah, dk))
            outs.setdefault("pm", []).append(pm.reshape(bp, ah))
            outs.setdefault("sc", []).append(sc_[0])
            outs.setdefault("sn", []).append(jnp.transpose(sn_tok[:, ss - 1::ss, :], (1, 0, 2)))
            outs.setdefault("sm", []).append(sm_tok[:, ss - 1::ss, 0].T)

            cw = conv_b_w[ie]
            cbias = conv_b_b[ie].reshape(1, bw)
            wa = lru_wa[ie].astype(BF16)
            wi = lru_wi[ie].astype(BF16)
            ba = lru_ba[ie].reshape(1, bw)
            bi = lru_bi[ie].reshape(1, bw)
            lam = lru_lam[ie].reshape(1, bw)
            halo_p = jnp.zeros((bp, V7X_SUBLANES, bw), F32)
            h0_p = jnp.zeros((bp, 1, bw), F32)
            yb_p, ph = lru_prompt(proj_b, halo_p, h0_p, cw, cbias, wa, wi, ba, bi, lam,
                                  batch=bp, seq=sp, width=bw, col_x=col_x, col_g=col_g)
            st_s = jnp.transpose(state_lru_conv[ie], (1, 0, 2))
            xb_p = proj_b[:tp, col_x:col_x + bw].reshape(bp, sp, bw)
            xb_s = proj_b[tp:, col_x:col_x + bw].reshape(bs, ss, bw)
            gb_s = proj_b[tp:, col_g:col_g + bw].reshape(bs, ss, bw)
            yb_s, sh = lru_sample(jnp.transpose(xb_s, (1, 0, 2)), jnp.transpose(gb_s, (1, 0, 2)), st_s,
                                  state_lru_h[ie], cw, cbias, wa, wi, ba, bi, lam)
            yb_s = jnp.transpose(yb_s, (1, 0, 2)).reshape(ts, bw)
            outs.setdefault("ph", []).append(ph.reshape(bp, bw))
            outs.setdefault("pcb", []).append(xb_p[:, sp - (bconv - 1):])
            outs.setdefault("sh", []).append(sh)
            outs.setdefault("scb", []).append(jnp.concatenate([state_lru_conv[ie], xb_s], axis=1)[:, ss:])

            ya = jnp.concatenate([ya_p, ya_s], axis=0)
            yb = jnp.concatenate([yb_p, yb_s], axis=0)
            x = matmul([ya, yb], w_out_e, layer=ie, res=x, tm=TM, tn=TN, name="out_proj_even")
            ie += 1
        else:
            proj = matmul([u], w_in_o, layer=io, tm=TM, tn=TN, name="in_proj_odd")
            cw = conv_c_w[io]
            halo_p = jnp.zeros((bp, V7X_SUBLANES, cwid), F32)
            y_p, tail = sconv_prompt(proj, halo_p, cw, batch=bp, seq=sp, width=cwid)
            st_s = jnp.transpose(state_sconv[io], (1, 0, 2))
            proj_t = jnp.transpose(proj[tp:].reshape(bs, ss, 3 * cwid), (1, 0, 2))
            y_s, ps = sconv_sample(proj_t, st_s, cw, width=cwid)
            y_s = jnp.transpose(y_s, (1, 0, 2)).reshape(ts, cwid)
            outs.setdefault("psb", []).append(tail[:, V7X_SUBLANES - (cconv - 1):])
            ps_b = jnp.transpose(ps, (1, 0, 2))
            outs.setdefault("ssb", []).append(jnp.concatenate([state_sconv[io], ps_b], axis=1)[:, ss:])
            y = jnp.concatenate([y_p, y_s], axis=0)
            x = matmul([y], w_out_o, layer=io, res=x, tm=TM, tn=TN, name="out_proj_odd")
            io += 1

        u = rmsnorm(x, norm_x[l])
        q = matmul([u], w_xq, layer=l, out_dtype=BF16, tm=TM, tn=TN, name="xattn_q")
        mk = matmul([mem_bf], w_xk, layer=l, tm=TM_MEM, tn=TN, name="mem_k")
        mv = matmul([mem_bf], w_xv, layer=l, tm=TM_MEM, tn=TN, name="mem_v")
        p_mem_k.append(mk.reshape(bp, n_mem, xh, xhd))
        p_mem_v.append(mv.reshape(bp, n_mem, xh, xhd))
        o_p = xattn_prompt(q, mk, mv, batch=bp, seq=sp, n_mem=n_mem, n_heads=xh, hd=xhd)
        o_s = xattn_sample(q[tp:].reshape(bs, ss * xh, xhd), cache_mem_k, cache_mem_v, l)
        o = jnp.concatenate([o_p, o_s.reshape(ts, d)], axis=0)
        x = matmul([o], w_xo, layer=l, res=x, tm=TM, tn=TN, name="xattn_o")

        u = rmsnorm(x, norm_ff[l])
        hmid = matmul([u], w_ff1, layer=l, act="relu2", out_dtype=BF16, tm=TM, tn=TN, name="ff1")
        x = matmul_ksplit(hmid, w_ff2, l, x, tm=TM, tn=TN, tk=TK2, name="ff2")

    y_p = rmsnorm(x, norm_final, out_dtype=F32, row0=0, nrows=tp)
    y_s = rmsnorm(x, norm_final, out_dtype=F32, row0=tp, nrows=ts)
    dt = x_prompt.dtype
    st = lambda name: jnp.stack(outs[name]).astype(dt)
    return (y_p.reshape(bp, sp, d), y_s.reshape(bs, ss, d),
            st("pc"), st("pn"), st("pm"), st("ph"), st("pcb"), st("psb"),
            jnp.stack(p_mem_k), jnp.stack(p_mem_v),
            st("sc"), st("sn"), st("sm"), st("sh"), st("scb"), st("ssb"))
```

```python
import functools

import jax
import jax.numpy as jnp
from jax import lax
from jax.experimental import pallas as pl
from jax.experimental.pallas import tpu as pltpu

F32 = jnp.float32
BF16 = jnp.bfloat16
EPS = 1e-6
LRU_C = 8.0

V7X_LANES = 128
V7X_SUBLANES = 8
V7X_VMEM_BUDGET_BYTES = 60 * 1024 * 1024

MLSTM_CHUNK = 128
SCAN_CHUNK = 256


def _cparams(*semantics):
    return pltpu.CompilerParams(dimension_semantics=semantics, vmem_limit_bytes=V7X_VMEM_BUDGET_BYTES)


def _log_sigmoid(x):
    return jnp.minimum(x, 0.0) - jnp.log1p(jnp.exp(-jnp.abs(x)))


def _sigmoid(x):
    return 1.0 / (1.0 + jnp.exp(-x))


def _gelu_tanh(x):
    c = 0.7978845608028654
    return 0.5 * x * (1.0 + jnp.tanh(c * (x + 0.044715 * (x * x * x))))


def _expm1(x):
    u = jnp.exp(x)
    um1 = u - 1.0
    safe = jnp.where(um1 == 0.0, 1.0, jnp.log(u))
    return jnp.where(um1 == 0.0, x, jnp.where(um1 == -1.0, -1.0, um1 * x / safe))


def _split3(x):
    hi = x.astype(BF16)
    r1 = x - hi.astype(F32)
    mid = r1.astype(BF16)
    lo = (r1 - mid.astype(F32)).astype(BF16)
    return hi, mid, lo


def _dot01(mask_bf16, x):
    hi, mid, lo = _split3(x)
    acc = jnp.dot(mask_bf16, lo, preferred_element_type=F32)
    acc = acc + jnp.dot(mask_bf16, mid, preferred_element_type=F32)
    return acc + jnp.dot(mask_bf16, hi, preferred_element_type=F32)


def _rmsnorm_body(x_ref, g_ref, o_ref):
    x = x_ref[...]
    inv = lax.rsqrt(jnp.mean(x * x, axis=-1, keepdims=True) + EPS)
    o_ref[...] = ((x * inv) * g_ref[...]).astype(o_ref.dtype)


def rmsnorm(x, g, out_dtype=BF16, rows=256, row0=0, nrows=None):
    d = x.shape[1]
    nrows = x.shape[0] if nrows is None else nrows
    b0 = row0 // rows
    return pl.pallas_call(
        _rmsnorm_body,
        grid=(nrows // rows,),
        in_specs=[pl.BlockSpec((rows, d), lambda i: (b0 + i, 0)), pl.BlockSpec((1, d), lambda i: (0, 0))],
        out_specs=pl.BlockSpec((rows, d), lambda i: (i, 0)),
        out_shape=jax.ShapeDtypeStruct((nrows, d), out_dtype),
        compiler_params=_cparams("parallel"),
        name="rmsnorm",
    )(x, g.reshape(1, d))


def _mm_body(*refs, n_in, has_res, act, kc, nj):
    x_refs = refs[:n_in]
    w_ref = refs[n_in]
    res_ref = refs[n_in + 1] if has_res else None
    n_io = n_in + 1 + (1 if has_res else 0)
    o_ref, wbf = refs[n_io], refs[n_io + 1]
    p, i = pl.program_id(0), pl.program_id(1)

    @pl.when(p > 0)
    def _():
        slot = (p - 1) % 2
        k0 = 0
        acc = None
        for x_ref in x_refs:
            kw = x_ref.shape[1]
            part = jnp.dot(x_ref[...], wbf[slot, k0:k0 + kw, :], preferred_element_type=F32)
            acc = part if acc is None else acc + part
            k0 += kw
        if act == "relu2":
            r = jnp.maximum(acc, 0.0)
            acc = r * r
        if has_res:
            acc = res_ref[...] + acc
        o_ref[...] = acc.astype(o_ref.dtype)

    @pl.when(p < nj)
    def _():
        wbf[p % 2, pl.ds(pl.multiple_of(i * kc, kc), kc), :] = w_ref[...].astype(BF16)


def matmul(xs, w, *, layer=None, n_out=None, res=None, act=None, out_dtype=F32, tm, tn, name):
    m = xs[0].shape[0]
    kk = sum(x.shape[1] for x in xs)
    n = w.shape[-1] if n_out is None else n_out
    ni, nj = m // tm, n // tn
    kc = kk // ni
    assert w.shape[-2] == kk and kk % ni == 0 and kc % 16 == 0 and n % tn == 0 and m % tm == 0
    n_in = len(xs)

    def row(p, i):
        return jnp.where(p > 0, i, 0)

    def w_idx(p, i):
        blk = (jnp.where(p < nj, i, ni - 1), jnp.minimum(p, nj - 1))
        return blk if layer is None else (layer,) + blk

    in_specs = [pl.BlockSpec((tm, x.shape[1]), lambda p, i: (row(p, i), 0)) for x in xs]
    in_specs.append(pl.BlockSpec((kc, tn) if layer is None else (None, kc, tn), w_idx))
    args = list(xs) + [w]
    out_idx = lambda p, i: (row(p, i), jnp.maximum(p - 1, 0))
    if res is not None:
        in_specs.append(pl.BlockSpec((tm, tn), out_idx))
        args.append(res)
    return pl.pallas_call(
        functools.partial(_mm_body, n_in=n_in, has_res=res is not None, act=act, kc=kc, nj=nj),
        grid=(nj + 1, ni),
        in_specs=in_specs,
        out_specs=pl.BlockSpec((tm, tn), out_idx),
        out_shape=jax.ShapeDtypeStruct((m, n), out_dtype),
        scratch_shapes=[pltpu.VMEM((2, kk, tn), BF16)],
        compiler_params=_cparams("arbitrary", "arbitrary"),
        name=name,
    )(*args)


def _mm_acc_body(x_ref, w_ref, res_ref, o_ref):
    k = pl.program_id(2)

    @pl.when(k == 0)
    def _():
        o_ref[...] = res_ref[...] + jnp.dot(x_ref[...], w_ref[...], preferred_element_type=F32)

    @pl.when(k != 0)
    def _():
        o_ref[...] += jnp.dot(x_ref[...], w_ref[...], preferred_element_type=F32)


def matmul_ksplit(x, w, layer, res, *, tm, tn, tk, name):
    m, kk = x.shape
    n = w.shape[-1]
    return pl.pallas_call(
        _mm_acc_body,
        grid=(m // tm, n // tn, kk // tk),
        in_specs=[
            pl.BlockSpec((tm, tk), lambda i, j, k: (i, k)),
            pl.BlockSpec((None, tk, tn), lambda i, j, k: (layer, k, j)),
            pl.BlockSpec((tm, tn), lambda i, j, k: (i, j)),
        ],
        out_specs=pl.BlockSpec((tm, tn), lambda i, j, k: (i, j)),
        out_shape=jax.ShapeDtypeStruct((m, n), F32),
        compiler_params=_cparams("parallel", "parallel", "arbitrary"),
        name=name,
    )(x, w, res)


def _gates_body(u_ref, w_ref, b_ref, gc_ref, gr_ref, *, tm, n_prompt_tiles, seg_sample):
    L = MLSTM_CHUNK
    g = jnp.dot(u_ref[...], w_ref[...], preferred_element_type=F32) + b_ref[...]
    lane = lax.broadcasted_iota(jnp.int32, (tm, V7X_LANES), 1)
    val = jnp.where(lane < 8, g, _log_sigmoid(g))
    row = lax.broadcasted_iota(jnp.int32, (L, L), 0)
    col = lax.broadcasted_iota(jnp.int32, (L, L), 1)
    is_sample = pl.program_id(0) >= n_prompt_tiles
    tril = jnp.where(col <= row, 1.0, 0.0)
    tril_seg = jnp.where((col <= row) & ((row // seg_sample) == (col // seg_sample)), 1.0, 0.0)
    mask = jnp.where(is_sample, tril_seg, tril).astype(BF16)
    lane_l = lax.broadcasted_iota(jnp.int32, (L, V7X_LANES), 1)
    for s in range(tm // L):
        x = val[s * L:(s + 1) * L]
        out = jnp.where(lane_l < 8, x, _dot01(mask, x))
        gr_ref[:, s * L:(s + 1) * L] = out.T[:16]
        for c in range(16):
            gc_ref[c, s * L:(s + 1) * L, :] = jnp.broadcast_to(out[:, c:c + 1], (L, V7X_LANES))


def mlstm_gates(u, w_if, b_if, *, n_prompt_rows, seg_sample, tm=512):
    t, d = u.shape
    return pl.pallas_call(
        functools.partial(_gates_body, tm=tm, n_prompt_tiles=n_prompt_rows // tm, seg_sample=seg_sample),
        grid=(t // tm,),
        in_specs=[
            pl.BlockSpec((tm, d), lambda i: (i, 0)),
            pl.BlockSpec((d, V7X_LANES), lambda i: (0, 0)),
            pl.BlockSpec((1, V7X_LANES), lambda i: (0, 0)),
        ],
        out_specs=[
            pl.BlockSpec((16, tm, V7X_LANES), lambda i: (0, i, 0)),
            pl.BlockSpec((16, tm), lambda i: (0, i)),
        ],
        out_shape=[
            jax.ShapeDtypeStruct((16, t, V7X_LANES), F32),
            jax.ShapeDtypeStruct((16, t), F32),
        ],
        compiler_params=_cparams("parallel"),
        name="mlstm_gates",
    )(u, w_if, b_if)


def _head_out(hh, o, mh):
    hn = hh * lax.rsqrt(jnp.mean(hh * hh, axis=-1, keepdims=True) + EPS)
    return _sigmoid(o) * (hn * mh)


def _mlstm_prompt_body(q_ref, k_ref, v_ref, o_ref, ic_ref, bc_ref, ir_ref, br_ref, mh_ref,
                       ya_ref, c_ref, n_ref, m_ref, c_scr, n_scr, m_scr, *, n_heads, seq, scale):
    L = MLSTM_CHUNK
    c_scr[...] = jnp.zeros_like(c_scr)
    n_scr[...] = jnp.zeros_like(n_scr)
    m_scr[...] = jnp.zeros_like(m_scr)
    row = lax.broadcasted_iota(jnp.int32, (L, L), 0)
    col = lax.broadcasted_iota(jnp.int32, (L, L), 1)
    causal = col <= row

    def chunk(ci, carry):
        r0 = pl.multiple_of(ci * L, L)
        q = q_ref[pl.ds(r0, L), :] * scale
        k = k_ref[pl.ds(r0, L), :]
        v = v_ref[pl.ds(r0, L), :]
        bc = bc_ref[0, pl.ds(r0, L), :]
        ic = ic_ref[0, pl.ds(r0, L), :]
        i_row = ir_ref[0, :, pl.ds(r0, L)]
        b_row = br_ref[0, :, pl.ds(r0, L)]
        m_prev = m_scr[...]
        c_prev = c_scr[...]
        n_prev = n_scr[...]
        dmat = jnp.where(causal, bc - b_row + i_row, -jnp.inf)
        g = bc + m_prev
        mt = jnp.maximum(g, jnp.max(dmat, axis=-1, keepdims=True))
        w_inter = jnp.exp(g - mt)
        qb = q.astype(BF16)
        kb = k.astype(BF16)
        vb = v.astype(BF16)
        sc = lax.dot_general(qb, kb, (((1,), (1,)), ((), ())), preferred_element_type=F32) * jnp.exp(dmat - mt)
        inter = jnp.dot(qb, c_prev.astype(BF16), preferred_element_type=F32)
        num = jnp.concatenate([w_inter, w_inter], axis=1) * inter + jnp.dot(sc.astype(BF16), vb, preferred_element_type=F32)
        qn = jnp.sum(q * n_prev, axis=-1, keepdims=True)
        den = w_inter[:, :1] * qn + jnp.sum(sc, axis=-1, keepdims=True)
        hh = num / jnp.maximum(jnp.abs(den), jnp.exp(-mt[:, :1]))
        ya_ref[pl.ds(r0, L), :] = _head_out(hh, o_ref[pl.ds(r0, L), :], mh_ref[...]).astype(ya_ref.dtype)
        m_new = mt[L - 1:L, :1]
        b_last = b_row[:, L - 1:L]
        decay = jnp.exp(b_last + m_prev - m_new)
        kw = k * jnp.exp(b_last - bc + ic - m_new)
        c_scr[...] = decay * c_prev + lax.dot_general(kw.astype(BF16), vb, (((0,), (0,)), ((), ())), preferred_element_type=F32)
        n_scr[...] = decay * n_prev + jnp.sum(kw, axis=0, keepdims=True)
        m_scr[...] = m_new
        return carry

    lax.fori_loop(0, seq // L, chunk, 0)
    c_ref[0, 0] = c_scr[...]
    n_ref[0, 0] = n_scr[...]
    m_ref[0, 0] = m_scr[...]


def mlstm_prompt(proj, gc, gr, mh_norm, *, batch, seq, n_heads, dk, dv, col_q, col_k, col_v, col_o):
    L = MLSTM_CHUNK
    assert dk == L and dv == 2 * L
    rows = batch * seq
    return pl.pallas_call(
        functools.partial(_mlstm_prompt_body, n_heads=n_heads, seq=seq, scale=dk ** -0.5),
        grid=(batch, n_heads),
        in_specs=[
            pl.BlockSpec((seq, dk), lambda b, h: (b, col_q // dk + h)),
            pl.BlockSpec((seq, dk), lambda b, h: (b, col_k // dk + h)),
            pl.BlockSpec((seq, dv), lambda b, h: (b, col_v // dv + h)),
            pl.BlockSpec((seq, dv), lambda b, h: (b, col_o // dv + h)),
            pl.BlockSpec((1, seq, V7X_LANES), lambda b, h: (h, b, 0)),
            pl.BlockSpec((1, seq, V7X_LANES), lambda b, h: (h + n_heads, b, 0)),
            pl.BlockSpec((1, 1, seq), lambda b, h: (h, 0, b)),
            pl.BlockSpec((1, 1, seq), lambda b, h: (h + n_heads, 0, b)),
            pl.BlockSpec((1, dv), lambda b, h: (0, h)),
        ],
        out_specs=[
            pl.BlockSpec((seq, dv), lambda b, h: (b, h)),
            pl.BlockSpec((1, 1, dk, dv), lambda b, h: (b, h, 0, 0)),
            pl.BlockSpec((1, 1, 1, dk), lambda b, h: (b, h, 0, 0)),
            pl.BlockSpec((1, 1, 1, 1), lambda b, h: (b, h, 0, 0)),
        ],
        out_shape=[
            jax.ShapeDtypeStruct((rows, n_heads * dv), BF16),
            jax.ShapeDtypeStruct((batch, n_heads, dk, dv), F32),
            jax.ShapeDtypeStruct((batch, n_heads, 1, dk), F32),
            jax.ShapeDtypeStruct((batch, n_heads, 1, 1), F32),
        ],
        scratch_shapes=[pltpu.VMEM((dk, dv), F32), pltpu.VMEM((1, dk), F32), pltpu.VMEM((1, 1), F32)],
        compiler_params=_cparams("parallel", "parallel"),
        name="mlstm_prompt",
    )(proj, proj, proj, proj, gc, gc, gr, gr, mh_norm)


def _mlstm_sample_body(q_ref, k_ref, v_ref, o_ref, ic_ref, bc_ref, ir_ref, br_ref, mh_ref, c0_ref, n0_ref, m0_ref,
                       ya_ref, c_ref, n_ref, m_ref, inter_scr, *, n_heads, seg, scale):
    L = MLSTM_CHUNK
    nb = L // seg
    row = lax.broadcasted_iota(jnp.int32, (L, L), 0)
    col = lax.broadcasted_iota(jnp.int32, (L, L), 1)
    same = (row // seg) == (col // seg)
    causal = (col <= row) & same
    last = col == (row // seg) * seg + (seg - 1)
    q = q_ref[...] * scale
    k = k_ref[...]
    v = v_ref[...]
    bc = bc_ref[0]
    ic = ic_ref[0]
    i_row = ir_ref[0]
    b_row = br_ref[0]
    m_prev = m0_ref[0]
    n_prev = n0_ref[0]
    dmat = jnp.where(causal, bc - b_row + i_row, -jnp.inf)
    g = bc + m_prev
    mt = jnp.maximum(g, jnp.max(dmat, axis=-1, keepdims=True))
    w_inter = jnp.exp(g - mt)
    qb = q.astype(BF16)
    kb = k.astype(BF16)
    vb = v.astype(BF16)
    sc = lax.dot_general(qb, kb, (((1,), (1,)), ((), ())), preferred_element_type=F32) * jnp.exp(dmat - mt)

    per_group = V7X_SUBLANES // seg
    sub = lax.broadcasted_iota(jnp.int32, (V7X_SUBLANES, 2 * L), 0)
    for gi in range(L // V7X_SUBLANES):
        q8 = qb[gi * V7X_SUBLANES:(gi + 1) * V7X_SUBLANES]
        acc = jnp.zeros((V7X_SUBLANES, 2 * L), F32)
        for j in range(per_group):
            r = jnp.dot(q8, c0_ref[0, gi * per_group + j, 0].astype(BF16), preferred_element_type=F32)
            acc = jnp.where((sub // seg) == j, r, acc)
        inter_scr[gi * V7X_SUBLANES:(gi + 1) * V7X_SUBLANES, :] = acc
    inter = inter_scr[...]

    num = jnp.concatenate([w_inter, w_inter], axis=1) * inter + jnp.dot(sc.astype(BF16), vb, preferred_element_type=F32)
    qn = jnp.sum(q * n_prev, axis=-1, keepdims=True)
    den = w_inter[:, :1] * qn + jnp.sum(sc, axis=-1, keepdims=True)
    hh = num / jnp.maximum(jnp.abs(den), jnp.exp(-mt[:, :1]))
    ya_ref[...] = _head_out(hh, o_ref[...], mh_ref[...]).astype(ya_ref.dtype)

    b_last = jnp.sum(jnp.where(last, jnp.broadcast_to(b_row, (L, L)), 0.0), axis=-1, keepdims=True)
    mt_row = mt.T
    m_new = jnp.sum(jnp.where(last, mt_row, 0.0), axis=-1, keepdims=True)
    decay = jnp.exp(b_last + m_prev - m_new)
    kw = k * jnp.exp(b_last - bc + ic - m_new)
    seg_ones = jnp.where(same, 1.0, 0.0).astype(BF16)
    n_ref[0] = decay * n_prev + _dot01(seg_ones, kw)
    m_ref[0] = jnp.broadcast_to(m_new, (L, L))
    rowk = lax.broadcasted_iota(jnp.int32, (L, L), 0)
    for bi in range(nb):
        kw_b = jnp.where((rowk // seg) == bi, kw, 0.0).astype(BF16)
        upd = lax.dot_general(kw_b, vb, (((0,), (0,)), ((), ())), preferred_element_type=F32)
        d_b = decay[bi * seg:bi * seg + 1, :1]
        c_ref[0, bi, 0] = d_b * c0_ref[0, bi, 0] + upd


def mlstm_sample(proj, gc, gr, mh_norm, c0, n0_tok, m0_tok, *, row0, n_seq, seg, n_heads, dk, dv,
                 col_q, col_k, col_v, col_o):
    L = MLSTM_CHUNK
    assert dk == L and dv == 2 * L and V7X_SUBLANES % seg == 0 and row0 % L == 0
    rows = n_seq * seg
    nb = L // seg
    t0 = row0 // L
    return pl.pallas_call(
        functools.partial(_mlstm_sample_body, n_heads=n_heads, seg=seg, scale=dk ** -0.5),
        grid=(rows // L, n_heads),
        in_specs=[
            pl.BlockSpec((L, dk), lambda j, h: (t0 + j, col_q // dk + h)),
            pl.BlockSpec((L, dk), lambda j, h: (t0 + j, col_k // dk + h)),
            pl.BlockSpec((L, dv), lambda j, h: (t0 + j, col_v // dv + h)),
            pl.BlockSpec((L, dv), lambda j, h: (t0 + j, col_o // dv + h)),
            pl.BlockSpec((1, L, V7X_LANES), lambda j, h: (h, t0 + j, 0)),
            pl.BlockSpec((1, L, V7X_LANES), lambda j, h: (h + n_heads, t0 + j, 0)),
            pl.BlockSpec((1, 1, L), lambda j, h: (h, 0, t0 + j)),
            pl.BlockSpec((1, 1, L), lambda j, h: (h + n_heads, 0, t0 + j)),
            pl.BlockSpec((1, dv), lambda j, h: (0, h)),
            pl.BlockSpec((1, nb, 1, dk, dv), lambda j, h: (0, j, h, 0, 0)),
            pl.BlockSpec((1, L, dk), lambda j, h: (h, j, 0)),
            pl.BlockSpec((1, L, L), lambda j, h: (h, j, 0)),
        ],
        out_specs=[
            pl.BlockSpec((L, dv), lambda j, h: (j, h)),
            pl.BlockSpec((1, nb, 1, dk, dv), lambda j, h: (0, j, h, 0, 0)),
            pl.BlockSpec((1, L, dk), lambda j, h: (h, j, 0)),
            pl.BlockSpec((1, L, L), lambda j, h: (h, j, 0)),
        ],
        out_shape=[
            jax.ShapeDtypeStruct((rows, n_heads * dv), BF16),
            jax.ShapeDtypeStruct((1, n_seq, n_heads, dk, dv), F32),
            jax.ShapeDtypeStruct((n_heads, rows, dk), F32),
            jax.ShapeDtypeStruct((n_heads, rows, L), F32),
        ],
        scratch_shapes=[pltpu.VMEM((L, dv), F32)],
        compiler_params=_cparams("parallel", "parallel"),
        name="mlstm_sample",
    )(proj, proj, proj, proj, gc, gc, gr, gr, mh_norm, c0, n0_tok, m0_tok)


def _lru_gates(xc, wa_ref, wi_ref, ba, bi, lam):
    nblk = wa_ref.shape[0]
    blk = wa_ref.shape[1]
    rs, igs = [], []
    for n in range(nblk):
        xb = xc[:, n * blk:(n + 1) * blk].astype(BF16)
        rs.append(jnp.dot(xb, wa_ref[n], preferred_element_type=F32))
        igs.append(jnp.dot(xb, wi_ref[n], preferred_element_type=F32))
    r = _sigmoid(jnp.concatenate(rs, axis=1) + ba)
    ig = _sigmoid(jnp.concatenate(igs, axis=1) + bi)
    log_a = LRU_C * r * _log_sigmoid(lam)
    a = jnp.exp(log_a)
    u = jnp.sqrt(-_expm1(2.0 * log_a)) * (ig * xc)
    return a, u


def _lru_prompt_body(x_ref, g_ref, halo_ref, h0_ref, cw_ref, cb_ref, wa_ref, wi_ref, ba_ref, bi_ref, lam_ref,
                     y_ref, hl_ref, h_scr, *, seq):
    Lc = SCAN_CHUNK
    cbw = x_ref.shape[1]
    h_scr[...] = h0_ref[0]
    row = lax.broadcasted_iota(jnp.int32, (Lc, cbw), 0)
    cw = cw_ref[...]
    width = cw.shape[0]

    def chunk(ci, carry):
        r0 = pl.multiple_of(ci * Lc, Lc)
        x = x_ref[pl.ds(r0, Lc), :]
        prev = x_ref[pl.ds(pl.multiple_of(jnp.maximum(r0 - V7X_SUBLANES, 0), V7X_SUBLANES), V7X_SUBLANES), :]
        prev = jnp.where(ci == 0, halo_ref[0], prev)
        xfull = jnp.concatenate([prev, x], axis=0)
        xc = None
        for j in range(width - 1, 0, -1):
            term = pltpu.roll(xfull, j, 0)[V7X_SUBLANES:] * cw[width - 1 - j:width - j]
            xc = term if xc is None else xc + term
        xc = xc + x * cw[width - 1:width] + cb_ref[...]
        a, u = _lru_gates(xc, wa_ref, wi_ref, ba_ref[...], bi_ref[...], lam_ref[...])
        d = 1
        while d < Lc:
            ok = row >= d
            a_sh = jnp.where(ok, pltpu.roll(a, d, 0), 1.0)
            u_sh = jnp.where(ok, pltpu.roll(u, d, 0), 0.0)
            u = a * u_sh + u
            a = a * a_sh
            d *= 2
        hs = u + a * h_scr[...]
        h_scr[...] = hs[Lc - 1:Lc]
        y_ref[pl.ds(r0, Lc), :] = (hs * _gelu_tanh(g_ref[pl.ds(r0, Lc), :])).astype(y_ref.dtype)
        return carry

    lax.fori_loop(0, seq // Lc, chunk, 0)
    hl_ref[0] = h_scr[...]


def lru_prompt(proj, halo, h0, cw, cb, wa, wi, ba, bi, lam, *, batch, seq, width, col_x, col_g, cbw=256):
    blk = wa.shape[1]
    nper = cbw // blk
    vec = lambda b, c: (0, c)
    return pl.pallas_call(
        functools.partial(_lru_prompt_body, seq=seq),
        grid=(batch, width // cbw),
        in_specs=[
            pl.BlockSpec((seq, cbw), lambda b, c: (b, col_x // cbw + c)),
            pl.BlockSpec((seq, cbw), lambda b, c: (b, col_g // cbw + c)),
            pl.BlockSpec((1, V7X_SUBLANES, cbw), lambda b, c: (b, 0, c)),
            pl.BlockSpec((1, 1, cbw), lambda b, c: (b, 0, c)),
            pl.BlockSpec((cw.shape[0], cbw), vec),
            pl.BlockSpec((1, cbw), vec),
            pl.BlockSpec((nper, blk, blk), lambda b, c: (c, 0, 0)),
            pl.BlockSpec((nper, blk, blk), lambda b, c: (c, 0, 0)),
            pl.BlockSpec((1, cbw), vec),
            pl.BlockSpec((1, cbw), vec),
            pl.BlockSpec((1, cbw), vec),
        ],
        out_specs=[
            pl.BlockSpec((seq, cbw), lambda b, c: (b, c)),
            pl.BlockSpec((1, 1, cbw), lambda b, c: (b, 0, c)),
        ],
        out_shape=[
            jax.ShapeDtypeStruct((batch * seq, width), BF16),
            jax.ShapeDtypeStruct((batch, 1, width), F32),
        ],
        scratch_shapes=[pltpu.VMEM((1, cbw), F32)],
        compiler_params=_cparams("parallel", "parallel"),
        name="lru_prompt",
    )(proj, proj, halo, h0, cw, cb, wa, wi, ba, bi, lam)


def _lru_sample_body(x_ref, g_ref, st_ref, h0_ref, cw_ref, cb_ref, wa_ref, wi_ref, ba_ref, bi_ref, lam_ref,
                     y_ref, hl_ref, *, seg):
    cw = cw_ref[...]
    width = cw.shape[0]
    hist = [st_ref[j] for j in range(width - 1)]
    h = h0_ref[...]
    for t in range(seg):
        x = x_ref[t]
        taps = hist + [x]
        xc = taps[0] * cw[0:1]
        for j in range(1, width):
            xc = xc + taps[j] * cw[j:j + 1]
        xc = xc + cb_ref[...]
        a, u = _lru_gates(xc, wa_ref, wi_ref, ba_ref[...], bi_ref[...], lam_ref[...])
        h = a * h + u
        y_ref[t] = (h * _gelu_tanh(g_ref[t])).astype(y_ref.dtype)
        hist = hist[1:] + [x]
    hl_ref[...] = h


def lru_sample(x_t, g_t, st, h0, cw, cb, wa, wi, ba, bi, lam, *, cbw=256):
    seg, n_seq, width = x_t.shape
    blk = wa.shape[1]
    nper = cbw // blk
    vec = lambda c: (0, c)
    return pl.pallas_call(
        functools.partial(_lru_sample_body, seg=seg),
        grid=(width // cbw,),
        in_specs=[
            pl.BlockSpec((seg, n_seq, cbw), lambda c: (0, 0, c)),
            pl.BlockSpec((seg, n_seq, cbw), lambda c: (0, 0, c)),
            pl.BlockSpec((st.shape[0], n_seq, cbw), lambda c: (0, 0, c)),
            pl.BlockSpec((n_seq, cbw), vec),
            pl.BlockSpec((cw.shape[0], cbw), vec),
            pl.BlockSpec((1, cbw), vec),
            pl.BlockSpec((nper, blk, blk), lambda c: (c, 0, 0)),
            pl.BlockSpec((nper, blk, blk), lambda c: (c, 0, 0)),
            pl.BlockSpec((1, cbw), vec),
            pl.BlockSpec((1, cbw), vec),
            pl.BlockSpec((1, cbw), vec),
        ],
        out_specs=[
            pl.BlockSpec((seg, n_seq, cbw), lambda c: (0, 0, c)),
            pl.BlockSpec((n_seq, cbw), vec),
        ],
        out_shape=[
            jax.ShapeDtypeStruct((seg, n_seq, width), BF16),
            jax.ShapeDtypeStruct((n_seq, width), F32),
        ],
        compiler_params=_cparams("parallel"),
        name="lru_sample",
    )(x_t, g_t, st, h0, cw, cb, wa, wi, ba, bi, lam)


def _sconv_prompt_body(bg_ref, cg_ref, hx_ref, halo_ref, cw_ref, y_ref, tail_ref, *, seq):
    Lc = SCAN_CHUNK
    cw = cw_ref[...]
    width = cw.shape[0]

    def chunk(ci, carry):
        r0 = pl.multiple_of(ci * Lc, Lc)
        p = cg_ref[pl.ds(r0, Lc), :] * hx_ref[pl.ds(r0, Lc), :]
        rp = pl.multiple_of(jnp.maximum(r0 - V7X_SUBLANES, 0), V7X_SUBLANES)
        prev = cg_ref[pl.ds(rp, V7X_SUBLANES), :] * hx_ref[pl.ds(rp, V7X_SUBLANES), :]
        prev = jnp.where(ci == 0, halo_ref[0], prev)
        pfull = jnp.concatenate([prev, p], axis=0)
        z = None
        for j in range(width - 1, 0, -1):
            term = pltpu.roll(pfull, j, 0)[V7X_SUBLANES:] * cw[width - 1 - j:width - j]
            z = term if z is None else z + term
        z = z + p * cw[width - 1:width]
        y_ref[pl.ds(r0, Lc), :] = (bg_ref[pl.ds(r0, Lc), :] * z).astype(y_ref.dtype)
        return carry

    lax.fori_loop(0, seq // Lc, chunk, 0)
    rt = seq - V7X_SUBLANES
    tail_ref[0] = cg_ref[pl.ds(rt, V7X_SUBLANES), :] * hx_ref[pl.ds(rt, V7X_SUBLANES), :]


def sconv_prompt(proj, halo, cw, *, batch, seq, width, cbw=512):
    nb = width // cbw
    return pl.pallas_call(
        functools.partial(_sconv_prompt_body, seq=seq),
        grid=(batch, nb),
        in_specs=[
            pl.BlockSpec((seq, cbw), lambda b, c: (b, c)),
            pl.BlockSpec((seq, cbw), lambda b, c: (b, nb + c)),
            pl.BlockSpec((seq, cbw), lambda b, c: (b, 2 * nb + c)),
            pl.BlockSpec((1, V7X_SUBLANES, cbw), lambda b, c: (b, 0, c)),
            pl.BlockSpec((cw.shape[0], cbw), lambda b, c: (0, c)),
        ],
        out_specs=[
            pl.BlockSpec((seq, cbw), lambda b, c: (b, c)),
            pl.BlockSpec((1, V7X_SUBLANES, cbw), lambda b, c: (b, 0, c)),
        ],
        out_shape=[
            jax.ShapeDtypeStruct((batch * seq, width), BF16),
            jax.ShapeDtypeStruct((batch, V7X_SUBLANES, width), F32),
        ],
        compiler_params=_cparams("parallel", "parallel"),
        name="sconv_prompt",
    )(proj, proj, proj, halo, cw)


def _sconv_sample_body(bg_ref, cg_ref, hx_ref, st_ref, cw_ref, y_ref, ps_ref, *, seg):
    cw = cw_ref[...]
    width = cw.shape[0]
    hist = [st_ref[j] for j in range(width - 1)]
    for t in range(seg):
        p = cg_ref[t] * hx_ref[t]
        ps_ref[t] = p
        taps = hist + [p]
        z = taps[0] * cw[0:1]
        for j in range(1, width):
            z = z + taps[j] * cw[j:j + 1]
        y_ref[t] = (bg_ref[t] * z).astype(y_ref.dtype)
        hist = hist[1:] + [p]


def sconv_sample(proj_t, st, cw, *, width, cbw=512):
    seg, n_seq, _ = proj_t.shape
    nb = width // cbw
    return pl.pallas_call(
        functools.partial(_sconv_sample_body, seg=seg),
        grid=(nb,),
        in_specs=[
            pl.BlockSpec((seg, n_seq, cbw), lambda c: (0, 0, c)),
            pl.BlockSpec((seg, n_seq, cbw), lambda c: (0, 0, nb + c)),
            pl.BlockSpec((seg, n_seq, cbw), lambda c: (0, 0, 2 * nb + c)),
            pl.BlockSpec((st.shape[0], n_seq, cbw), lambda c: (0, 0, c)),
            pl.BlockSpec((cw.shape[0], cbw), lambda c: (0, c)),
        ],
        out_specs=[
            pl.BlockSpec((seg, n_seq, cbw), lambda c: (0, 0, c)),
            pl.BlockSpec((seg, n_seq, cbw), lambda c: (0, 0, c)),
        ],
        out_shape=[
            jax.ShapeDtypeStruct((seg, n_seq, width), BF16),
            jax.ShapeDtypeStruct((seg, n_seq, width), F32),
        ],
        compiler_params=_cparams("parallel"),
        name="sconv_sample",
    )(proj_t, proj_t, proj_t, st, cw)


def _softmax_rows(s):
    e = jnp.exp(s - jnp.max(s, axis=-1, keepdims=True))
    return e / jnp.sum(e, axis=-1, keepdims=True)


def _xattn_prompt_body(q_ref, k_ref, v_ref, o_ref, *, scale):
    s = lax.dot_general(q_ref[...], k_ref[...].astype(BF16), (((1,), (1,)), ((), ())), preferred_element_type=F32) * scale
    p = _softmax_rows(s).astype(BF16)
    o_ref[...] = jnp.dot(p, v_ref[...].astype(BF16), preferred_element_type=F32).astype(o_ref.dtype)


def xattn_prompt(q, mk, mv, *, batch, seq, n_mem, n_heads, hd, tq=1024):
    nq = seq // tq
    return pl.pallas_call(
        functools.partial(_xattn_prompt_body, scale=hd ** -0.5),
        grid=(batch, n_heads, nq),
        in_specs=[
            pl.BlockSpec((tq, hd), lambda b, h, i: (b * nq + i, h)),
            pl.BlockSpec((n_mem, hd), lambda b, h, i: (b, h)),
            pl.BlockSpec((n_mem, hd), lambda b, h, i: (b, h)),
        ],
        out_specs=pl.BlockSpec((tq, hd), lambda b, h, i: (b * nq + i, h)),
        out_shape=jax.ShapeDtypeStruct((batch * seq, n_heads * hd), BF16),
        compiler_params=_cparams("parallel", "parallel", "parallel"),
        name="xattn_prompt",
    )(q, mk, mv)


def _xattn_sample_body(q_ref, k_ref, v_ref, o_ref, *, n_heads, scale):
    n_mem, hd = k_ref.shape[2], k_ref.shape[4]
    k2 = k_ref[0, 0].reshape(n_mem * n_heads, hd).astype(BF16)
    v2 = v_ref[0, 0].reshape(n_mem * n_heads, hd).astype(BF16)
    s = lax.dot_general(q_ref[0], k2, (((1,), (1,)), ((), ())), preferred_element_type=F32) * scale
    row = lax.broadcasted_iota(jnp.int32, s.shape, 0)
    col = lax.broadcasted_iota(jnp.int32, s.shape, 1)
    s = jnp.where((row % n_heads) == (col % n_heads), s, -jnp.inf)
    p = _softmax_rows(s).astype(BF16)
    o_ref[0] = jnp.dot(p, v2, preferred_element_type=F32).astype(o_ref.dtype)


def xattn_sample(q3, cache_k, cache_v, layer):
    n_seq, rows, hd = q3.shape
    _, _, n_mem, n_heads, _ = cache_k.shape
    kv_spec = pl.BlockSpec((1, 1, n_mem, n_heads, hd), lambda b: (layer, b, 0, 0, 0))
    return pl.pallas_call(
        functools.partial(_xattn_sample_body, n_heads=n_heads, scale=hd ** -0.5),
        grid=(n_seq,),
        in_specs=[pl.BlockSpec((1, rows, hd), lambda b: (b, 0, 0)), kv_spec, kv_spec],
        out_specs=pl.BlockSpec((1, rows, hd), lambda b: (b, 0, 0)),
        out_shape=jax.ShapeDtypeStruct((n_seq, rows, hd), BF16),
        compiler_params=_cparams("parallel"),
        name="xattn_sample",
    )(q3, cache_k, cache_v)


def kernel(x_prompt, x_sample, mem_prompt, cache_mem_k, cache_mem_v, state_mlstm_c, state_mlstm_n, state_mlstm_m, state_lru_h, state_lru_conv, state_sconv, norm_mix, norm_x, norm_ff, norm_final, w_in_e, b_if, mh_norm, conv_b_w, conv_b_b, lru_wa, lru_ba, lru_wi, lru_bi, lru_lam, w_out_e, w_in_o, conv_c_w, w_out_o, w_xq, w_xk, w_xv, w_xo, w_ff1, w_ff2):
    bp, sp, d = x_prompt.shape
    bs, ss, _ = x_sample.shape
    n_mem = mem_prompt.shape[1]
    depth, _, _, xh, xhd = cache_mem_k.shape
    _, _, ah, dk, dv = state_mlstm_c.shape
    bw = state_lru_h.shape[-1]
    bconv = state_lru_conv.shape[2] + 1
    cconv = state_sconv.shape[2] + 1
    cwid = state_sconv.shape[-1]
    aw = ah * dv
    aqk = ah * dk
    tp = bp * sp
    ts = bs * ss
    t = tp + ts
    TM = 1088
    TN = 1024
    TK2 = 4096
    TM_RES = 544
    TM_MEM = 512
    assert t % TM == 0 and tp % 512 == 0 and ts == 512
    w_ff2 = w_ff2.astype(BF16)

    x = jnp.concatenate([x_prompt.reshape(tp, d), x_sample.reshape(ts, d)], axis=0)
    mem_bf = mem_prompt.reshape(bp * n_mem, d).astype(BF16)

    outs = {}
    ie = io = 0
    p_mem_k, p_mem_v = [], []
    for l in range(depth):
        u = rmsnorm(x, norm_mix[l])
        if l % 2 == 0:
            w = w_in_e[ie]
            o_qkvo = 2 * aqk + 2 * aw
            o_if = o_qkvo + 2 * ah
            w_if = jnp.pad(w[:, o_qkvo:o_if], ((0, 0), (0, V7X_LANES - 2 * ah))).astype(BF16)
            bias_if = jnp.pad(b_if[ie], (0, V7X_LANES - 2 * ah)).reshape(1, V7X_LANES)
            col_q, col_k, col_v, col_o = 0, aqk, 2 * aqk, 2 * aqk + aw
            col_x, col_g = 0, bw
            proj = matmul([u], w_in_e, layer=ie, n_out=o_qkvo, tm=TM, tn=TN, name="in_proj_even_qkvo")
            proj_b = matmul([u], w[:, o_if:], tm=TM, tn=TN, name="in_proj_even_lru")
            gc, gr = mlstm_gates(u, w_if, bias_if, n_prompt_rows=tp, seg_sample=ss)
            gr = gr.reshape(16, 1, t)
            mh = mh_norm[ie].reshape(1, aw)
            mkw = dict(n_heads=ah, dk=dk, dv=dv, col_q=col_q, col_k=col_k, col_v=col_v, col_o=col_o)
            ya_p, pc, pn, pm = mlstm_prompt(proj, gc, gr, mh, batch=bp, seq=sp, **mkw)
            n0_tok = jnp.repeat(jnp.transpose(state_mlstm_n[ie], (1, 0, 2)), ss, axis=1)
            m0_tok = jnp.broadcast_to(jnp.repeat(state_mlstm_m[ie].T, ss, axis=1)[:, :, None], (ah, ts, MLSTM_CHUNK))
            ya_s, sc_, sn_tok, sm_tok = mlstm_sample(proj, gc, gr, mh, state_mlstm_c[ie:ie + 1], n0_tok, m0_tok,
                                                     row0=tp, n_seq=bs, seg=ss, **mkw)
            outs.setdefault("pc", []).append(pc.reshape(bp, ah, dk, dv))
            outs.setdefault("pn", []).append(pn.reshape(bp, ah, dk))
            outs.setdefault("pm", []).append(pm.reshape(bp, ah))
            outs.setdefault("sc", []).append(sc_[0])
            outs.setdefault("sn", []).append(jnp.transpose(sn_tok[:, ss - 1::ss, :], (1, 0, 2)))
            outs.setdefault("sm", []).append(sm_tok[:, ss - 1::ss, 0].T)

            cw = conv_b_w[ie]
            cbias = conv_b_b[ie].reshape(1, bw)
            wa = lru_wa[ie].astype(BF16)
            wi = lru_wi[ie].astype(BF16)
            ba = lru_ba[ie].reshape(1, bw)
            bi = lru_bi[ie].reshape(1, bw)
            lam = lru_lam[ie].reshape(1, bw)
            halo_p = jnp.zeros((bp, V7X_SUBLANES, bw), F32)
            h0_p = jnp.zeros((bp, 1, bw), F32)
            yb_p, ph = lru_prompt(proj_b, halo_p, h0_p, cw, cbias, wa, wi, ba, bi, lam,
                                  batch=bp, seq=sp, width=bw, col_x=col_x, col_g=col_g)
            st_s = jnp.transpose(state_lru_conv[ie], (1, 0, 2))
            xb_p = proj_b[:tp, col_x:col_x + bw].reshape(bp, sp, bw)
            xb_s = proj_b[tp:, col_x:col_x + bw].reshape(bs, ss, bw)
            gb_s = proj_b[tp:, col_g:col_g + bw].reshape(bs, ss, bw)
            yb_s, sh = lru_sample(jnp.transpose(xb_s, (1, 0, 2)), jnp.transpose(gb_s, (1, 0, 2)), st_s,
                                  state_lru_h[ie], cw, cbias, wa, wi, ba, bi, lam)
            yb_s = jnp.transpose(yb_s, (1, 0, 2)).reshape(ts, bw)
            outs.setdefault("ph", []).append(ph.reshape(bp, bw))
            outs.setdefault("pcb", []).append(xb_p[:, sp - (bconv - 1):])
            outs.setdefault("sh", []).append(sh)
            outs.setdefault("scb", []).append(jnp.concatenate([state_lru_conv[ie], xb_s], axis=1)[:, ss:])

            ya = jnp.concatenate([ya_p, ya_s], axis=0)
            yb = jnp.concatenate([yb_p, yb_s], axis=0)
            x = matmul([ya, yb], w_out_e, layer=ie, res=x, tm=TM_RES, tn=TN, name="out_proj_even")
            ie += 1
        else:
            proj = matmul([u], w_in_o, layer=io, tm=TM, tn=TN, name="in_proj_odd")
            cw = conv_c_w[io]
            halo_p = jnp.zeros((bp, V7X_SUBLANES, cwid), F32)
            y_p, tail = sconv_prompt(proj, halo_p, cw, batch=bp, seq=sp, width=cwid)
            st_s = jnp.transpose(state_sconv[io], (1, 0, 2))
            proj_t = jnp.transpose(proj[tp:].reshape(bs, ss, 3 * cwid), (1, 0, 2))
            y_s, ps = sconv_sample(proj_t, st_s, cw, width=cwid)
            y_s = jnp.transpose(y_s, (1, 0, 2)).reshape(ts, cwid)
            outs.setdefault("psb", []).append(tail[:, V7X_SUBLANES - (cconv - 1):])
            ps_b = jnp.transpose(ps, (1, 0, 2))
            outs.setdefault("ssb", []).append(jnp.concatenate([state_sconv[io], ps_b], axis=1)[:, ss:])
            y = jnp.concatenate([y_p, y_s], axis=0)
            x = matmul([y], w_out_o, layer=io, res=x, tm=TM_RES, tn=TN, name="out_proj_odd")
            io += 1

        u = rmsnorm(x, norm_x[l])
        q = matmul([u], w_xq, layer=l, out_dtype=BF16, tm=TM, tn=TN, name="xattn_q")
        mk = matmul([mem_bf], w_xk, layer=l, tm=TM_MEM, tn=TN, name="mem_k")
        mv = matmul([mem_bf], w_xv, layer=l, tm=TM_MEM, tn=TN, name="mem_v")
        p_mem_k.append(mk.reshape(bp, n_mem, xh, xhd))
        p_mem_v.append(mv.reshape(bp, n_mem, xh, xhd))
        o_p = xattn_prompt(q, mk, mv, batch=bp, seq=sp, n_mem=n_mem, n_heads=xh, hd=xhd)
        o_s = xattn_sample(q[tp:].reshape(bs, ss * xh, xhd), cache_mem_k, cache_mem_v, l)
        o = jnp.concatenate([o_p, o_s.reshape(ts, d)], axis=0)
        x = matmul([o], w_xo, layer=l, res=x, tm=TM_RES, tn=TN, name="xattn_o")

        u = rmsnorm(x, norm_ff[l])
        hmid = matmul([u], w_ff1, layer=l, act="relu2", out_dtype=BF16, tm=TM, tn=TN, name="ff1")
        x = matmul_ksplit(hmid, w_ff2, l, x, tm=TM, tn=TN, tk=TK2, name="ff2")

    y_p = rmsnorm(x, norm_final, out_dtype=F32, row0=0, nrows=tp)
    y_s = rmsnorm(x, norm_final, out_dtype=F32, row0=tp, nrows=ts)
    dt = x_prompt.dtype
    st = lambda name: jnp.stack(outs[name]).astype(dt)
    return (y_p.reshape(bp, sp, d), y_s.reshape(bs, ss, d),
            st("pc"), st("pn"), st("pm"), st("ph"), st("pcb"), st("psb"),
            jnp.stack(p_mem_k), jnp.stack(p_mem_v),
            st("sc"), st("sn"), st("sm"), st("sh"), st("scb"), st("ssb"))
```

```python
import functools

import jax
import jax.numpy as jnp
from jax import lax
from jax.experimental import pallas as pl
from jax.experimental.pallas import tpu as pltpu

F32 = jnp.float32
BF16 = jnp.bfloat16
EPS = 1e-6
LRU_C = 8.0

V7X_LANES = 128
V7X_SUBLANES = 8
V7X_VMEM_BUDGET_BYTES = 60 * 1024 * 1024

MLSTM_CHUNK = 128
SCAN_CHUNK = 256


def _cparams(*semantics):
    return pltpu.CompilerParams(dimension_semantics=semantics, vmem_limit_bytes=V7X_VMEM_BUDGET_BYTES)


def _log_sigmoid(x):
    return jnp.minimum(x, 0.0) - jnp.log1p(jnp.exp(-jnp.abs(x)))


def _sigmoid(x):
    return 1.0 / (1.0 + jnp.exp(-x))


def _gelu_tanh(x):
    c = 0.7978845608028654
    return 0.5 * x * (1.0 + jnp.tanh(c * (x + 0.044715 * (x * x * x))))


def _expm1(x):
    u = jnp.exp(x)
    um1 = u - 1.0
    safe = jnp.where(um1 == 0.0, 1.0, jnp.log(u))
    return jnp.where(um1 == 0.0, x, jnp.where(um1 == -1.0, -1.0, um1 * x / safe))


def _split3(x):
    hi = x.astype(BF16)
    r1 = x - hi.astype(F32)
    mid = r1.astype(BF16)
    lo = (r1 - mid.astype(F32)).astype(BF16)
    return hi, mid, lo


def _dot01(mask_bf16, x):
    hi, mid, lo = _split3(x)
    acc = jnp.dot(mask_bf16, lo, preferred_element_type=F32)
    acc = acc + jnp.dot(mask_bf16, mid, preferred_element_type=F32)
    return acc + jnp.dot(mask_bf16, hi, preferred_element_type=F32)


def _rmsnorm_body(x_ref, g_ref, o_ref):
    x = x_ref[...]
    inv = lax.rsqrt(jnp.mean(x * x, axis=-1, keepdims=True) + EPS)
    o_ref[...] = ((x * inv) * g_ref[...]).astype(o_ref.dtype)


def rmsnorm(x, g, out_dtype=BF16, rows=256, row0=0, nrows=None):
    d = x.shape[1]
    nrows = x.shape[0] if nrows is None else nrows
    b0 = row0 // rows
    return pl.pallas_call(
        _rmsnorm_body,
        grid=(nrows // rows,),
        in_specs=[pl.BlockSpec((rows, d), lambda i: (b0 + i, 0)), pl.BlockSpec((1, d), lambda i: (0, 0))],
        out_specs=pl.BlockSpec((rows, d), lambda i: (i, 0)),
        out_shape=jax.ShapeDtypeStruct((nrows, d), out_dtype),
        compiler_params=_cparams("parallel"),
        name="rmsnorm",
    )(x, g.reshape(1, d))


def _place_rows_body(buf_ref, rows_ref, o_ref):
    del buf_ref
    o_ref[...] = rows_ref[...]


def place_rows(buf, rows, row0):
    r, c = rows.shape
    assert row0 % r == 0 and buf.shape[1] == c and buf.dtype == rows.dtype
    return pl.pallas_call(
        _place_rows_body,
        grid=(1,),
        in_specs=[pl.BlockSpec(memory_space=pl.ANY), pl.BlockSpec((r, c), lambda i: (0, 0))],
        out_specs=pl.BlockSpec((r, c), lambda i: (row0 // r, 0)),
        out_shape=jax.ShapeDtypeStruct(buf.shape, buf.dtype),
        input_output_aliases={0: 0},
        compiler_params=_cparams("arbitrary"),
        name="place_rows",
    )(buf, rows)


def _mm_body(*refs, n_in, has_res, act, kc, nj):
    x_refs = refs[:n_in]
    w_ref = refs[n_in]
    res_ref = refs[n_in + 1] if has_res else None
    n_io = n_in + 1 + (1 if has_res else 0)
    o_ref, wbf = refs[n_io], refs[n_io + 1]
    p, i = pl.program_id(0), pl.program_id(1)

    @pl.when(p > 0)
    def _():
        slot = (p - 1) % 2
        k0 = 0
        acc = None
        for x_ref in x_refs:
            kw = x_ref.shape[1]
            part = jnp.dot(x_ref[...], wbf[slot, k0:k0 + kw, :], preferred_element_type=F32)
            acc = part if acc is None else acc + part
            k0 += kw
        if act == "relu2":
            r = jnp.maximum(acc, 0.0)
            acc = r * r
        if has_res:
            acc = res_ref[...] + acc
        o_ref[...] = acc.astype(o_ref.dtype)

    @pl.when(p < nj)
    def _():
        wbf[p % 2, pl.ds(pl.multiple_of(i * kc, kc), kc), :] = w_ref[...].astype(BF16)


def matmul(xs, w, *, layer=None, n_out=None, res=None, act=None, out_dtype=F32, tm, tn, name):
    m = xs[0].shape[0]
    kk = sum(x.shape[1] for x in xs)
    n = w.shape[-1] if n_out is None else n_out
    ni, nj = m // tm, n // tn
    kc = kk // ni
    assert w.shape[-2] == kk and kk % ni == 0 and kc % 16 == 0 and n % tn == 0 and m % tm == 0
    n_in = len(xs)

    def row(p, i):
        return jnp.where(p > 0, i, 0)

    def w_idx(p, i):
        blk = (jnp.where(p < nj, i, ni - 1), jnp.minimum(p, nj - 1))
        return blk if layer is None else (layer,) + blk

    in_specs = [pl.BlockSpec((tm, x.shape[1]), lambda p, i: (row(p, i), 0)) for x in xs]
    in_specs.append(pl.BlockSpec((kc, tn) if layer is None else (None, kc, tn), w_idx))
    args = list(xs) + [w]
    out_idx = lambda p, i: (row(p, i), jnp.maximum(p - 1, 0))
    if res is not None:
        in_specs.append(pl.BlockSpec((tm, tn), out_idx))
        args.append(res)
    return pl.pallas_call(
        functools.partial(_mm_body, n_in=n_in, has_res=res is not None, act=act, kc=kc, nj=nj),
        grid=(nj + 1, ni),
        in_specs=in_specs,
        out_specs=pl.BlockSpec((tm, tn), out_idx),
        out_shape=jax.ShapeDtypeStruct((m, n), out_dtype),
        scratch_shapes=[pltpu.VMEM((2, kk, tn), BF16)],
        compiler_params=_cparams("arbitrary", "arbitrary"),
        name=name,
    )(*args)


def _mm_acc_body(x_ref, w_ref, res_ref, o_ref):
    k = pl.program_id(2)

    @pl.when(k == 0)
    def _():
        o_ref[...] = res_ref[...] + jnp.dot(x_ref[...], w_ref[...], preferred_element_type=F32)

    @pl.when(k != 0)
    def _():
        o_ref[...] += jnp.dot(x_ref[...], w_ref[...], preferred_element_type=F32)


def matmul_ksplit(x, w, layer, res, *, tm, tn, tk, name):
    m, kk = x.shape
    n = w.shape[-1]
    return pl.pallas_call(
        _mm_acc_body,
        grid=(m // tm, n // tn, kk // tk),
        in_specs=[
            pl.BlockSpec((tm, tk), lambda i, j, k: (i, k)),
            pl.BlockSpec((None, tk, tn), lambda i, j, k: (layer, k, j)),
            pl.BlockSpec((tm, tn), lambda i, j, k: (i, j)),
        ],
        out_specs=pl.BlockSpec((tm, tn), lambda i, j, k: (i, j)),
        out_shape=jax.ShapeDtypeStruct((m, n), F32),
        compiler_params=_cparams("parallel", "parallel", "arbitrary"),
        name=name,
    )(x, w, res)


def _gates_body(u_ref, w_ref, b_ref, gc_ref, gr_ref, *, tm, n_prompt_tiles, seg_sample):
    L = MLSTM_CHUNK
    g = jnp.dot(u_ref[...], w_ref[...], preferred_element_type=F32) + b_ref[...]
    lane = lax.broadcasted_iota(jnp.int32, (tm, V7X_LANES), 1)
    val = jnp.where(lane < 8, g, _log_sigmoid(g))
    row = lax.broadcasted_iota(jnp.int32, (L, L), 0)
    col = lax.broadcasted_iota(jnp.int32, (L, L), 1)
    is_sample = pl.program_id(0) >= n_prompt_tiles
    tril = jnp.where(col <= row, 1.0, 0.0)
    tril_seg = jnp.where((col <= row) & ((row // seg_sample) == (col // seg_sample)), 1.0, 0.0)
    mask = jnp.where(is_sample, tril_seg, tril).astype(BF16)
    lane_l = lax.broadcasted_iota(jnp.int32, (L, V7X_LANES), 1)
    for s in range(tm // L):
        x = val[s * L:(s + 1) * L]
        out = jnp.where(lane_l < 8, x, _dot01(mask, x))
        gr_ref[:, s * L:(s + 1) * L] = out.T[:16]
        for c in range(16):
            gc_ref[c, s * L:(s + 1) * L, :] = jnp.broadcast_to(out[:, c:c + 1], (L, V7X_LANES))


def mlstm_gates(u, w_if, b_if, *, n_prompt_rows, seg_sample, tm=512):
    t, d = u.shape
    return pl.pallas_call(
        functools.partial(_gates_body, tm=tm, n_prompt_tiles=n_prompt_rows // tm, seg_sample=seg_sample),
        grid=(t // tm,),
        in_specs=[
            pl.BlockSpec((tm, d), lambda i: (i, 0)),
            pl.BlockSpec((d, V7X_LANES), lambda i: (0, 0)),
            pl.BlockSpec((1, V7X_LANES), lambda i: (0, 0)),
        ],
        out_specs=[
            pl.BlockSpec((16, tm, V7X_LANES), lambda i: (0, i, 0)),
            pl.BlockSpec((16, tm), lambda i: (0, i)),
        ],
        out_shape=[
            jax.ShapeDtypeStruct((16, t, V7X_LANES), F32),
            jax.ShapeDtypeStruct((16, t), F32),
        ],
        compiler_params=_cparams("parallel"),
        name="mlstm_gates",
    )(u, w_if, b_if)


def _head_out(hh, o, mh):
    hn = hh * lax.rsqrt(jnp.mean(hh * hh, axis=-1, keepdims=True) + EPS)
    return _sigmoid(o) * (hn * mh)


def _mlstm_prompt_body(q_ref, k_ref, v_ref, o_ref, ic_ref, bc_ref, ir_ref, br_ref, mh_ref,
                       ya_ref, c_ref, n_ref, m_ref, c_scr, n_scr, m_scr, *, n_heads, seq, scale):
    L = MLSTM_CHUNK
    c_scr[...] = jnp.zeros_like(c_scr)
    n_scr[...] = jnp.zeros_like(n_scr)
    m_scr[...] = jnp.zeros_like(m_scr)
    row = lax.broadcasted_iota(jnp.int32, (L, L), 0)
    col = lax.broadcasted_iota(jnp.int32, (L, L), 1)
    causal = col <= row

    def chunk(ci, carry):
        r0 = pl.multiple_of(ci * L, L)
        q = q_ref[pl.ds(r0, L), :] * scale
        k = k_ref[pl.ds(r0, L), :]
        v = v_ref[pl.ds(r0, L), :]
        bc = bc_ref[0, pl.ds(r0, L), :]
        ic = ic_ref[0, pl.ds(r0, L), :]
        i_row = ir_ref[0, :, pl.ds(r0, L)]
        b_row = br_ref[0, :, pl.ds(r0, L)]
        m_prev = m_scr[...]
        c_prev = c_scr[...]
        n_prev = n_scr[...]
        dmat = jnp.where(causal, bc - b_row + i_row, -jnp.inf)
        g = bc + m_prev
        mt = jnp.maximum(g, jnp.max(dmat, axis=-1, keepdims=True))
        w_inter = jnp.exp(g - mt)
        qb = q.astype(BF16)
        kb = k.astype(BF16)
        vb = v.astype(BF16)
        sc = lax.dot_general(qb, kb, (((1,), (1,)), ((), ())), preferred_element_type=F32) * jnp.exp(dmat - mt)
        inter = jnp.dot(qb, c_prev.astype(BF16), preferred_element_type=F32)
        num = jnp.concatenate([w_inter, w_inter], axis=1) * inter + jnp.dot(sc.astype(BF16), vb, preferred_element_type=F32)
        qn = jnp.sum(q * n_prev, axis=-1, keepdims=True)
        den = w_inter[:, :1] * qn + jnp.sum(sc, axis=-1, keepdims=True)
        hh = num / jnp.maximum(jnp.abs(den), jnp.exp(-mt[:, :1]))
        ya_ref[pl.ds(r0, L), :] = _head_out(hh, o_ref[pl.ds(r0, L), :], mh_ref[...]).astype(ya_ref.dtype)
        m_new = mt[L - 1:L, :1]
        b_last = b_row[:, L - 1:L]
        decay = jnp.exp(b_last + m_prev - m_new)
        kw = k * jnp.exp(b_last - bc + ic - m_new)
        c_scr[...] = decay * c_prev + lax.dot_general(kw.astype(BF16), vb, (((0,), (0,)), ((), ())), preferred_element_type=F32)
        n_scr[...] = decay * n_prev + jnp.sum(kw, axis=0, keepdims=True)
        m_scr[...] = m_new
        return carry

    lax.fori_loop(0, seq // L, chunk, 0)
    c_ref[0, 0] = c_scr[...]
    n_ref[0, 0] = n_scr[...]
    m_ref[0, 0] = m_scr[...]


def mlstm_prompt(proj, gc, gr, mh_norm, *, batch, seq, n_heads, dk, dv, col_q, col_k, col_v, col_o):
    L = MLSTM_CHUNK
    assert dk == L and dv == 2 * L
    rows = proj.shape[0]
    return pl.pallas_call(
        functools.partial(_mlstm_prompt_body, n_heads=n_heads, seq=seq, scale=dk ** -0.5),
        grid=(batch, n_heads),
        in_specs=[
            pl.BlockSpec((seq, dk), lambda b, h: (b, col_q // dk + h)),
            pl.BlockSpec((seq, dk), lambda b, h: (b, col_k // dk + h)),
            pl.BlockSpec((seq, dv), lambda b, h: (b, col_v // dv + h)),
            pl.BlockSpec((seq, dv), lambda b, h: (b, col_o // dv + h)),
            pl.BlockSpec((1, seq, V7X_LANES), lambda b, h: (h, b, 0)),
            pl.BlockSpec((1, seq, V7X_LANES), lambda b, h: (h + n_heads, b, 0)),
            pl.BlockSpec((1, 1, seq), lambda b, h: (h, 0, b)),
            pl.BlockSpec((1, 1, seq), lambda b, h: (h + n_heads, 0, b)),
            pl.BlockSpec((1, dv), lambda b, h: (0, h)),
        ],
        out_specs=[
            pl.BlockSpec((seq, dv), lambda b, h: (b, h)),
            pl.BlockSpec((1, 1, dk, dv), lambda b, h: (b, h, 0, 0)),
            pl.BlockSpec((1, 1, 1, dk), lambda b, h: (b, h, 0, 0)),
            pl.BlockSpec((1, 1, 1, 1), lambda b, h: (b, h, 0, 0)),
        ],
        out_shape=[
            jax.ShapeDtypeStruct((rows, n_heads * dv), BF16),
            jax.ShapeDtypeStruct((batch, n_heads, dk, dv), F32),
            jax.ShapeDtypeStruct((batch, n_heads, 1, dk), F32),
            jax.ShapeDtypeStruct((batch, n_heads, 1, 1), F32),
        ],
        scratch_shapes=[pltpu.VMEM((dk, dv), F32), pltpu.VMEM((1, dk), F32), pltpu.VMEM((1, 1), F32)],
        compiler_params=_cparams("parallel", "parallel"),
        name="mlstm_prompt",
    )(proj, proj, proj, proj, gc, gc, gr, gr, mh_norm)


def _mlstm_sample_body(q_ref, k_ref, v_ref, o_ref, ic_ref, bc_ref, ir_ref, br_ref, mh_ref, c0_ref, n0_ref, m0_ref,
                       ya_ref, c_ref, n_ref, m_ref, inter_scr, *, n_heads, seg, scale):
    L = MLSTM_CHUNK
    nb = L // seg
    row = lax.broadcasted_iota(jnp.int32, (L, L), 0)
    col = lax.broadcasted_iota(jnp.int32, (L, L), 1)
    same = (row // seg) == (col // seg)
    causal = (col <= row) & same
    last = col == (row // seg) * seg + (seg - 1)
    q = q_ref[...] * scale
    k = k_ref[...]
    v = v_ref[...]
    bc = bc_ref[0]
    ic = ic_ref[0]
    i_row = ir_ref[0]
    b_row = br_ref[0]
    m_prev = m0_ref[0]
    n_prev = n0_ref[0]
    dmat = jnp.where(causal, bc - b_row + i_row, -jnp.inf)
    g = bc + m_prev
    mt = jnp.maximum(g, jnp.max(dmat, axis=-1, keepdims=True))
    w_inter = jnp.exp(g - mt)
    qb = q.astype(BF16)
    kb = k.astype(BF16)
    vb = v.astype(BF16)
    sc = lax.dot_general(qb, kb, (((1,), (1,)), ((), ())), preferred_element_type=F32) * jnp.exp(dmat - mt)

    per_group = V7X_SUBLANES // seg
    sub = lax.broadcasted_iota(jnp.int32, (V7X_SUBLANES, 2 * L), 0)
    for gi in range(L // V7X_SUBLANES):
        q8 = qb[gi * V7X_SUBLANES:(gi + 1) * V7X_SUBLANES]
        acc = jnp.zeros((V7X_SUBLANES, 2 * L), F32)
        for j in range(per_group):
            r = jnp.dot(q8, c0_ref[0, gi * per_group + j, 0].astype(BF16), preferred_element_type=F32)
            acc = jnp.where((sub // seg) == j, r, acc)
        inter_scr[gi * V7X_SUBLANES:(gi + 1) * V7X_SUBLANES, :] = acc
    inter = inter_scr[...]

    num = jnp.concatenate([w_inter, w_inter], axis=1) * inter + jnp.dot(sc.astype(BF16), vb, preferred_element_type=F32)
    qn = jnp.sum(q * n_prev, axis=-1, keepdims=True)
    den = w_inter[:, :1] * qn + jnp.sum(sc, axis=-1, keepdims=True)
    hh = num / jnp.maximum(jnp.abs(den), jnp.exp(-mt[:, :1]))
    ya_ref[...] = _head_out(hh, o_ref[...], mh_ref[...]).astype(ya_ref.dtype)

    b_last = jnp.sum(jnp.where(last, jnp.broadcast_to(b_row, (L, L)), 0.0), axis=-1, keepdims=True)
    mt_row = mt.T
    m_new = jnp.sum(jnp.where(last, mt_row, 0.0), axis=-1, keepdims=True)
    decay = jnp.exp(b_last + m_prev - m_new)
    kw = k * jnp.exp(b_last - bc + ic - m_new)
    seg_ones = jnp.where(same, 1.0, 0.0).astype(BF16)
    n_ref[0] = decay * n_prev + _dot01(seg_ones, kw)
    m_ref[0] = jnp.broadcast_to(m_new, (L, L))
    rowk = lax.broadcasted_iota(jnp.int32, (L, L), 0)
    for bi in range(nb):
        kw_b = jnp.where((rowk // seg) == bi, kw, 0.0).astype(BF16)
        upd = lax.dot_general(kw_b, vb, (((0,), (0,)), ((), ())), preferred_element_type=F32)
        d_b = decay[bi * seg:bi * seg + 1, :1]
        c_ref[0, bi, 0] = d_b * c0_ref[0, bi, 0] + upd


def mlstm_sample(proj, gc, gr, mh_norm, c0, n0_tok, m0_tok, *, row0, n_seq, seg, n_heads, dk, dv,
                 col_q, col_k, col_v, col_o):
    L = MLSTM_CHUNK
    assert dk == L and dv == 2 * L and V7X_SUBLANES % seg == 0 and row0 % L == 0
    rows = n_seq * seg
    nb = L // seg
    t0 = row0 // L
    return pl.pallas_call(
        functools.partial(_mlstm_sample_body, n_heads=n_heads, seg=seg, scale=dk ** -0.5),
        grid=(rows // L, n_heads),
        in_specs=[
            pl.BlockSpec((L, dk), lambda j, h: (t0 + j, col_q // dk + h)),
            pl.BlockSpec((L, dk), lambda j, h: (t0 + j, col_k // dk + h)),
            pl.BlockSpec((L, dv), lambda j, h: (t0 + j, col_v // dv + h)),
            pl.BlockSpec((L, dv), lambda j, h: (t0 + j, col_o // dv + h)),
            pl.BlockSpec((1, L, V7X_LANES), lambda j, h: (h, t0 + j, 0)),
            pl.BlockSpec((1, L, V7X_LANES), lambda j, h: (h + n_heads, t0 + j, 0)),
            pl.BlockSpec((1, 1, L), lambda j, h: (h, 0, t0 + j)),
            pl.BlockSpec((1, 1, L), lambda j, h: (h + n_heads, 0, t0 + j)),
            pl.BlockSpec((1, dv), lambda j, h: (0, h)),
            pl.BlockSpec((1, nb, 1, dk, dv), lambda j, h: (0, j, h, 0, 0)),
            pl.BlockSpec((1, L, dk), lambda j, h: (h, j, 0)),
            pl.BlockSpec((1, L, L), lambda j, h: (h, j, 0)),
        ],
        out_specs=[
            pl.BlockSpec((L, dv), lambda j, h: (j, h)),
            pl.BlockSpec((1, nb, 1, dk, dv), lambda j, h: (0, j, h, 0, 0)),
            pl.BlockSpec((1, L, dk), lambda j, h: (h, j, 0)),
            pl.BlockSpec((1, L, L), lambda j, h: (h, j, 0)),
        ],
        out_shape=[
            jax.ShapeDtypeStruct((rows, n_heads * dv), BF16),
            jax.ShapeDtypeStruct((1, n_seq, n_heads, dk, dv), F32),
            jax.ShapeDtypeStruct((n_heads, rows, dk), F32),
            jax.ShapeDtypeStruct((n_heads, rows, L), F32),
        ],
        scratch_shapes=[pltpu.VMEM((L, dv), F32)],
        compiler_params=_cparams("parallel", "parallel"),
        name="mlstm_sample",
    )(proj, proj, proj, proj, gc, gc, gr, gr, mh_norm, c0, n0_tok, m0_tok)


def _lru_gates(xc, wa_ref, wi_ref, ba, bi, lam):
    nblk = wa_ref.shape[0]
    blk = wa_ref.shape[1]
    rs, igs = [], []
    for n in range(nblk):
        xb = xc[:, n * blk:(n + 1) * blk].astype(BF16)
        rs.append(jnp.dot(xb, wa_ref[n], preferred_element_type=F32))
        igs.append(jnp.dot(xb, wi_ref[n], preferred_element_type=F32))
    r = _sigmoid(jnp.concatenate(rs, axis=1) + ba)
    ig = _sigmoid(jnp.concatenate(igs, axis=1) + bi)
    log_a = LRU_C * r * _log_sigmoid(lam)
    a = jnp.exp(log_a)
    u = jnp.sqrt(-_expm1(2.0 * log_a)) * (ig * xc)
    return a, u


def _lru_prompt_body(x_ref, g_ref, halo_ref, h0_ref, cw_ref, cb_ref, wa_ref, wi_ref, ba_ref, bi_ref, lam_ref,
                     y_ref, hl_ref, h_scr, *, seq):
    Lc = SCAN_CHUNK
    cbw = x_ref.shape[1]
    h_scr[...] = h0_ref[0]
    row = lax.broadcasted_iota(jnp.int32, (Lc, cbw), 0)
    cw = cw_ref[...]
    width = cw.shape[0]

    def chunk(ci, carry):
        r0 = pl.multiple_of(ci * Lc, Lc)
        x = x_ref[pl.ds(r0, Lc), :]
        prev = x_ref[pl.ds(pl.multiple_of(jnp.maximum(r0 - V7X_SUBLANES, 0), V7X_SUBLANES), V7X_SUBLANES), :]
        prev = jnp.where(ci == 0, halo_ref[0], prev)
        xfull = jnp.concatenate([prev, x], axis=0)
        xc = None
        for j in range(width - 1, 0, -1):
            term = pltpu.roll(xfull, j, 0)[V7X_SUBLANES:] * cw[width - 1 - j:width - j]
            xc = term if xc is None else xc + term
        xc = xc + x * cw[width - 1:width] + cb_ref[...]
        a, u = _lru_gates(xc, wa_ref, wi_ref, ba_ref[...], bi_ref[...], lam_ref[...])
        d = 1
        while d < Lc:
            ok = row >= d
            a_sh = jnp.where(ok, pltpu.roll(a, d, 0), 1.0)
            u_sh = jnp.where(ok, pltpu.roll(u, d, 0), 0.0)
            u = a * u_sh + u
            a = a * a_sh
            d *= 2
        hs = u + a * h_scr[...]
        h_scr[...] = hs[Lc - 1:Lc]
        y_ref[pl.ds(r0, Lc), :] = (hs * _gelu_tanh(g_ref[pl.ds(r0, Lc), :])).astype(y_ref.dtype)
        return carry

    lax.fori_loop(0, seq // Lc, chunk, 0)
    hl_ref[0] = h_scr[...]


def lru_prompt(proj, halo, h0, cw, cb, wa, wi, ba, bi, lam, *, batch, seq, width, col_x, col_g, cbw=256):
    blk = wa.shape[1]
    nper = cbw // blk
    vec = lambda b, c: (0, c)
    return pl.pallas_call(
        functools.partial(_lru_prompt_body, seq=seq),
        grid=(batch, width // cbw),
        in_specs=[
            pl.BlockSpec((seq, cbw), lambda b, c: (b, col_x // cbw + c)),
            pl.BlockSpec((seq, cbw), lambda b, c: (b, col_g // cbw + c)),
            pl.BlockSpec((1, V7X_SUBLANES, cbw), lambda b, c: (b, 0, c)),
            pl.BlockSpec((1, 1, cbw), lambda b, c: (b, 0, c)),
            pl.BlockSpec((cw.shape[0], cbw), vec),
            pl.BlockSpec((1, cbw), vec),
            pl.BlockSpec((nper, blk, blk), lambda b, c: (c, 0, 0)),
            pl.BlockSpec((nper, blk, blk), lambda b, c: (c, 0, 0)),
            pl.BlockSpec((1, cbw), vec),
            pl.BlockSpec((1, cbw), vec),
            pl.BlockSpec((1, cbw), vec),
        ],
        out_specs=[
            pl.BlockSpec((seq, cbw), lambda b, c: (b, c)),
            pl.BlockSpec((1, 1, cbw), lambda b, c: (b, 0, c)),
        ],
        out_shape=[
            jax.ShapeDtypeStruct((proj.shape[0], width), BF16),
            jax.ShapeDtypeStruct((batch, 1, width), F32),
        ],
        scratch_shapes=[pltpu.VMEM((1, cbw), F32)],
        compiler_params=_cparams("parallel", "parallel"),
        name="lru_prompt",
    )(proj, proj, halo, h0, cw, cb, wa, wi, ba, bi, lam)


def _lru_sample_body(x_ref, g_ref, st_ref, h0_ref, cw_ref, cb_ref, wa_ref, wi_ref, ba_ref, bi_ref, lam_ref,
                     y_ref, hl_ref, *, seg):
    cw = cw_ref[...]
    width = cw.shape[0]
    hist = [st_ref[j] for j in range(width - 1)]
    h = h0_ref[...]
    for t in range(seg):
        x = x_ref[t]
        taps = hist + [x]
        xc = taps[0] * cw[0:1]
        for j in range(1, width):
            xc = xc + taps[j] * cw[j:j + 1]
        xc = xc + cb_ref[...]
        a, u = _lru_gates(xc, wa_ref, wi_ref, ba_ref[...], bi_ref[...], lam_ref[...])
        h = a * h + u
        y_ref[t] = (h * _gelu_tanh(g_ref[t])).astype(y_ref.dtype)
        hist = hist[1:] + [x]
    hl_ref[...] = h


def lru_sample(x_t, g_t, st, h0, cw, cb, wa, wi, ba, bi, lam, *, cbw=256):
    seg, n_seq, width = x_t.shape
    blk = wa.shape[1]
    nper = cbw // blk
    vec = lambda c: (0, c)
    return pl.pallas_call(
        functools.partial(_lru_sample_body, seg=seg),
        grid=(width // cbw,),
        in_specs=[
            pl.BlockSpec((seg, n_seq, cbw), lambda c: (0, 0, c)),
            pl.BlockSpec((seg, n_seq, cbw), lambda c: (0, 0, c)),
            pl.BlockSpec((st.shape[0], n_seq, cbw), lambda c: (0, 0, c)),
            pl.BlockSpec((n_seq, cbw), vec),
            pl.BlockSpec((cw.shape[0], cbw), vec),
            pl.BlockSpec((1, cbw), vec),
            pl.BlockSpec((nper, blk, blk), lambda c: (c, 0, 0)),
            pl.BlockSpec((nper, blk, blk), lambda c: (c, 0, 0)),
            pl.BlockSpec((1, cbw), vec),
            pl.BlockSpec((1, cbw), vec),
            pl.BlockSpec((1, cbw), vec),
        ],
        out_specs=[
            pl.BlockSpec((seg, n_seq, cbw), lambda c: (0, 0, c)),
            pl.BlockSpec((n_seq, cbw), vec),
        ],
        out_shape=[
            jax.ShapeDtypeStruct((seg, n_seq, width), BF16),
            jax.ShapeDtypeStruct((n_seq, width), F32),
        ],
        compiler_params=_cparams("parallel"),
        name="lru_sample",
    )(x_t, g_t, st, h0, cw, cb, wa, wi, ba, bi, lam)


def _sconv_prompt_body(bg_ref, cg_ref, hx_ref, halo_ref, cw_ref, y_ref, tail_ref, *, seq):
    Lc = SCAN_CHUNK
    cw = cw_ref[...]
    width = cw.shape[0]

    def chunk(ci, carry):
        r0 = pl.multiple_of(ci * Lc, Lc)
        p = cg_ref[pl.ds(r0, Lc), :] * hx_ref[pl.ds(r0, Lc), :]
        rp = pl.multiple_of(jnp.maximum(r0 - V7X_SUBLANES, 0), V7X_SUBLANES)
        prev = cg_ref[pl.ds(rp, V7X_SUBLANES), :] * hx_ref[pl.ds(rp, V7X_SUBLANES), :]
        prev = jnp.where(ci == 0, halo_ref[0], prev)
        pfull = jnp.concatenate([prev, p], axis=0)
        z = None
        for j in range(width - 1, 0, -1):
            term = pltpu.roll(pfull, j, 0)[V7X_SUBLANES:] * cw[width - 1 - j:width - j]
            z = term if z is None else z + term
        z = z + p * cw[width - 1:width]
        y_ref[pl.ds(r0, Lc), :] = (bg_ref[pl.ds(r0, Lc), :] * z).astype(y_ref.dtype)
        return carry

    lax.fori_loop(0, seq // Lc, chunk, 0)
    rt = seq - V7X_SUBLANES
    tail_ref[0] = cg_ref[pl.ds(rt, V7X_SUBLANES), :] * hx_ref[pl.ds(rt, V7X_SUBLANES), :]


def sconv_prompt(proj, halo, cw, *, batch, seq, width, cbw=512):
    nb = width // cbw
    return pl.pallas_call(
        functools.partial(_sconv_prompt_body, seq=seq),
        grid=(batch, nb),
        in_specs=[
            pl.BlockSpec((seq, cbw), lambda b, c: (b, c)),
            pl.BlockSpec((seq, cbw), lambda b, c: (b, nb + c)),
            pl.BlockSpec((seq, cbw), lambda b, c: (b, 2 * nb + c)),
            pl.BlockSpec((1, V7X_SUBLANES, cbw), lambda b, c: (b, 0, c)),
            pl.BlockSpec((cw.shape[0], cbw), lambda b, c: (0, c)),
        ],
        out_specs=[
            pl.BlockSpec((seq, cbw), lambda b, c: (b, c)),
            pl.BlockSpec((1, V7X_SUBLANES, cbw), lambda b, c: (b, 0, c)),
        ],
        out_shape=[
            jax.ShapeDtypeStruct((proj.shape[0], width), BF16),
            jax.ShapeDtypeStruct((batch, V7X_SUBLANES, width), F32),
        ],
        compiler_params=_cparams("parallel", "parallel"),
        name="sconv_prompt",
    )(proj, proj, proj, halo, cw)


def _sconv_sample_body(bg_ref, cg_ref, hx_ref, st_ref, cw_ref, y_ref, ps_ref, *, seg):
    cw = cw_ref[...]
    width = cw.shape[0]
    hist = [st_ref[j] for j in range(width - 1)]
    for t in range(seg):
        p = cg_ref[t] * hx_ref[t]
        ps_ref[t] = p
        taps = hist + [p]
        z = taps[0] * cw[0:1]
        for j in range(1, width):
            z = z + taps[j] * cw[j:j + 1]
        y_ref[t] = (bg_ref[t] * z).astype(y_ref.dtype)
        hist = hist[1:] + [p]


def sconv_sample(proj_t, st, cw, *, width, cbw=512):
    seg, n_seq, _ = proj_t.shape
    nb = width // cbw
    return pl.pallas_call(
        functools.partial(_sconv_sample_body, seg=seg),
        grid=(nb,),
        in_specs=[
            pl.BlockSpec((seg, n_seq, cbw), lambda c: (0, 0, c)),
            pl.BlockSpec((seg, n_seq, cbw), lambda c: (0, 0, nb + c)),
            pl.BlockSpec((seg, n_seq, cbw), lambda c: (0, 0, 2 * nb + c)),
            pl.BlockSpec((st.shape[0], n_seq, cbw), lambda c: (0, 0, c)),
            pl.BlockSpec((cw.shape[0], cbw), lambda c: (0, c)),
        ],
        out_specs=[
            pl.BlockSpec((seg, n_seq, cbw), lambda c: (0, 0, c)),
            pl.BlockSpec((seg, n_seq, cbw), lambda c: (0, 0, c)),
        ],
        out_shape=[
            jax.ShapeDtypeStruct((seg, n_seq, width), BF16),
            jax.ShapeDtypeStruct((seg, n_seq, width), F32),
        ],
        compiler_params=_cparams("parallel"),
        name="sconv_sample",
    )(proj_t, proj_t, proj_t, st, cw)


def _softmax_rows(s):
    e = jnp.exp(s - jnp.max(s, axis=-1, keepdims=True))
    return e / jnp.sum(e, axis=-1, keepdims=True)


def _xattn_prompt_body(q_ref, k_ref, v_ref, o_ref, *, scale):
    s = lax.dot_general(q_ref[...], k_ref[...].astype(BF16), (((1,), (1,)), ((), ())), preferred_element_type=F32) * scale
    p = _softmax_rows(s).astype(BF16)
    o_ref[...] = jnp.dot(p, v_ref[...].astype(BF16), preferred_element_type=F32).astype(o_ref.dtype)


def xattn_prompt(q, mk, mv, *, batch, seq, n_mem, n_heads, hd, tq=1024):
    nq = seq // tq
    return pl.pallas_call(
        functools.partial(_xattn_prompt_body, scale=hd ** -0.5),
        grid=(batch, n_heads, nq),
        in_specs=[
            pl.BlockSpec((tq, hd), lambda b, h, i: (b * nq + i, h)),
            pl.BlockSpec((n_mem, hd), lambda b, h, i: (b, h)),
            pl.BlockSpec((n_mem, hd), lambda b, h, i: (b, h)),
        ],
        out_specs=pl.BlockSpec((tq, hd), lambda b, h, i: (b * nq + i, h)),
        out_shape=jax.ShapeDtypeStruct((q.shape[0], n_heads * hd), BF16),
        compiler_params=_cparams("parallel", "parallel", "parallel"),
        name="xattn_prompt",
    )(q, mk, mv)


def _xattn_sample_body(q_ref, k_ref, v_ref, o_ref, *, n_heads, scale):
    n_mem, hd = k_ref.shape[2], k_ref.shape[4]
    k2 = k_ref[0, 0].reshape(n_mem * n_heads, hd).astype(BF16)
    v2 = v_ref[0, 0].reshape(n_mem * n_heads, hd).astype(BF16)
    s = lax.dot_general(q_ref[0], k2, (((1,), (1,)), ((), ())), preferred_element_type=F32) * scale
    row = lax.broadcasted_iota(jnp.int32, s.shape, 0)
    col = lax.broadcasted_iota(jnp.int32, s.shape, 1)
    s = jnp.where((row % n_heads) == (col % n_heads), s, -jnp.inf)
    p = _softmax_rows(s).astype(BF16)
    o_ref[0] = jnp.dot(p, v2, preferred_element_type=F32).astype(o_ref.dtype)


def xattn_sample(q3, cache_k, cache_v, layer):
    n_seq, rows, hd = q3.shape
    _, _, n_mem, n_heads, _ = cache_k.shape
    kv_spec = pl.BlockSpec((1, 1, n_mem, n_heads, hd), lambda b: (layer, b, 0, 0, 0))
    return pl.pallas_call(
        functools.partial(_xattn_sample_body, n_heads=n_heads, scale=hd ** -0.5),
        grid=(n_seq,),
        in_specs=[pl.BlockSpec((1, rows, hd), lambda b: (b, 0, 0)), kv_spec, kv_spec],
        out_specs=pl.BlockSpec((1, rows, hd), lambda b: (b, 0, 0)),
        out_shape=jax.ShapeDtypeStruct((n_seq, rows, hd), BF16),
        compiler_params=_cparams("parallel"),
        name="xattn_sample",
    )(q3, cache_k, cache_v)


def kernel(x_prompt, x_sample, mem_prompt, cache_mem_k, cache_mem_v, state_mlstm_c, state_mlstm_n, state_mlstm_m, state_lru_h, state_lru_conv, state_sconv, norm_mix, norm_x, norm_ff, norm_final, w_in_e, b_if, mh_norm, conv_b_w, conv_b_b, lru_wa, lru_ba, lru_wi, lru_bi, lru_lam, w_out_e, w_in_o, conv_c_w, w_out_o, w_xq, w_xk, w_xv, w_xo, w_ff1, w_ff2):
    bp, sp, d = x_prompt.shape
    bs, ss, _ = x_sample.shape
    n_mem = mem_prompt.shape[1]
    depth, _, _, xh, xhd = cache_mem_k.shape
    _, _, ah, dk, dv = state_mlstm_c.shape
    bw = state_lru_h.shape[-1]
    bconv = state_lru_conv.shape[2] + 1
    cconv = state_sconv.shape[2] + 1
    cwid = state_sconv.shape[-1]
    aw = ah * dv
    aqk = ah * dk
    tp = bp * sp
    ts = bs * ss
    t = tp + ts
    TM = 1088
    TN = 1024
    TK2 = 4096
    TM_RES = 544
    TM_MEM = 512
    assert t % TM == 0 and tp % 512 == 0 and ts == 512
    w_ff2 = w_ff2.astype(BF16)

    x = jnp.concatenate([x_prompt.reshape(tp, d), x_sample.reshape(ts, d)], axis=0)
    mem_bf = mem_prompt.reshape(bp * n_mem, d).astype(BF16)

    outs = {}
    ie = io = 0
    p_mem_k, p_mem_v = [], []
    for l in range(depth):
        u = rmsnorm(x, norm_mix[l])
        if l % 2 == 0:
            w = w_in_e[ie]
            o_qkvo = 2 * aqk + 2 * aw
            o_if = o_qkvo + 2 * ah
            w_if = jnp.pad(w[:, o_qkvo:o_if], ((0, 0), (0, V7X_LANES - 2 * ah))).astype(BF16)
            bias_if = jnp.pad(b_if[ie], (0, V7X_LANES - 2 * ah)).reshape(1, V7X_LANES)
            col_q, col_k, col_v, col_o = 0, aqk, 2 * aqk, 2 * aqk + aw
            col_x, col_g = 0, bw
            proj = matmul([u], w[:, :o_qkvo], tm=TM, tn=TN, name="in_proj_even_qkvo")
            proj_b = matmul([u], w[:, o_if:].astype(BF16), tm=TM, tn=TN, name="in_proj_even_lru")
            gc, gr = mlstm_gates(u, w_if, bias_if, n_prompt_rows=tp, seg_sample=ss)
            gr = gr.reshape(16, 1, t)
            mh = mh_norm[ie].reshape(1, aw)
            mkw = dict(n_heads=ah, dk=dk, dv=dv, col_q=col_q, col_k=col_k, col_v=col_v, col_o=col_o)
            ya_p, pc, pn, pm = mlstm_prompt(proj, gc, gr, mh, batch=bp, seq=sp, **mkw)
            n0_tok = jnp.repeat(jnp.transpose(state_mlstm_n[ie], (1, 0, 2)), ss, axis=1)
            m0_tok = jnp.broadcast_to(jnp.repeat(state_mlstm_m[ie].T, ss, axis=1)[:, :, None], (ah, ts, MLSTM_CHUNK))
            ya_s, sc_, sn_tok, sm_tok = mlstm_sample(proj, gc, gr, mh, state_mlstm_c[ie:ie + 1], n0_tok, m0_tok,
                                                     row0=tp, n_seq=bs, seg=ss, **mkw)
            outs.setdefault("pc", []).append(pc.reshape(bp, ah, dk, dv))
            outs.setdefault("pn", []).append(pn.reshape(bp, ah, dk))
            outs.setdefault("pm", []).append(pm.reshape(bp, ah))
            outs.setdefault("sc", []).append(sc_[0])
            outs.setdefault("sn", []).append(jnp.transpose(sn_tok[:, ss - 1::ss, :], (1, 0, 2)))
            outs.setdefault("sm", []).append(sm_tok[:, ss - 1::ss, 0].T)

            cw = conv_b_w[ie]
            cbias = conv_b_b[ie].reshape(1, bw)
            wa = lru_wa[ie].astype(BF16)
            wi = lru_wi[ie].astype(BF16)
            ba = lru_ba[ie].reshape(1, bw)
            bi = lru_bi[ie].reshape(1, bw)
            lam = lru_lam[ie].reshape(1, bw)
            halo_p = jnp.zeros((bp, V7X_SUBLANES, bw), F32)
            h0_p = jnp.zeros((bp, 1, bw), F32)
            yb_p, ph = lru_prompt(proj_b, halo_p, h0_p, cw, cbias, wa, wi, ba, bi, lam,
                                  batch=bp, seq=sp, width=bw, col_x=col_x, col_g=col_g)
            st_s = jnp.transpose(state_lru_conv[ie], (1, 0, 2))
            xb_p = proj_b[:tp, col_x:col_x + bw].reshape(bp, sp, bw)
            xb_s = proj_b[tp:, col_x:col_x + bw].reshape(bs, ss, bw)
            gb_s = proj_b[tp:, col_g:col_g + bw].reshape(bs, ss, bw)
            yb_s, sh = lru_sample(jnp.transpose(xb_s, (1, 0, 2)), jnp.transpose(gb_s, (1, 0, 2)), st_s,
                                  state_lru_h[ie], cw, cbias, wa, wi, ba, bi, lam)
            yb_s = jnp.transpose(yb_s, (1, 0, 2)).reshape(ts, bw)
            outs.setdefault("ph", []).append(ph.reshape(bp, bw))
            outs.setdefault("pcb", []).append(xb_p[:, sp - (bconv - 1):])
            outs.setdefault("sh", []).append(sh)
            outs.setdefault("scb", []).append(jnp.concatenate([state_lru_conv[ie], xb_s], axis=1)[:, ss:])

            ya = place_rows(ya_p, ya_s, tp)
            yb = place_rows(yb_p, yb_s, tp)
            x = matmul([ya, yb], w_out_e, layer=ie, res=x, tm=TM_RES, tn=TN, name="out_proj_even")
            ie += 1
        else:
            proj = matmul([u], w_in_o, layer=io, tm=TM, tn=TN, name="in_proj_odd")
            cw = conv_c_w[io]
            halo_p = jnp.zeros((bp, V7X_SUBLANES, cwid), F32)
            y_p, tail = sconv_prompt(proj, halo_p, cw, batch=bp, seq=sp, width=cwid)
            st_s = jnp.transpose(state_sconv[io], (1, 0, 2))
            proj_t = jnp.transpose(proj[tp:].reshape(bs, ss, 3 * cwid), (1, 0, 2))
            y_s, ps = sconv_sample(proj_t, st_s, cw, width=cwid)
            y_s = jnp.transpose(y_s, (1, 0, 2)).reshape(ts, cwid)
            outs.setdefault("psb", []).append(tail[:, V7X_SUBLANES - (cconv - 1):])
            ps_b = jnp.transpose(ps, (1, 0, 2))
            outs.setdefault("ssb", []).append(jnp.concatenate([state_sconv[io], ps_b], axis=1)[:, ss:])
            y = place_rows(y_p, y_s, tp)
            x = matmul([y], w_out_o, layer=io, res=x, tm=TM_RES, tn=TN, name="out_proj_odd")
            io += 1

        u = rmsnorm(x, norm_x[l])
        q = matmul([u], w_xq, layer=l, out_dtype=BF16, tm=TM, tn=TN, name="xattn_q")
        mk = matmul([mem_bf], w_xk, layer=l, tm=TM_MEM, tn=TN, name="mem_k")
        mv = matmul([mem_bf], w_xv, layer=l, tm=TM_MEM, tn=TN, name="mem_v")
        p_mem_k.append(mk.reshape(bp, n_mem, xh, xhd))
        p_mem_v.append(mv.reshape(bp, n_mem, xh, xhd))
        o_p = xattn_prompt(q, mk, mv, batch=bp, seq=sp, n_mem=n_mem, n_heads=xh, hd=xhd)
        o_s = xattn_sample(q[tp:].reshape(bs, ss * xh, xhd), cache_mem_k, cache_mem_v, l)
        o = place_rows(o_p, o_s.reshape(ts, d), tp)
        x = matmul([o], w_xo, layer=l, res=x, tm=TM_RES, tn=TN, name="xattn_o")

        u = rmsnorm(x, norm_ff[l])
        hmid = matmul([u], w_ff1, layer=l, act="relu2", out_dtype=BF16, tm=TM, tn=TN, name="ff1")
        x = matmul_ksplit(hmid, w_ff2, l, x, tm=TM, tn=TN, tk=TK2, name="ff2")

    y_p = rmsnorm(x, norm_final, out_dtype=F32, row0=0, nrows=tp)
    y_s = rmsnorm(x, norm_final, out_dtype=F32, row0=tp, nrows=ts)
    dt = x_prompt.dtype
    st = lambda name: jnp.stack(outs[name]).astype(dt)
    return (y_p.reshape(bp, sp, d), y_s.reshape(bs, ss, d),
            st("pc"), st("pn"), st("pm"), st("ph"), st("pcb"), st("psb"),
            jnp.stack(p_mem_k), jnp.stack(p_mem_v),
            st("sc"), st("sn"), st("sm"), st("sh"), st("scb"), st("ssb"))
```

```python
import functools

import jax
import jax.numpy as jnp
from jax import lax
from jax.experimental import pallas as pl
from jax.experimental.pallas import tpu as pltpu

F32 = jnp.float32
BF16 = jnp.bfloat16
EPS = 1e-6
LRU_C = 8.0

V7X_LANES = 128
V7X_SUBLANES = 8
V7X_VMEM_BUDGET_BYTES = 60 * 1024 * 1024

MLSTM_CHUNK = 128
SCAN_CHUNK = 256


def _cparams(*semantics):
    return pltpu.CompilerParams(dimension_semantics=semantics, vmem_limit_bytes=V7X_VMEM_BUDGET_BYTES)


def _log_sigmoid(x):
    return jnp.minimum(x, 0.0) - jnp.log1p(jnp.exp(-jnp.abs(x)))


def _sigmoid(x):
    return 1.0 / (1.0 + jnp.exp(-x))


def _gelu_tanh(x):
    c = 0.7978845608028654
    return 0.5 * x * (1.0 + jnp.tanh(c * (x + 0.044715 * (x * x * x))))


def _expm1(x):
    u = jnp.exp(x)
    um1 = u - 1.0
    safe = jnp.where(um1 == 0.0, 1.0, jnp.log(u))
    return jnp.where(um1 == 0.0, x, jnp.where(um1 == -1.0, -1.0, um1 * x / safe))


def _split3(x):
    hi = x.astype(BF16)
    r1 = x - hi.astype(F32)
    mid = r1.astype(BF16)
    lo = (r1 - mid.astype(F32)).astype(BF16)
    return hi, mid, lo


def _dot01(mask_bf16, x):
    hi, mid, lo = _split3(x)
    acc = jnp.dot(mask_bf16, lo, preferred_element_type=F32)
    acc = acc + jnp.dot(mask_bf16, mid, preferred_element_type=F32)
    return acc + jnp.dot(mask_bf16, hi, preferred_element_type=F32)


def _rmsnorm_body(x_ref, g_ref, o_ref):
    x = x_ref[...]
    inv = lax.rsqrt(jnp.mean(x * x, axis=-1, keepdims=True) + EPS)
    o_ref[...] = ((x * inv) * g_ref[...]).astype(o_ref.dtype)


def rmsnorm(x, g, out_dtype=BF16, rows=256, row0=0, nrows=None):
    d = x.shape[1]
    nrows = x.shape[0] if nrows is None else nrows
    b0 = row0 // rows
    return pl.pallas_call(
        _rmsnorm_body,
        grid=(nrows // rows,),
        in_specs=[pl.BlockSpec((rows, d), lambda i: (b0 + i, 0)), pl.BlockSpec((1, d), lambda i: (0, 0))],
        out_specs=pl.BlockSpec((rows, d), lambda i: (i, 0)),
        out_shape=jax.ShapeDtypeStruct((nrows, d), out_dtype),
        compiler_params=_cparams("parallel"),
        name="rmsnorm",
    )(x, g.reshape(1, d))


def _place_rows_body(buf_ref, rows_ref, o_ref):
    del buf_ref
    o_ref[...] = rows_ref[...]


def place_rows(buf, rows, row0):
    r, c = rows.shape
    assert row0 % r == 0 and buf.shape[1] == c and buf.dtype == rows.dtype
    return pl.pallas_call(
        _place_rows_body,
        grid=(1,),
        in_specs=[pl.BlockSpec(memory_space=pl.ANY), pl.BlockSpec((r, c), lambda i: (0, 0))],
        out_specs=pl.BlockSpec((r, c), lambda i: (row0 // r, 0)),
        out_shape=jax.ShapeDtypeStruct(buf.shape, buf.dtype),
        input_output_aliases={0: 0},
        compiler_params=_cparams("arbitrary"),
        name="place_rows",
    )(buf, rows)


def _mm_body(*refs, n_in, has_res, act, kc, nj, w_t):
    x_refs = refs[:n_in]
    w_ref = refs[n_in]
    res_ref = refs[n_in + 1] if has_res else None
    n_io = n_in + 1 + (1 if has_res else 0)
    o_ref, wbf = refs[n_io], refs[n_io + 1]
    p, i = pl.program_id(0), pl.program_id(1)

    @pl.when(p > 0)
    def _():
        slot = (p - 1) % 2
        k0 = 0
        acc = None
        for x_ref in x_refs:
            kw = x_ref.shape[1]
            if w_t:
                part = lax.dot_general(x_ref[...], wbf[slot, :, k0:k0 + kw], (((1,), (1,)), ((), ())),
                                       preferred_element_type=F32)
            else:
                part = jnp.dot(x_ref[...], wbf[slot, k0:k0 + kw, :], preferred_element_type=F32)
            acc = part if acc is None else acc + part
            k0 += kw
        if act == "relu2":
            r = jnp.maximum(acc, 0.0)
            acc = r * r
        if has_res:
            acc = res_ref[...] + acc
        o_ref[...] = acc.astype(o_ref.dtype)

    @pl.when(p < nj)
    def _():
        chunk = pl.ds(pl.multiple_of(i * kc, kc), kc)
        if w_t:
            wbf[p % 2, :, chunk] = w_ref[...].astype(BF16)
        else:
            wbf[p % 2, chunk, :] = w_ref[...].astype(BF16)


def matmul(xs, w, *, layer=None, n_out=None, w_t=False, res=None, act=None, out_dtype=F32, tm, tn, name):
    m = xs[0].shape[0]
    kk = sum(x.shape[1] for x in xs)
    w_k, w_n = (w.shape[-1], w.shape[-2]) if w_t else (w.shape[-2], w.shape[-1])
    n = w_n if n_out is None else n_out
    ni, nj = m // tm, n // tn
    kc = kk // ni
    assert w_k == kk and kk % ni == 0 and kc % V7X_LANES == 0 and n % tn == 0 and m % tm == 0
    n_in = len(xs)

    def row(p, i):
        return jnp.where(p > 0, i, 0)

    def w_idx(p, i):
        blk = (jnp.where(p < nj, i, ni - 1), jnp.minimum(p, nj - 1))
        blk = blk[::-1] if w_t else blk
        return blk if layer is None else (layer,) + blk

    w_blk = (tn, kc) if w_t else (kc, tn)
    in_specs = [pl.BlockSpec((tm, x.shape[1]), lambda p, i: (row(p, i), 0)) for x in xs]
    in_specs.append(pl.BlockSpec(w_blk if layer is None else (None,) + w_blk, w_idx))
    args = list(xs) + [w]
    out_idx = lambda p, i: (row(p, i), jnp.maximum(p - 1, 0))
    if res is not None:
        in_specs.append(pl.BlockSpec((tm, tn), out_idx))
        args.append(res)
    return pl.pallas_call(
        functools.partial(_mm_body, n_in=n_in, has_res=res is not None, act=act, kc=kc, nj=nj, w_t=w_t),
        grid=(nj + 1, ni),
        in_specs=in_specs,
        out_specs=pl.BlockSpec((tm, tn), out_idx),
        out_shape=jax.ShapeDtypeStruct((m, n), out_dtype),
        scratch_shapes=[pltpu.VMEM((2, tn, kk) if w_t else (2, kk, tn), BF16)],
        compiler_params=_cparams("arbitrary", "arbitrary"),
        name=name,
    )(*args)


def _mm_acc_body(x_ref, w_ref, res_ref, o_ref):
    k = pl.program_id(2)

    @pl.when(k == 0)
    def _():
        o_ref[...] = res_ref[...] + jnp.dot(x_ref[...], w_ref[...], preferred_element_type=F32)

    @pl.when(k != 0)
    def _():
        o_ref[...] += jnp.dot(x_ref[...], w_ref[...], preferred_element_type=F32)


def matmul_ksplit(x, w, layer, res, *, tm, tn, tk, name):
    m, kk = x.shape
    n = w.shape[-1]
    return pl.pallas_call(
        _mm_acc_body,
        grid=(m // tm, n // tn, kk // tk),
        in_specs=[
            pl.BlockSpec((tm, tk), lambda i, j, k: (i, k)),
            pl.BlockSpec((None, tk, tn), lambda i, j, k: (layer, k, j)),
            pl.BlockSpec((tm, tn), lambda i, j, k: (i, j)),
        ],
        out_specs=pl.BlockSpec((tm, tn), lambda i, j, k: (i, j)),
        out_shape=jax.ShapeDtypeStruct((m, n), F32),
        compiler_params=_cparams("parallel", "parallel", "arbitrary"),
        name=name,
    )(x, w, res)


def _gates_body(u_ref, w_ref, b_ref, gc_ref, gr_ref, *, tm, n_prompt_tiles, seg_sample):
    L = MLSTM_CHUNK
    g = jnp.dot(u_ref[...], w_ref[...], preferred_element_type=F32) + b_ref[...]
    lane = lax.broadcasted_iota(jnp.int32, (tm, V7X_LANES), 1)
    val = jnp.where(lane < 8, g, _log_sigmoid(g))
    row = lax.broadcasted_iota(jnp.int32, (L, L), 0)
    col = lax.broadcasted_iota(jnp.int32, (L, L), 1)
    is_sample = pl.program_id(0) >= n_prompt_tiles
    tril = jnp.where(col <= row, 1.0, 0.0)
    tril_seg = jnp.where((col <= row) & ((row // seg_sample) == (col // seg_sample)), 1.0, 0.0)
    mask = jnp.where(is_sample, tril_seg, tril).astype(BF16)
    lane_l = lax.broadcasted_iota(jnp.int32, (L, V7X_LANES), 1)
    for s in range(tm // L):
        x = val[s * L:(s + 1) * L]
        out = jnp.where(lane_l < 8, x, _dot01(mask, x))
        gr_ref[:, s * L:(s + 1) * L] = out.T[:16]
        for c in range(16):
            gc_ref[c, s * L:(s + 1) * L, :] = jnp.broadcast_to(out[:, c:c + 1], (L, V7X_LANES))


def mlstm_gates(u, w_if, b_if, *, n_prompt_rows, seg_sample, tm=512):
    t, d = u.shape
    return pl.pallas_call(
        functools.partial(_gates_body, tm=tm, n_prompt_tiles=n_prompt_rows // tm, seg_sample=seg_sample),
        grid=(t // tm,),
        in_specs=[
            pl.BlockSpec((tm, d), lambda i: (i, 0)),
            pl.BlockSpec((d, V7X_LANES), lambda i: (0, 0)),
            pl.BlockSpec((1, V7X_LANES), lambda i: (0, 0)),
        ],
        out_specs=[
            pl.BlockSpec((16, tm, V7X_LANES), lambda i: (0, i, 0)),
            pl.BlockSpec((16, tm), lambda i: (0, i)),
        ],
        out_shape=[
            jax.ShapeDtypeStruct((16, t, V7X_LANES), F32),
            jax.ShapeDtypeStruct((16, t), F32),
        ],
        compiler_params=_cparams("parallel"),
        name="mlstm_gates",
    )(u, w_if, b_if)


def _head_out(hh, o, mh):
    hn = hh * lax.rsqrt(jnp.mean(hh * hh, axis=-1, keepdims=True) + EPS)
    return _sigmoid(o) * (hn * mh)


def _mlstm_prompt_body(q_ref, k_ref, v_ref, o_ref, ic_ref, bc_ref, ir_ref, br_ref, mh_ref,
                       ya_ref, c_ref, n_ref, m_ref, c_scr, n_scr, m_scr, *, n_heads, seq, scale):
    L = MLSTM_CHUNK
    c_scr[...] = jnp.zeros_like(c_scr)
    n_scr[...] = jnp.zeros_like(n_scr)
    m_scr[...] = jnp.zeros_like(m_scr)
    row = lax.broadcasted_iota(jnp.int32, (L, L), 0)
    col = lax.broadcasted_iota(jnp.int32, (L, L), 1)
    causal = col <= row

    def chunk(ci, carry):
        r0 = pl.multiple_of(ci * L, L)
        q = q_ref[pl.ds(r0, L), :] * scale
        k = k_ref[pl.ds(r0, L), :]
        v = v_ref[pl.ds(r0, L), :]
        bc = bc_ref[0, pl.ds(r0, L), :]
        ic = ic_ref[0, pl.ds(r0, L), :]
        i_row = ir_ref[0, :, pl.ds(r0, L)]
        b_row = br_ref[0, :, pl.ds(r0, L)]
        m_prev = m_scr[...]
        c_prev = c_scr[...]
        n_prev = n_scr[...]
        dmat = jnp.where(causal, bc - b_row + i_row, -jnp.inf)
        g = bc + m_prev
        mt = jnp.maximum(g, jnp.max(dmat, axis=-1, keepdims=True))
        w_inter = jnp.exp(g - mt)
        qb = q.astype(BF16)
        kb = k.astype(BF16)
        vb = v.astype(BF16)
        sc = lax.dot_general(qb, kb, (((1,), (1,)), ((), ())), preferred_element_type=F32) * jnp.exp(dmat - mt)
        inter = jnp.dot(qb, c_prev.astype(BF16), preferred_element_type=F32)
        num = jnp.concatenate([w_inter, w_inter], axis=1) * inter + jnp.dot(sc.astype(BF16), vb, preferred_element_type=F32)
        qn = jnp.sum(q * n_prev, axis=-1, keepdims=True)
        den = w_inter[:, :1] * qn + jnp.sum(sc, axis=-1, keepdims=True)
        hh = num / jnp.maximum(jnp.abs(den), jnp.exp(-mt[:, :1]))
        ya_ref[pl.ds(r0, L), :] = _head_out(hh, o_ref[pl.ds(r0, L), :], mh_ref[...]).astype(ya_ref.dtype)
        m_new = mt[L - 1:L, :1]
        b_last = b_row[:, L - 1:L]
        decay = jnp.exp(b_last + m_prev - m_new)
        kw = k * jnp.exp(b_last - bc + ic - m_new)
        c_scr[...] = decay * c_prev + lax.dot_general(kw.astype(BF16), vb, (((0,), (0,)), ((), ())), preferred_element_type=F32)
        n_scr[...] = decay * n_prev + jnp.sum(kw, axis=0, keepdims=True)
        m_scr[...] = m_new
        return carry

    lax.fori_loop(0, seq // L, chunk, 0)
    c_ref[0, 0] = c_scr[...]
    n_ref[0, 0] = n_scr[...]
    m_ref[0, 0] = m_scr[...]


def mlstm_prompt(proj, gc, gr, mh_norm, *, batch, seq, n_heads, dk, dv, col_q, col_k, col_v, col_o):
    L = MLSTM_CHUNK
    assert dk == L and dv == 2 * L
    rows = proj.shape[0]
    return pl.pallas_call(
        functools.partial(_mlstm_prompt_body, n_heads=n_heads, seq=seq, scale=dk ** -0.5),
        grid=(batch, n_heads),
        in_specs=[
            pl.BlockSpec((seq, dk), lambda b, h: (b, col_q // dk + h)),
            pl.BlockSpec((seq, dk), lambda b, h: (b, col_k // dk + h)),
            pl.BlockSpec((seq, dv), lambda b, h: (b, col_v // dv + h)),
            pl.BlockSpec((seq, dv), lambda b, h: (b, col_o // dv + h)),
            pl.BlockSpec((1, seq, V7X_LANES), lambda b, h: (h, b, 0)),
            pl.BlockSpec((1, seq, V7X_LANES), lambda b, h: (h + n_heads, b, 0)),
            pl.BlockSpec((1, 1, seq), lambda b, h: (h, 0, b)),
            pl.BlockSpec((1, 1, seq), lambda b, h: (h + n_heads, 0, b)),
            pl.BlockSpec((1, dv), lambda b, h: (0, h)),
        ],
        out_specs=[
            pl.BlockSpec((seq, dv), lambda b, h: (b, h)),
            pl.BlockSpec((1, 1, dk, dv), lambda b, h: (b, h, 0, 0)),
            pl.BlockSpec((1, 1, 1, dk), lambda b, h: (b, h, 0, 0)),
            pl.BlockSpec((1, 1, 1, 1), lambda b, h: (b, h, 0, 0)),
        ],
        out_shape=[
            jax.ShapeDtypeStruct((rows, n_heads * dv), BF16),
            jax.ShapeDtypeStruct((batch, n_heads, dk, dv), F32),
            jax.ShapeDtypeStruct((batch, n_heads, 1, dk), F32),
            jax.ShapeDtypeStruct((batch, n_heads, 1, 1), F32),
        ],
        scratch_shapes=[pltpu.VMEM((dk, dv), F32), pltpu.VMEM((1, dk), F32), pltpu.VMEM((1, 1), F32)],
        compiler_params=_cparams("parallel", "parallel"),
        name="mlstm_prompt",
    )(proj, proj, proj, proj, gc, gc, gr, gr, mh_norm)


def _mlstm_sample_body(q_ref, k_ref, v_ref, o_ref, ic_ref, bc_ref, ir_ref, br_ref, mh_ref, c0_ref, n0_ref, m0_ref,
                       ya_ref, c_ref, n_ref, m_ref, inter_scr, *, n_heads, seg, scale):
    L = MLSTM_CHUNK
    nb = L // seg
    row = lax.broadcasted_iota(jnp.int32, (L, L), 0)
    col = lax.broadcasted_iota(jnp.int32, (L, L), 1)
    same = (row // seg) == (col // seg)
    causal = (col <= row) & same
    last = col == (row // seg) * seg + (seg - 1)
    q = q_ref[...] * scale
    k = k_ref[...]
    v = v_ref[...]
    bc = bc_ref[0]
    ic = ic_ref[0]
    i_row = ir_ref[0]
    b_row = br_ref[0]
    m_prev = m0_ref[0]
    n_prev = n0_ref[0]
    dmat = jnp.where(causal, bc - b_row + i_row, -jnp.inf)
    g = bc + m_prev
    mt = jnp.maximum(g, jnp.max(dmat, axis=-1, keepdims=True))
    w_inter = jnp.exp(g - mt)
    qb = q.astype(BF16)
    kb = k.astype(BF16)
    vb = v.astype(BF16)
    sc = lax.dot_general(qb, kb, (((1,), (1,)), ((), ())), preferred_element_type=F32) * jnp.exp(dmat - mt)

    per_group = V7X_SUBLANES // seg
    sub = lax.broadcasted_iota(jnp.int32, (V7X_SUBLANES, 2 * L), 0)
    for gi in range(L // V7X_SUBLANES):
        q8 = qb[gi * V7X_SUBLANES:(gi + 1) * V7X_SUBLANES]
        acc = jnp.zeros((V7X_SUBLANES, 2 * L), F32)
        for j in range(per_group):
            r = jnp.dot(q8, c0_ref[0, gi * per_group + j, 0].astype(BF16), preferred_element_type=F32)
            acc = jnp.where((sub // seg) == j, r, acc)
        inter_scr[gi * V7X_SUBLANES:(gi + 1) * V7X_SUBLANES, :] = acc
    inter = inter_scr[...]

    num = jnp.concatenate([w_inter, w_inter], axis=1) * inter + jnp.dot(sc.astype(BF16), vb, preferred_element_type=F32)
    qn = jnp.sum(q * n_prev, axis=-1, keepdims=True)
    den = w_inter[:, :1] * qn + jnp.sum(sc, axis=-1, keepdims=True)
    hh = num / jnp.maximum(jnp.abs(den), jnp.exp(-mt[:, :1]))
    ya_ref[...] = _head_out(hh, o_ref[...], mh_ref[...]).astype(ya_ref.dtype)

    b_last = jnp.sum(jnp.where(last, jnp.broadcast_to(b_row, (L, L)), 0.0), axis=-1, keepdims=True)
    mt_row = mt.T
    m_new = jnp.sum(jnp.where(last, mt_row, 0.0), axis=-1, keepdims=True)
    decay = jnp.exp(b_last + m_prev - m_new)
    kw = k * jnp.exp(b_last - bc + ic - m_new)
    seg_ones = jnp.where(same, 1.0, 0.0).astype(BF16)
    n_ref[0] = decay * n_prev + _dot01(seg_ones, kw)
    m_ref[0] = jnp.broadcast_to(m_new, (L, L))
    rowk = lax.broadcasted_iota(jnp.int32, (L, L), 0)
    for bi in range(nb):
        kw_b = jnp.where((rowk // seg) == bi, kw, 0.0).astype(BF16)
        upd = lax.dot_general(kw_b, vb, (((0,), (0,)), ((), ())), preferred_element_type=F32)
        d_b = decay[bi * seg:bi * seg + 1, :1]
        c_ref[0, bi, 0] = d_b * c0_ref[0, bi, 0] + upd


def mlstm_sample(proj, gc, gr, mh_norm, c0, n0_tok, m0_tok, *, row0, n_seq, seg, n_heads, dk, dv,
                 col_q, col_k, col_v, col_o):
    L = MLSTM_CHUNK
    assert dk == L and dv == 2 * L and V7X_SUBLANES % seg == 0 and row0 % L == 0
    rows = n_seq * seg
    nb = L // seg
    t0 = row0 // L
    return pl.pallas_call(
        functools.partial(_mlstm_sample_body, n_heads=n_heads, seg=seg, scale=dk ** -0.5),
        grid=(rows // L, n_heads),
        in_specs=[
            pl.BlockSpec((L, dk), lambda j, h: (t0 + j, col_q // dk + h)),
            pl.BlockSpec((L, dk), lambda j, h: (t0 + j, col_k // dk + h)),
            pl.BlockSpec((L, dv), lambda j, h: (t0 + j, col_v // dv + h)),
            pl.BlockSpec((L, dv), lambda j, h: (t0 + j, col_o // dv + h)),
            pl.BlockSpec((1, L, V7X_LANES), lambda j, h: (h, t0 + j, 0)),
            pl.BlockSpec((1, L, V7X_LANES), lambda j, h: (h + n_heads, t0 + j, 0)),
            pl.BlockSpec((1, 1, L), lambda j, h: (h, 0, t0 + j)),
            pl.BlockSpec((1, 1, L), lambda j, h: (h + n_heads, 0, t0 + j)),
            pl.BlockSpec((1, dv), lambda j, h: (0, h)),
            pl.BlockSpec((1, nb, 1, dk, dv), lambda j, h: (0, j, h, 0, 0)),
            pl.BlockSpec((1, L, dk), lambda j, h: (h, j, 0)),
            pl.BlockSpec((1, L, L), lambda j, h: (h, j, 0)),
        ],
        out_specs=[
            pl.BlockSpec((L, dv), lambda j, h: (j, h)),
            pl.BlockSpec((1, nb, 1, dk, dv), lambda j, h: (0, j, h, 0, 0)),
            pl.BlockSpec((1, L, dk), lambda j, h: (h, j, 0)),
            pl.BlockSpec((1, L, L), lambda j, h: (h, j, 0)),
        ],
        out_shape=[
            jax.ShapeDtypeStruct((rows, n_heads * dv), BF16),
            jax.ShapeDtypeStruct((1, n_seq, n_heads, dk, dv), F32),
            jax.ShapeDtypeStruct((n_heads, rows, dk), F32),
            jax.ShapeDtypeStruct((n_heads, rows, L), F32),
        ],
        scratch_shapes=[pltpu.VMEM((L, dv), F32)],
        compiler_params=_cparams("parallel", "parallel"),
        name="mlstm_sample",
    )(proj, proj, proj, proj, gc, gc, gr, gr, mh_norm, c0, n0_tok, m0_tok)


def _lru_gates(xc, wa_ref, wi_ref, ba, bi, lam):
    nblk = wa_ref.shape[0]
    blk = wa_ref.shape[1]
    rs, igs = [], []
    for n in range(nblk):
        xb = xc[:, n * blk:(n + 1) * blk].astype(BF16)
        rs.append(jnp.dot(xb, wa_ref[n], preferred_element_type=F32))
        igs.append(jnp.dot(xb, wi_ref[n], preferred_element_type=F32))
    r = _sigmoid(jnp.concatenate(rs, axis=1) + ba)
    ig = _sigmoid(jnp.concatenate(igs, axis=1) + bi)
    log_a = LRU_C * r * _log_sigmoid(lam)
    a = jnp.exp(log_a)
    u = jnp.sqrt(-_expm1(2.0 * log_a)) * (ig * xc)
    return a, u


def _lru_prompt_body(x_ref, g_ref, halo_ref, h0_ref, cw_ref, cb_ref, wa_ref, wi_ref, ba_ref, bi_ref, lam_ref,
                     y_ref, hl_ref, tail_ref, h_scr, *, seq):
    Lc = SCAN_CHUNK
    cbw = x_ref.shape[1]
    h_scr[...] = h0_ref[0]
    row = lax.broadcasted_iota(jnp.int32, (Lc, cbw), 0)
    cw = cw_ref[...]
    width = cw.shape[0]

    def chunk(ci, carry):
        r0 = pl.multiple_of(ci * Lc, Lc)
        x = x_ref[pl.ds(r0, Lc), :]
        prev = x_ref[pl.ds(pl.multiple_of(jnp.maximum(r0 - V7X_SUBLANES, 0), V7X_SUBLANES), V7X_SUBLANES), :]
        prev = jnp.where(ci == 0, halo_ref[0], prev)
        xfull = jnp.concatenate([prev, x], axis=0)
        xc = None
        for j in range(width - 1, 0, -1):
            term = pltpu.roll(xfull, j, 0)[V7X_SUBLANES:] * cw[width - 1 - j:width - j]
            xc = term if xc is None else xc + term
        xc = xc + x * cw[width - 1:width] + cb_ref[...]
        a, u = _lru_gates(xc, wa_ref, wi_ref, ba_ref[...], bi_ref[...], lam_ref[...])
        d = 1
        while d < Lc:
            ok = row >= d
            a_sh = jnp.where(ok, pltpu.roll(a, d, 0), 1.0)
            u_sh = jnp.where(ok, pltpu.roll(u, d, 0), 0.0)
            u = a * u_sh + u
            a = a * a_sh
            d *= 2
        hs = u + a * h_scr[...]
        h_scr[...] = hs[Lc - 1:Lc]
        y_ref[pl.ds(r0, Lc), :] = (hs * _gelu_tanh(g_ref[pl.ds(r0, Lc), :])).astype(y_ref.dtype)
        return carry

    lax.fori_loop(0, seq // Lc, chunk, 0)
    hl_ref[0] = h_scr[...]
    tail_ref[0] = x_ref[pl.ds(seq - V7X_SUBLANES, V7X_SUBLANES), :]


def lru_prompt(proj, halo, h0, cw, cb, wa, wi, ba, bi, lam, *, batch, seq, width, col_x, col_g, cbw=256):
    blk = wa.shape[1]
    nper = cbw // blk
    vec = lambda b, c: (0, c)
    return pl.pallas_call(
        functools.partial(_lru_prompt_body, seq=seq),
        grid=(batch, width // cbw),
        in_specs=[
            pl.BlockSpec((seq, cbw), lambda b, c: (b, col_x // cbw + c)),
            pl.BlockSpec((seq, cbw), lambda b, c: (b, col_g // cbw + c)),
            pl.BlockSpec((1, V7X_SUBLANES, cbw), lambda b, c: (b, 0, c)),
            pl.BlockSpec((1, 1, cbw), lambda b, c: (b, 0, c)),
            pl.BlockSpec((cw.shape[0], cbw), vec),
            pl.BlockSpec((1, cbw), vec),
            pl.BlockSpec((nper, blk, blk), lambda b, c: (c, 0, 0)),
            pl.BlockSpec((nper, blk, blk), lambda b, c: (c, 0, 0)),
            pl.BlockSpec((1, cbw), vec),
            pl.BlockSpec((1, cbw), vec),
            pl.BlockSpec((1, cbw), vec),
        ],
        out_specs=[
            pl.BlockSpec((seq, cbw), lambda b, c: (b, c)),
            pl.BlockSpec((1, 1, cbw), lambda b, c: (b, 0, c)),
            pl.BlockSpec((1, V7X_SUBLANES, cbw), lambda b, c: (b, 0, c)),
        ],
        out_shape=[
            jax.ShapeDtypeStruct((proj.shape[0], width), BF16),
            jax.ShapeDtypeStruct((batch, 1, width), F32),
            jax.ShapeDtypeStruct((batch, V7X_SUBLANES, width), F32),
        ],
        scratch_shapes=[pltpu.VMEM((1, cbw), F32)],
        compiler_params=_cparams("parallel", "parallel"),
        name="lru_prompt",
    )(proj, proj, halo, h0, cw, cb, wa, wi, ba, bi, lam)


def _lru_sample_body(x_ref, g_ref, st_ref, h0_ref, cw_ref, cb_ref, wa_ref, wi_ref, ba_ref, bi_ref, lam_ref,
                     y_ref, hl_ref, *, seg):
    cw = cw_ref[...]
    width = cw.shape[0]
    hist = [st_ref[j] for j in range(width - 1)]
    h = h0_ref[...]
    for t in range(seg):
        x = x_ref[t]
        taps = hist + [x]
        xc = taps[0] * cw[0:1]
        for j in range(1, width):
            xc = xc + taps[j] * cw[j:j + 1]
        xc = xc + cb_ref[...]
        a, u = _lru_gates(xc, wa_ref, wi_ref, ba_ref[...], bi_ref[...], lam_ref[...])
        h = a * h + u
        y_ref[t] = (h * _gelu_tanh(g_ref[t])).astype(y_ref.dtype)
        hist = hist[1:] + [x]
    hl_ref[...] = h


def lru_sample(x_t, g_t, st, h0, cw, cb, wa, wi, ba, bi, lam, *, cbw=256):
    seg, n_seq, width = x_t.shape
    blk = wa.shape[1]
    nper = cbw // blk
    vec = lambda c: (0, c)
    return pl.pallas_call(
        functools.partial(_lru_sample_body, seg=seg),
        grid=(width // cbw,),
        in_specs=[
            pl.BlockSpec((seg, n_seq, cbw), lambda c: (0, 0, c)),
            pl.BlockSpec((seg, n_seq, cbw), lambda c: (0, 0, c)),
            pl.BlockSpec((st.shape[0], n_seq, cbw), lambda c: (0, 0, c)),
            pl.BlockSpec((n_seq, cbw), vec),
            pl.BlockSpec((cw.shape[0], cbw), vec),
            pl.BlockSpec((1, cbw), vec),
            pl.BlockSpec((nper, blk, blk), lambda c: (c, 0, 0)),
            pl.BlockSpec((nper, blk, blk), lambda c: (c, 0, 0)),
            pl.BlockSpec((1, cbw), vec),
            pl.BlockSpec((1, cbw), vec),
            pl.BlockSpec((1, cbw), vec),
        ],
        out_specs=[
            pl.BlockSpec((seg, n_seq, cbw), lambda c: (0, 0, c)),
            pl.BlockSpec((n_seq, cbw), vec),
        ],
        out_shape=[
            jax.ShapeDtypeStruct((seg, n_seq, width), BF16),
            jax.ShapeDtypeStruct((n_seq, width), F32),
        ],
        compiler_params=_cparams("parallel"),
        name="lru_sample",
    )(x_t, g_t, st, h0, cw, cb, wa, wi, ba, bi, lam)


def _sconv_prompt_body(bg_ref, cg_ref, hx_ref, halo_ref, cw_ref, y_ref, tail_ref, *, seq):
    Lc = SCAN_CHUNK
    cw = cw_ref[...]
    width = cw.shape[0]

    def chunk(ci, carry):
        r0 = pl.multiple_of(ci * Lc, Lc)
        p = cg_ref[pl.ds(r0, Lc), :] * hx_ref[pl.ds(r0, Lc), :]
        rp = pl.multiple_of(jnp.maximum(r0 - V7X_SUBLANES, 0), V7X_SUBLANES)
        prev = cg_ref[pl.ds(rp, V7X_SUBLANES), :] * hx_ref[pl.ds(rp, V7X_SUBLANES), :]
        prev = jnp.where(ci == 0, halo_ref[0], prev)
        pfull = jnp.concatenate([prev, p], axis=0)
        z = None
        for j in range(width - 1, 0, -1):
            term = pltpu.roll(pfull, j, 0)[V7X_SUBLANES:] * cw[width - 1 - j:width - j]
            z = term if z is None else z + term
        z = z + p * cw[width - 1:width]
        y_ref[pl.ds(r0, Lc), :] = (bg_ref[pl.ds(r0, Lc), :] * z).astype(y_ref.dtype)
        return carry

    lax.fori_loop(0, seq // Lc, chunk, 0)
    rt = seq - V7X_SUBLANES
    tail_ref[0] = cg_ref[pl.ds(rt, V7X_SUBLANES), :] * hx_ref[pl.ds(rt, V7X_SUBLANES), :]


def sconv_prompt(proj, halo, cw, *, batch, seq, width, cbw=512):
    nb = width // cbw
    return pl.pallas_call(
        functools.partial(_sconv_prompt_body, seq=seq),
        grid=(batch, nb),
        in_specs=[
            pl.BlockSpec((seq, cbw), lambda b, c: (b, c)),
            pl.BlockSpec((seq, cbw), lambda b, c: (b, nb + c)),
            pl.BlockSpec((seq, cbw), lambda b, c: (b, 2 * nb + c)),
            pl.BlockSpec((1, V7X_SUBLANES, cbw), lambda b, c: (b, 0, c)),
            pl.BlockSpec((cw.shape[0], cbw), lambda b, c: (0, c)),
        ],
        out_specs=[
            pl.BlockSpec((seq, cbw), lambda b, c: (b, c)),
            pl.BlockSpec((1, V7X_SUBLANES, cbw), lambda b, c: (b, 0, c)),
        ],
        out_shape=[
            jax.ShapeDtypeStruct((proj.shape[0], width), BF16),
            jax.ShapeDtypeStruct((batch, V7X_SUBLANES, width), F32),
        ],
        compiler_params=_cparams("parallel", "parallel"),
        name="sconv_prompt",
    )(proj, proj, proj, halo, cw)


def _sconv_sample_body(bg_ref, cg_ref, hx_ref, st_ref, cw_ref, y_ref, ps_ref, *, seg):
    cw = cw_ref[...]
    width = cw.shape[0]
    hist = [st_ref[j] for j in range(width - 1)]
    for t in range(seg):
        p = cg_ref[t] * hx_ref[t]
        ps_ref[t] = p
        taps = hist + [p]
        z = taps[0] * cw[0:1]
        for j in range(1, width):
            z = z + taps[j] * cw[j:j + 1]
        y_ref[t] = (bg_ref[t] * z).astype(y_ref.dtype)
        hist = hist[1:] + [p]


def sconv_sample(proj_t, st, cw, *, width, cbw=512):
    seg, n_seq, _ = proj_t.shape
    nb = width // cbw
    return pl.pallas_call(
        functools.partial(_sconv_sample_body, seg=seg),
        grid=(nb,),
        in_specs=[
            pl.BlockSpec((seg, n_seq, cbw), lambda c: (0, 0, c)),
            pl.BlockSpec((seg, n_seq, cbw), lambda c: (0, 0, nb + c)),
            pl.BlockSpec((seg, n_seq, cbw), lambda c: (0, 0, 2 * nb + c)),
            pl.BlockSpec((st.shape[0], n_seq, cbw), lambda c: (0, 0, c)),
            pl.BlockSpec((cw.shape[0], cbw), lambda c: (0, c)),
        ],
        out_specs=[
            pl.BlockSpec((seg, n_seq, cbw), lambda c: (0, 0, c)),
            pl.BlockSpec((seg, n_seq, cbw), lambda c: (0, 0, c)),
        ],
        out_shape=[
            jax.ShapeDtypeStruct((seg, n_seq, width), BF16),
            jax.ShapeDtypeStruct((seg, n_seq, width), F32),
        ],
        compiler_params=_cparams("parallel"),
        name="sconv_sample",
    )(proj_t, proj_t, proj_t, st, cw)


def _softmax_rows(s):
    e = jnp.exp(s - jnp.max(s, axis=-1, keepdims=True))
    return e / jnp.sum(e, axis=-1, keepdims=True)


def _xattn_prompt_body(q_ref, k_ref, v_ref, o_ref, *, scale):
    s = lax.dot_general(q_ref[...], k_ref[...].astype(BF16), (((1,), (1,)), ((), ())), preferred_element_type=F32) * scale
    p = _softmax_rows(s).astype(BF16)
    o_ref[...] = jnp.dot(p, v_ref[...].astype(BF16), preferred_element_type=F32).astype(o_ref.dtype)


def xattn_prompt(q, mk, mv, *, batch, seq, n_mem, n_heads, hd, tq=2048):
    nq = seq // tq
    return pl.pallas_call(
        functools.partial(_xattn_prompt_body, scale=hd ** -0.5),
        grid=(batch, n_heads, nq),
        in_specs=[
            pl.BlockSpec((tq, hd), lambda b, h, i: (b * nq + i, h)),
            pl.BlockSpec((n_mem, hd), lambda b, h, i: (b, h)),
            pl.BlockSpec((n_mem, hd), lambda b, h, i: (b, h)),
        ],
        out_specs=pl.BlockSpec((tq, hd), lambda b, h, i: (b * nq + i, h)),
        out_shape=jax.ShapeDtypeStruct((q.shape[0], n_heads * hd), BF16),
        compiler_params=_cparams("parallel", "parallel", "parallel"),
        name="xattn_prompt",
    )(q, mk, mv)


def _xattn_sample_body(q_ref, k_ref, v_ref, o_ref, *, n_heads, scale):
    n_mem, hd = k_ref.shape[2], k_ref.shape[4]
    k2 = k_ref[0, 0].reshape(n_mem * n_heads, hd).astype(BF16)
    v2 = v_ref[0, 0].reshape(n_mem * n_heads, hd).astype(BF16)
    s = lax.dot_general(q_ref[0], k2, (((1,), (1,)), ((), ())), preferred_element_type=F32) * scale
    row = lax.broadcasted_iota(jnp.int32, s.shape, 0)
    col = lax.broadcasted_iota(jnp.int32, s.shape, 1)
    s = jnp.where((row % n_heads) == (col % n_heads), s, -jnp.inf)
    p = _softmax_rows(s).astype(BF16)
    o_ref[0] = jnp.dot(p, v2, preferred_element_type=F32).astype(o_ref.dtype)


def xattn_sample(q3, cache_k, cache_v, layer):
    n_seq, rows, hd = q3.shape
    _, _, n_mem, n_heads, _ = cache_k.shape
    kv_spec = pl.BlockSpec((1, 1, n_mem, n_heads, hd), lambda b: (layer, b, 0, 0, 0))
    return pl.pallas_call(
        functools.partial(_xattn_sample_body, n_heads=n_heads, scale=hd ** -0.5),
        grid=(n_seq,),
        in_specs=[pl.BlockSpec((1, rows, hd), lambda b: (b, 0, 0)), kv_spec, kv_spec],
        out_specs=pl.BlockSpec((1, rows, hd), lambda b: (b, 0, 0)),
        out_shape=jax.ShapeDtypeStruct((n_seq, rows, hd), BF16),
        compiler_params=_cparams("parallel"),
        name="xattn_sample",
    )(q3, cache_k, cache_v)


def kernel(x_prompt, x_sample, mem_prompt, cache_mem_k, cache_mem_v, state_mlstm_c, state_mlstm_n, state_mlstm_m, state_lru_h, state_lru_conv, state_sconv, norm_mix, norm_x, norm_ff, norm_final, w_in_e, b_if, mh_norm, conv_b_w, conv_b_b, lru_wa, lru_ba, lru_wi, lru_bi, lru_lam, w_out_e, w_in_o, conv_c_w, w_out_o, w_xq, w_xk, w_xv, w_xo, w_ff1, w_ff2):
    bp, sp, d = x_prompt.shape
    bs, ss, _ = x_sample.shape
    n_mem = mem_prompt.shape[1]
    depth, _, _, xh, xhd = cache_mem_k.shape
    _, _, ah, dk, dv = state_mlstm_c.shape
    bw = state_lru_h.shape[-1]
    bconv = state_lru_conv.shape[2] + 1
    cconv = state_sconv.shape[2] + 1
    cwid = state_sconv.shape[-1]
    aw = ah * dv
    aqk = ah * dk
    tp = bp * sp
    ts = bs * ss
    t = tp + ts
    TM = 1088
    TN = 1024
    TK2 = 4096
    TM_RES = 544
    TM_MEM = 512
    assert t % TM == 0 and tp % 512 == 0 and ts == 512
    w_ff2 = w_ff2.astype(BF16)

    x = jnp.concatenate([x_prompt.reshape(tp, d), x_sample.reshape(ts, d)], axis=0)
    mem_bf = mem_prompt.reshape(bp * n_mem, d).astype(BF16)

    outs = {}
    ie = io = 0
    p_mem_k, p_mem_v = [], []
    for l in range(depth):
        u = rmsnorm(x, norm_mix[l])
        if l % 2 == 0:
            wt = jnp.swapaxes(w_in_e[ie], 0, 1)
            o_qkvo = 2 * aqk + 2 * aw
            o_if = o_qkvo + 2 * ah
            w_if = jnp.pad(wt[o_qkvo:o_if].T, ((0, 0), (0, V7X_LANES - 2 * ah))).astype(BF16)
            bias_if = jnp.pad(b_if[ie], (0, V7X_LANES - 2 * ah)).reshape(1, V7X_LANES)
            col_q, col_k, col_v, col_o = 0, aqk, 2 * aqk, 2 * aqk + aw
            col_x, col_g = 0, bw
            proj = matmul([u], wt, n_out=o_qkvo, w_t=True, tm=TM, tn=TN, name="in_proj_even_qkvo")
            proj_b = matmul([u], wt[o_if:], w_t=True, tm=TM, tn=TN, name="in_proj_even_lru")
            gc, gr = mlstm_gates(u, w_if, bias_if, n_prompt_rows=tp, seg_sample=ss)
            gr = gr.reshape(16, 1, t)
            mh = mh_norm[ie].reshape(1, aw)
            mkw = dict(n_heads=ah, dk=dk, dv=dv, col_q=col_q, col_k=col_k, col_v=col_v, col_o=col_o)
            ya_p, pc, pn, pm = mlstm_prompt(proj, gc, gr, mh, batch=bp, seq=sp, **mkw)
            n0_tok = jnp.repeat(jnp.transpose(state_mlstm_n[ie], (1, 0, 2)), ss, axis=1)
            m0_tok = jnp.broadcast_to(jnp.repeat(state_mlstm_m[ie].T, ss, axis=1)[:, :, None], (ah, ts, MLSTM_CHUNK))
            ya_s, sc_, sn_tok, sm_tok = mlstm_sample(proj, gc, gr, mh, state_mlstm_c[ie:ie + 1], n0_tok, m0_tok,
                                                     row0=tp, n_seq=bs, seg=ss, **mkw)
            outs.setdefault("pc", []).append(pc.reshape(bp, ah, dk, dv))
            outs.setdefault("pn", []).append(pn.reshape(bp, ah, dk))
            outs.setdefault("pm", []).append(pm.reshape(bp, ah))
            outs.setdefault("sc", []).append(sc_[0])
            outs.setdefault("sn", []).append(jnp.transpose(sn_tok[:, ss - 1::ss, :], (1, 0, 2)))
            outs.setdefault("sm", []).append(sm_tok[:, ss - 1::ss, 0].T)

            cw = conv_b_w[ie]
            cbias = conv_b_b[ie].reshape(1, bw)
            wa = lru_wa[ie].astype(BF16)
            wi = lru_wi[ie].astype(BF16)
            ba = lru_ba[ie].reshape(1, bw)
            bi = lru_bi[ie].reshape(1, bw)
            lam = lru_lam[ie].reshape(1, bw)
            halo_p = jnp.zeros((bp, V7X_SUBLANES, bw), F32)
            h0_p = jnp.zeros((bp, 1, bw), F32)
            yb_p, ph, xtail = lru_prompt(proj_b, halo_p, h0_p, cw, cbias, wa, wi, ba, bi, lam,
                                  batch=bp, seq=sp, width=bw, col_x=col_x, col_g=col_g)
            st_s = jnp.transpose(state_lru_conv[ie], (1, 0, 2))
            xb_s = proj_b[tp:, col_x:col_x + bw].reshape(bs, ss, bw)
            gb_s = proj_b[tp:, col_g:col_g + bw].reshape(bs, ss, bw)
            yb_s, sh = lru_sample(jnp.transpose(xb_s, (1, 0, 2)), jnp.transpose(gb_s, (1, 0, 2)), st_s,
                                  state_lru_h[ie], cw, cbias, wa, wi, ba, bi, lam)
            yb_s = jnp.transpose(yb_s, (1, 0, 2)).reshape(ts, bw)
            outs.setdefault("ph", []).append(ph.reshape(bp, bw))
            outs.setdefault("pcb", []).append(xtail[:, V7X_SUBLANES - (bconv - 1):])
            outs.setdefault("sh", []).append(sh)
            outs.setdefault("scb", []).append(jnp.concatenate([state_lru_conv[ie], xb_s], axis=1)[:, ss:])

            ya = place_rows(ya_p, ya_s, tp)
            yb = place_rows(yb_p, yb_s, tp)
            x = matmul([ya, yb], w_out_e, layer=ie, res=x, tm=TM_RES, tn=TN, name="out_proj_even")
            ie += 1
        else:
            proj = matmul([u], w_in_o, layer=io, tm=TM, tn=TN, name="in_proj_odd")
            cw = conv_c_w[io]
            halo_p = jnp.zeros((bp, V7X_SUBLANES, cwid), F32)
            y_p, tail = sconv_prompt(proj, halo_p, cw, batch=bp, seq=sp, width=cwid)
            st_s = jnp.transpose(state_sconv[io], (1, 0, 2))
            proj_t = jnp.transpose(proj[tp:].reshape(bs, ss, 3 * cwid), (1, 0, 2))
            y_s, ps = sconv_sample(proj_t, st_s, cw, width=cwid)
            y_s = jnp.transpose(y_s, (1, 0, 2)).reshape(ts, cwid)
            outs.setdefault("psb", []).append(tail[:, V7X_SUBLANES - (cconv - 1):])
            ps_b = jnp.transpose(ps, (1, 0, 2))
            outs.setdefault("ssb", []).append(jnp.concatenate([state_sconv[io], ps_b], axis=1)[:, ss:])
            y = place_rows(y_p, y_s, tp)
            x = matmul([y], w_out_o, layer=io, res=x, tm=TM_RES, tn=TN, name="out_proj_odd")
            io += 1

        u = rmsnorm(x, norm_x[l])
        q = matmul([u], w_xq, layer=l, out_dtype=BF16, tm=TM, tn=TN, name="xattn_q")
        mk = matmul([mem_bf], w_xk, layer=l, tm=TM_MEM, tn=TN, name="mem_k")
        mv = matmul([mem_bf], w_xv, layer=l, tm=TM_MEM, tn=TN, name="mem_v")
        p_mem_k.append(mk.reshape(bp, n_mem, xh, xhd))
        p_mem_v.append(mv.reshape(bp, n_mem, xh, xhd))
        o_p = xattn_prompt(q, mk, mv, batch=bp, seq=sp, n_mem=n_mem, n_heads=xh, hd=xhd)
        o_s = xattn_sample(q[tp:].reshape(bs, ss * xh, xhd), cache_mem_k, cache_mem_v, l)
        o = place_rows(o_p, o_s.reshape(ts, d), tp)
        x = matmul([o], w_xo, layer=l, res=x, tm=TM_RES, tn=TN, name="xattn_o")

        u = rmsnorm(x, norm_ff[l])
        hmid = matmul([u], w_ff1, layer=l, act="relu2", out_dtype=BF16, tm=TM, tn=TN, name="ff1")
        x = matmul_ksplit(hmid, w_ff2, l, x, tm=TM, tn=TN, tk=TK2, name="ff2")

    y_p = rmsnorm(x, norm_final, out_dtype=F32, row0=0, nrows=tp)
    y_s = rmsnorm(x, norm_final, out_dtype=F32, row0=tp, nrows=ts)
    dt = x_prompt.dtype
    st = lambda name: jnp.stack(outs[name]).astype(dt)
    return (y_p.reshape(bp, sp, d), y_s.reshape(bs, ss, d),
            st("pc"), st("pn"), st("pm"), st("ph"), st("pcb"), st("psb"),
            jnp.stack(p_mem_k), jnp.stack(p_mem_v),
            st("sc"), st("sn"), st("sm"), st("sh"), st("scb"), st("ssb"))
```

```python
import functools

import jax
import jax.numpy as jnp
from jax import lax
from jax.experimental import pallas as pl
from jax.experimental.pallas import tpu as pltpu

F32 = jnp.float32
BF16 = jnp.bfloat16
EPS = 1e-6
LRU_C = 8.0

V7X_LANES = 128
V7X_SUBLANES = 8
V7X_VMEM_BUDGET_BYTES = 60 * 1024 * 1024

MLSTM_CHUNK = 128
SCAN_CHUNK = 256


def _cparams(*semantics):
    return pltpu.CompilerParams(dimension_semantics=semantics, vmem_limit_bytes=V7X_VMEM_BUDGET_BYTES)


def _log_sigmoid(x):
    return jnp.minimum(x, 0.0) - jnp.log1p(jnp.exp(-jnp.abs(x)))


def _sigmoid(x):
    return 1.0 / (1.0 + jnp.exp(-x))


def _gelu_tanh(x):
    c = 0.7978845608028654
    return 0.5 * x * (1.0 + jnp.tanh(c * (x + 0.044715 * (x * x * x))))


def _expm1(x):
    u = jnp.exp(x)
    um1 = u - 1.0
    safe = jnp.where(um1 == 0.0, 1.0, jnp.log(u))
    return jnp.where(um1 == 0.0, x, jnp.where(um1 == -1.0, -1.0, um1 * x / safe))


def _split3(x):
    hi = x.astype(BF16)
    r1 = x - hi.astype(F32)
    mid = r1.astype(BF16)
    lo = (r1 - mid.astype(F32)).astype(BF16)
    return hi, mid, lo


def _dot01(mask_bf16, x):
    hi, mid, lo = _split3(x)
    acc = jnp.dot(mask_bf16, lo, preferred_element_type=F32)
    acc = acc + jnp.dot(mask_bf16, mid, preferred_element_type=F32)
    return acc + jnp.dot(mask_bf16, hi, preferred_element_type=F32)


def _rmsnorm_body(x_ref, g_ref, o_ref):
    x = x_ref[...]
    inv = lax.rsqrt(jnp.mean(x * x, axis=-1, keepdims=True) + EPS)
    o_ref[...] = ((x * inv) * g_ref[...]).astype(o_ref.dtype)


def rmsnorm(x, g, out_dtype=BF16, rows=256, row0=0, nrows=None):
    d = x.shape[1]
    nrows = x.shape[0] if nrows is None else nrows
    b0 = row0 // rows
    return pl.pallas_call(
        _rmsnorm_body,
        grid=(nrows // rows,),
        in_specs=[pl.BlockSpec((rows, d), lambda i: (b0 + i, 0)), pl.BlockSpec((1, d), lambda i: (0, 0))],
        out_specs=pl.BlockSpec((rows, d), lambda i: (i, 0)),
        out_shape=jax.ShapeDtypeStruct((nrows, d), out_dtype),
        compiler_params=_cparams("parallel"),
        name="rmsnorm",
    )(x, g.reshape(1, d))


def _place_rows_body(buf_ref, rows_ref, o_ref):
    del buf_ref
    o_ref[...] = rows_ref[...]


def place_rows(buf, rows, row0):
    r, c = rows.shape
    assert row0 % r == 0 and buf.shape[1] == c and buf.dtype == rows.dtype
    return pl.pallas_call(
        _place_rows_body,
        grid=(1,),
        in_specs=[pl.BlockSpec(memory_space=pl.ANY), pl.BlockSpec((r, c), lambda i: (0, 0))],
        out_specs=pl.BlockSpec((r, c), lambda i: (row0 // r, 0)),
        out_shape=jax.ShapeDtypeStruct(buf.shape, buf.dtype),
        input_output_aliases={0: 0},
        compiler_params=_cparams("arbitrary"),
        name="place_rows",
    )(buf, rows)


def _mm_body(*refs, n_in, has_res, has_side, act, kc, nj, w_t):
    x_refs = refs[:n_in]
    w_ref = refs[n_in]
    res_ref = refs[n_in + 1] if has_res else None
    n_io = n_in + 1 + (1 if has_res else 0)
    side_ref = refs[n_io] if has_side else None
    n_io += 1 if has_side else 0
    o_ref = refs[n_io]
    side_o_ref = refs[n_io + 1] if has_side else None
    wbf = refs[-1]
    p, i = pl.program_id(0), pl.program_id(1)

    @pl.when(p > 0)
    def _():
        if has_side:
            side_o_ref[...] = side_ref[...].astype(BF16)
        slot = (p - 1) % 2
        k0 = 0
        acc = None
        for x_ref in x_refs:
            kw = x_ref.shape[1]
            if w_t:
                part = lax.dot_general(x_ref[...], wbf[slot, :, k0:k0 + kw], (((1,), (1,)), ((), ())),
                                       preferred_element_type=F32)
            else:
                part = jnp.dot(x_ref[...], wbf[slot, k0:k0 + kw, :], preferred_element_type=F32)
            acc = part if acc is None else acc + part
            k0 += kw
        if act == "relu2":
            r = jnp.maximum(acc, 0.0)
            acc = r * r
        if has_res:
            acc = res_ref[...] + acc
        o_ref[...] = acc.astype(o_ref.dtype)

    @pl.when(p < nj)
    def _():
        chunk = pl.ds(pl.multiple_of(i * kc, kc), kc)
        if w_t:
            wbf[p % 2, :, chunk] = w_ref[...].astype(BF16)
        else:
            wbf[p % 2, chunk, :] = w_ref[...].astype(BF16)


def matmul(xs, w, *, layer=None, n_out=None, w_t=False, res=None, act=None, side=None, out_dtype=F32, tm, tn, name):
    m = xs[0].shape[0]
    kk = sum(x.shape[1] for x in xs)
    w_k, w_n = (w.shape[-1], w.shape[-2]) if w_t else (w.shape[-2], w.shape[-1])
    n = w_n if n_out is None else n_out
    ni, nj = m // tm, n // tn
    kc = kk // ni
    assert w_k == kk and kk % ni == 0 and kc % V7X_LANES == 0 and n % tn == 0 and m % tm == 0
    n_in = len(xs)

    def row(p, i):
        return jnp.where(p > 0, i, 0)

    def w_idx(p, i):
        blk = (jnp.where(p < nj, i, ni - 1), jnp.minimum(p, nj - 1))
        blk = blk[::-1] if w_t else blk
        return blk if layer is None else (layer,) + blk

    w_blk = (tn, kc) if w_t else (kc, tn)
    in_specs = [pl.BlockSpec((tm, x.shape[1]), lambda p, i: (row(p, i), 0)) for x in xs]
    in_specs.append(pl.BlockSpec(w_blk if layer is None else (None,) + w_blk, w_idx))
    args = list(xs) + [w]
    out_idx = lambda p, i: (row(p, i), jnp.maximum(p - 1, 0))
    if res is not None:
        in_specs.append(pl.BlockSpec((tm, tn), out_idx))
        args.append(res)
    out_specs = pl.BlockSpec((tm, tn), out_idx)
    out_shape = jax.ShapeDtypeStruct((m, n), out_dtype)
    if side is not None:
        s_arr, s_layer = side
        s_rows, s_cols = s_arr.shape[-2:]
        rb = s_rows // (nj * ni)
        assert s_rows % (nj * ni) == 0 and rb % 16 == 0
        slab = lambda p, i: jnp.maximum(p - 1, 0) * ni + row(p, i)
        in_specs.append(pl.BlockSpec((None, rb, s_cols), lambda p, i: (s_layer, slab(p, i), 0)))
        args.append(s_arr)
        out_specs = [out_specs, pl.BlockSpec((rb, s_cols), lambda p, i: (slab(p, i), 0))]
        out_shape = [out_shape, jax.ShapeDtypeStruct((s_rows, s_cols), BF16)]
    return pl.pallas_call(
        functools.partial(_mm_body, n_in=n_in, has_res=res is not None, has_side=side is not None, act=act, kc=kc,
                          nj=nj, w_t=w_t),
        grid=(nj + 1, ni),
        in_specs=in_specs,
        out_specs=out_specs,
        out_shape=out_shape,
        scratch_shapes=[pltpu.VMEM((2, tn, kk) if w_t else (2, kk, tn), BF16)],
        compiler_params=_cparams("arbitrary", "arbitrary"),
        name=name,
    )(*args)


def _mm_acc_body(x_ref, w_ref, res_ref, o_ref):
    k = pl.program_id(2)

    @pl.when(k == 0)
    def _():
        o_ref[...] = res_ref[...] + jnp.dot(x_ref[...], w_ref[...], preferred_element_type=F32)

    @pl.when(k != 0)
    def _():
        o_ref[...] += jnp.dot(x_ref[...], w_ref[...], preferred_element_type=F32)


def matmul_ksplit(x, w, res, *, tm, tn, tk, name):
    m, kk = x.shape
    n = w.shape[-1]
    return pl.pallas_call(
        _mm_acc_body,
        grid=(m // tm, n // tn, kk // tk),
        in_specs=[
            pl.BlockSpec((tm, tk), lambda i, j, k: (i, k)),
            pl.BlockSpec((tk, tn), lambda i, j, k: (k, j)),
            pl.BlockSpec((tm, tn), lambda i, j, k: (i, j)),
        ],
        out_specs=pl.BlockSpec((tm, tn), lambda i, j, k: (i, j)),
        out_shape=jax.ShapeDtypeStruct((m, n), F32),
        compiler_params=_cparams("parallel", "parallel", "arbitrary"),
        name=name,
    )(x, w, res)


def _gates_body(u_ref, w_ref, b_ref, gc_ref, gr_ref, *, tm, n_prompt_tiles, seg_sample):
    L = MLSTM_CHUNK
    g = jnp.dot(u_ref[...], w_ref[...], preferred_element_type=F32) + b_ref[...]
    lane = lax.broadcasted_iota(jnp.int32, (tm, V7X_LANES), 1)
    val = jnp.where(lane < 8, g, _log_sigmoid(g))
    row = lax.broadcasted_iota(jnp.int32, (L, L), 0)
    col = lax.broadcasted_iota(jnp.int32, (L, L), 1)
    is_sample = pl.program_id(0) >= n_prompt_tiles
    tril = jnp.where(col <= row, 1.0, 0.0)
    tril_seg = jnp.where((col <= row) & ((row // seg_sample) == (col // seg_sample)), 1.0, 0.0)
    mask = jnp.where(is_sample, tril_seg, tril).astype(BF16)
    lane_l = lax.broadcasted_iota(jnp.int32, (L, V7X_LANES), 1)
    for s in range(tm // L):
        x = val[s * L:(s + 1) * L]
        out = jnp.where(lane_l < 8, x, _dot01(mask, x))
        gr_ref[:, s * L:(s + 1) * L] = out.T[:16]
        for c in range(16):
            gc_ref[c, s * L:(s + 1) * L, :] = jnp.broadcast_to(out[:, c:c + 1], (L, V7X_LANES))


def mlstm_gates(u, w_if, b_if, *, n_prompt_rows, seg_sample, tm=512):
    t, d = u.shape
    return pl.pallas_call(
        functools.partial(_gates_body, tm=tm, n_prompt_tiles=n_prompt_rows // tm, seg_sample=seg_sample),
        grid=(t // tm,),
        in_specs=[
            pl.BlockSpec((tm, d), lambda i: (i, 0)),
            pl.BlockSpec((d, V7X_LANES), lambda i: (0, 0)),
            pl.BlockSpec((1, V7X_LANES), lambda i: (0, 0)),
        ],
        out_specs=[
            pl.BlockSpec((16, tm, V7X_LANES), lambda i: (0, i, 0)),
            pl.BlockSpec((16, tm), lambda i: (0, i)),
        ],
        out_shape=[
            jax.ShapeDtypeStruct((16, t, V7X_LANES), F32),
            jax.ShapeDtypeStruct((16, t), F32),
        ],
        compiler_params=_cparams("parallel"),
        name="mlstm_gates",
    )(u, w_if, b_if)


def _head_out(hh, o, mh):
    hn = hh * lax.rsqrt(jnp.mean(hh * hh, axis=-1, keepdims=True) + EPS)
    return _sigmoid(o) * (hn * mh)


def _mlstm_prompt_body(q_ref, k_ref, v_ref, o_ref, ic_ref, bc_ref, ir_ref, br_ref, mh_ref,
                       ya_ref, c_ref, n_ref, m_ref, c_scr, n_scr, m_scr, *, n_heads, seq, scale):
    L = MLSTM_CHUNK
    c_scr[...] = jnp.zeros_like(c_scr)
    n_scr[...] = jnp.zeros_like(n_scr)
    m_scr[...] = jnp.zeros_like(m_scr)
    row = lax.broadcasted_iota(jnp.int32, (L, L), 0)
    col = lax.broadcasted_iota(jnp.int32, (L, L), 1)
    causal = col <= row

    def chunk(ci, carry):
        r0 = pl.multiple_of(ci * L, L)
        q = q_ref[pl.ds(r0, L), :] * scale
        k = k_ref[pl.ds(r0, L), :]
        v = v_ref[pl.ds(r0, L), :]
        bc = bc_ref[0, pl.ds(r0, L), :]
        ic = ic_ref[0, pl.ds(r0, L), :]
        i_row = ir_ref[0, :, pl.ds(r0, L)]
        b_row = br_ref[0, :, pl.ds(r0, L)]
        m_prev = m_scr[...]
        c_prev = c_scr[...]
        n_prev = n_scr[...]
        dmat = jnp.where(causal, bc - b_row + i_row, -jnp.inf)
        g = bc + m_prev
        mt = jnp.maximum(g, jnp.max(dmat, axis=-1, keepdims=True))
        w_inter = jnp.exp(g - mt)
        qb = q.astype(BF16)
        kb = k.astype(BF16)
        vb = v.astype(BF16)
        sc = lax.dot_general(qb, kb, (((1,), (1,)), ((), ())), preferred_element_type=F32) * jnp.exp(dmat - mt)
        inter = jnp.dot(qb, c_prev.astype(BF16), preferred_element_type=F32)
        num = jnp.concatenate([w_inter, w_inter], axis=1) * inter + jnp.dot(sc.astype(BF16), vb, preferred_element_type=F32)
        qn = jnp.sum(q * n_prev, axis=-1, keepdims=True)
        den = w_inter[:, :1] * qn + jnp.sum(sc, axis=-1, keepdims=True)
        hh = num / jnp.maximum(jnp.abs(den), jnp.exp(-mt[:, :1]))
        ya_ref[pl.ds(r0, L), :] = _head_out(hh, o_ref[pl.ds(r0, L), :], mh_ref[...]).astype(ya_ref.dtype)
        m_new = mt[L - 1:L, :1]
        b_last = b_row[:, L - 1:L]
        decay = jnp.exp(b_last + m_prev - m_new)
        kw = k * jnp.exp(b_last - bc + ic - m_new)
        c_scr[...] = decay * c_prev + lax.dot_general(kw.astype(BF16), vb, (((0,), (0,)), ((), ())), preferred_element_type=F32)
        n_scr[...] = decay * n_prev + jnp.sum(kw, axis=0, keepdims=True)
        m_scr[...] = m_new
        return carry

    lax.fori_loop(0, seq // L, chunk, 0, unroll=2)
    c_ref[0, 0] = c_scr[...]
    n_ref[0, 0] = n_scr[...]
    m_ref[0, 0] = m_scr[...]


def mlstm_prompt(proj, gc, gr, mh_norm, *, batch, seq, n_heads, dk, dv, col_q, col_k, col_v, col_o):
    L = MLSTM_CHUNK
    assert dk == L and dv == 2 * L
    rows = proj.shape[0]
    return pl.pallas_call(
        functools.partial(_mlstm_prompt_body, n_heads=n_heads, seq=seq, scale=dk ** -0.5),
        grid=(batch, n_heads),
        in_specs=[
            pl.BlockSpec((seq, dk), lambda b, h: (b, col_q // dk + h)),
            pl.BlockSpec((seq, dk), lambda b, h: (b, col_k // dk + h)),
            pl.BlockSpec((seq, dv), lambda b, h: (b, col_v // dv + h)),
            pl.BlockSpec((seq, dv), lambda b, h: (b, col_o // dv + h)),
            pl.BlockSpec((1, seq, V7X_LANES), lambda b, h: (h, b, 0)),
            pl.BlockSpec((1, seq, V7X_LANES), lambda b, h: (h + n_heads, b, 0)),
            pl.BlockSpec((1, 1, seq), lambda b, h: (h, 0, b)),
            pl.BlockSpec((1, 1, seq), lambda b, h: (h + n_heads, 0, b)),
            pl.BlockSpec((1, dv), lambda b, h: (0, h)),
        ],
        out_specs=[
            pl.BlockSpec((seq, dv), lambda b, h: (b, h)),
            pl.BlockSpec((1, 1, dk, dv), lambda b, h: (b, h, 0, 0)),
            pl.BlockSpec((1, 1, 1, dk), lambda b, h: (b, h, 0, 0)),
            pl.BlockSpec((1, 1, 1, 1), lambda b, h: (b, h, 0, 0)),
        ],
        out_shape=[
            jax.ShapeDtypeStruct((rows, n_heads * dv), BF16),
            jax.ShapeDtypeStruct((batch, n_heads, dk, dv), F32),
            jax.ShapeDtypeStruct((batch, n_heads, 1, dk), F32),
            jax.ShapeDtypeStruct((batch, n_heads, 1, 1), F32),
        ],
        scratch_shapes=[pltpu.VMEM((dk, dv), F32), pltpu.VMEM((1, dk), F32), pltpu.VMEM((1, 1), F32)],
        compiler_params=_cparams("parallel", "parallel"),
        name="mlstm_prompt",
    )(proj, proj, proj, proj, gc, gc, gr, gr, mh_norm)


def _mlstm_sample_body(q_ref, k_ref, v_ref, o_ref, ic_ref, bc_ref, ir_ref, br_ref, mh_ref, c0_ref, n0_ref, m0_ref,
                       ya_ref, c_ref, n_ref, m_ref, inter_scr, *, n_heads, seg, scale):
    L = MLSTM_CHUNK
    nb = L // seg
    row = lax.broadcasted_iota(jnp.int32, (L, L), 0)
    col = lax.broadcasted_iota(jnp.int32, (L, L), 1)
    same = (row // seg) == (col // seg)
    causal = (col <= row) & same
    last = col == (row // seg) * seg + (seg - 1)
    q = q_ref[...] * scale
    k = k_ref[...]
    v = v_ref[...]
    bc = bc_ref[0]
    ic = ic_ref[0]
    i_row = ir_ref[0]
    b_row = br_ref[0]
    m_prev = m0_ref[0]
    n_prev = n0_ref[0]
    dmat = jnp.where(causal, bc - b_row + i_row, -jnp.inf)
    g = bc + m_prev
    mt = jnp.maximum(g, jnp.max(dmat, axis=-1, keepdims=True))
    w_inter = jnp.exp(g - mt)
    qb = q.astype(BF16)
    kb = k.astype(BF16)
    vb = v.astype(BF16)
    sc = lax.dot_general(qb, kb, (((1,), (1,)), ((), ())), preferred_element_type=F32) * jnp.exp(dmat - mt)

    per_group = V7X_SUBLANES // seg
    sub = lax.broadcasted_iota(jnp.int32, (V7X_SUBLANES, 2 * L), 0)
    for gi in range(L // V7X_SUBLANES):
        q8 = qb[gi * V7X_SUBLANES:(gi + 1) * V7X_SUBLANES]
        acc = jnp.zeros((V7X_SUBLANES, 2 * L), F32)
        for j in range(per_group):
            r = jnp.dot(q8, c0_ref[0, gi * per_group + j, 0].astype(BF16), preferred_element_type=F32)
            acc = jnp.where((sub // seg) == j, r, acc)
        inter_scr[gi * V7X_SUBLANES:(gi + 1) * V7X_SUBLANES, :] = acc
    inter = inter_scr[...]

    num = jnp.concatenate([w_inter, w_inter], axis=1) * inter + jnp.dot(sc.astype(BF16), vb, preferred_element_type=F32)
    qn = jnp.sum(q * n_prev, axis=-1, keepdims=True)
    den = w_inter[:, :1] * qn + jnp.sum(sc, axis=-1, keepdims=True)
    hh = num / jnp.maximum(jnp.abs(den), jnp.exp(-mt[:, :1]))
    ya_ref[...] = _head_out(hh, o_ref[...], mh_ref[...]).astype(ya_ref.dtype)

    b_last = jnp.sum(jnp.where(last, jnp.broadcast_to(b_row, (L, L)), 0.0), axis=-1, keepdims=True)
    mt_row = mt.T
    m_new = jnp.sum(jnp.where(last, mt_row, 0.0), axis=-1, keepdims=True)
    decay = jnp.exp(b_last + m_prev - m_new)
    kw = k * jnp.exp(b_last - bc + ic - m_new)
    seg_ones = jnp.where(same, 1.0, 0.0).astype(BF16)
    n_ref[0] = decay * n_prev + _dot01(seg_ones, kw)
    m_ref[0] = jnp.broadcast_to(m_new, (L, L))
    rowk = lax.broadcasted_iota(jnp.int32, (L, L), 0)
    for bi in range(nb):
        kw_b = jnp.where((rowk // seg) == bi, kw, 0.0).astype(BF16)
        upd = lax.dot_general(kw_b, vb, (((0,), (0,)), ((), ())), preferred_element_type=F32)
        d_b = decay[bi * seg:bi * seg + 1, :1]
        c_ref[0, bi, 0] = d_b * c0_ref[0, bi, 0] + upd


def mlstm_sample(proj, gc, gr, mh_norm, c0, n0_tok, m0_tok, *, row0, n_seq, seg, n_heads, dk, dv,
                 col_q, col_k, col_v, col_o):
    L = MLSTM_CHUNK
    assert dk == L and dv == 2 * L and V7X_SUBLANES % seg == 0 and row0 % L == 0
    rows = n_seq * seg
    nb = L // seg
    t0 = row0 // L
    return pl.pallas_call(
        functools.partial(_mlstm_sample_body, n_heads=n_heads, seg=seg, scale=dk ** -0.5),
        grid=(rows // L, n_heads),
        in_specs=[
            pl.BlockSpec((L, dk), lambda j, h: (t0 + j, col_q // dk + h)),
            pl.BlockSpec((L, dk), lambda j, h: (t0 + j, col_k // dk + h)),
            pl.BlockSpec((L, dv), lambda j, h: (t0 + j, col_v // dv + h)),
            pl.BlockSpec((L, dv), lambda j, h: (t0 + j, col_o // dv + h)),
            pl.BlockSpec((1, L, V7X_LANES), lambda j, h: (h, t0 + j, 0)),
            pl.BlockSpec((1, L, V7X_LANES), lambda j, h: (h + n_heads, t0 + j, 0)),
            pl.BlockSpec((1, 1, L), lambda j, h: (h, 0, t0 + j)),
            pl.BlockSpec((1, 1, L), lambda j, h: (h + n_heads, 0, t0 + j)),
            pl.BlockSpec((1, dv), lambda j, h: (0, h)),
            pl.BlockSpec((1, nb, 1, dk, dv), lambda j, h: (0, j, h, 0, 0)),
            pl.BlockSpec((1, L, dk), lambda j, h: (h, j, 0)),
            pl.BlockSpec((1, L, L), lambda j, h: (h, j, 0)),
        ],
        out_specs=[
            pl.BlockSpec((L, dv), lambda j, h: (j, h)),
            pl.BlockSpec((1, nb, 1, dk, dv), lambda j, h: (0, j, h, 0, 0)),
            pl.BlockSpec((1, L, dk), lambda j, h: (h, j, 0)),
            pl.BlockSpec((1, L, L), lambda j, h: (h, j, 0)),
        ],
        out_shape=[
            jax.ShapeDtypeStruct((rows, n_heads * dv), BF16),
            jax.ShapeDtypeStruct((1, n_seq, n_heads, dk, dv), F32),
            jax.ShapeDtypeStruct((n_heads, rows, dk), F32),
            jax.ShapeDtypeStruct((n_heads, rows, L), F32),
        ],
        scratch_shapes=[pltpu.VMEM((L, dv), F32)],
        compiler_params=_cparams("parallel", "parallel"),
        name="mlstm_sample",
    )(proj, proj, proj, proj, gc, gc, gr, gr, mh_norm, c0, n0_tok, m0_tok)


def _lru_gates(xc, wa_ref, wi_ref, ba, bi, lam):
    nblk = wa_ref.shape[0]
    blk = wa_ref.shape[1]
    rs, igs = [], []
    for n in range(nblk):
        xb = xc[:, n * blk:(n + 1) * blk].astype(BF16)
        rs.append(jnp.dot(xb, wa_ref[n], preferred_element_type=F32))
        igs.append(jnp.dot(xb, wi_ref[n], preferred_element_type=F32))
    r = _sigmoid(jnp.concatenate(rs, axis=1) + ba)
    ig = _sigmoid(jnp.concatenate(igs, axis=1) + bi)
    log_a = LRU_C * r * _log_sigmoid(lam)
    a = jnp.exp(log_a)
    u = jnp.sqrt(-_expm1(2.0 * log_a)) * (ig * xc)
    return a, u


def _lru_prompt_body(x_ref, g_ref, halo_ref, h0_ref, cw_ref, cb_ref, wa_ref, wi_ref, ba_ref, bi_ref, lam_ref,
                     y_ref, hl_ref, tail_ref, h_scr, *, seq):
    Lc = SCAN_CHUNK
    cbw = x_ref.shape[1]
    h_scr[...] = h0_ref[0]
    row = lax.broadcasted_iota(jnp.int32, (Lc, cbw), 0)
    cw = cw_ref[...]
    width = cw.shape[0]

    def chunk(ci, carry):
        r0 = pl.multiple_of(ci * Lc, Lc)
        x = x_ref[pl.ds(r0, Lc), :]
        prev = x_ref[pl.ds(pl.multiple_of(jnp.maximum(r0 - V7X_SUBLANES, 0), V7X_SUBLANES), V7X_SUBLANES), :]
        prev = jnp.where(ci == 0, halo_ref[0], prev)
        xfull = jnp.concatenate([prev, x], axis=0)
        xc = None
        for j in range(width - 1, 0, -1):
            term = pltpu.roll(xfull, j, 0)[V7X_SUBLANES:] * cw[width - 1 - j:width - j]
            xc = term if xc is None else xc + term
        xc = xc + x * cw[width - 1:width] + cb_ref[...]
        a, u = _lru_gates(xc, wa_ref, wi_ref, ba_ref[...], bi_ref[...], lam_ref[...])
        d = 1
        while d < Lc:
            ok = row >= d
            a_sh = jnp.where(ok, pltpu.roll(a, d, 0), 1.0)
            u_sh = jnp.where(ok, pltpu.roll(u, d, 0), 0.0)
            u = a * u_sh + u
            a = a * a_sh
            d *= 2
        hs = u + a * h_scr[...]
        h_scr[...] = hs[Lc - 1:Lc]
        y_ref[pl.ds(r0, Lc), :] = (hs * _gelu_tanh(g_ref[pl.ds(r0, Lc), :])).astype(y_ref.dtype)
        return carry

    lax.fori_loop(0, seq // Lc, chunk, 0)
    hl_ref[0] = h_scr[...]
    tail_ref[0] = x_ref[pl.ds(seq - V7X_SUBLANES, V7X_SUBLANES), :]


def lru_prompt(proj, halo, h0, cw, cb, wa, wi, ba, bi, lam, *, batch, seq, width, col_x, col_g, cbw=256):
    blk = wa.shape[1]
    nper = cbw // blk
    vec = lambda b, c: (0, c)
    return pl.pallas_call(
        functools.partial(_lru_prompt_body, seq=seq),
        grid=(batch, width // cbw),
        in_specs=[
            pl.BlockSpec((seq, cbw), lambda b, c: (b, col_x // cbw + c)),
            pl.BlockSpec((seq, cbw), lambda b, c: (b, col_g // cbw + c)),
            pl.BlockSpec((1, V7X_SUBLANES, cbw), lambda b, c: (b, 0, c)),
            pl.BlockSpec((1, 1, cbw), lambda b, c: (b, 0, c)),
            pl.BlockSpec((cw.shape[0], cbw), vec),
            pl.BlockSpec((1, cbw), vec),
            pl.BlockSpec((nper, blk, blk), lambda b, c: (c, 0, 0)),
            pl.BlockSpec((nper, blk, blk), lambda b, c: (c, 0, 0)),
            pl.BlockSpec((1, cbw), vec),
            pl.BlockSpec((1, cbw), vec),
            pl.BlockSpec((1, cbw), vec),
        ],
        out_specs=[
            pl.BlockSpec((seq, cbw), lambda b, c: (b, c)),
            pl.BlockSpec((1, 1, cbw), lambda b, c: (b, 0, c)),
            pl.BlockSpec((1, V7X_SUBLANES, cbw), lambda b, c: (b, 0, c)),
        ],
        out_shape=[
            jax.ShapeDtypeStruct((proj.shape[0], width), BF16),
            jax.ShapeDtypeStruct((batch, 1, width), F32),
            jax.ShapeDtypeStruct((batch, V7X_SUBLANES, width), F32),
        ],
        scratch_shapes=[pltpu.VMEM((1, cbw), F32)],
        compiler_params=_cparams("parallel", "parallel"),
        name="lru_prompt",
    )(proj, proj, halo, h0, cw, cb, wa, wi, ba, bi, lam)


def _lru_sample_body(x_ref, g_ref, st_ref, h0_ref, cw_ref, cb_ref, wa_ref, wi_ref, ba_ref, bi_ref, lam_ref,
                     y_ref, hl_ref, *, seg):
    cw = cw_ref[...]
    width = cw.shape[0]
    hist = [st_ref[j] for j in range(width - 1)]
    h = h0_ref[...]
    for t in range(seg):
        x = x_ref[t]
        taps = hist + [x]
        xc = taps[0] * cw[0:1]
        for j in range(1, width):
            xc = xc + taps[j] * cw[j:j + 1]
        xc = xc + cb_ref[...]
        a, u = _lru_gates(xc, wa_ref, wi_ref, ba_ref[...], bi_ref[...], lam_ref[...])
        h = a * h + u
        y_ref[t] = (h * _gelu_tanh(g_ref[t])).astype(y_ref.dtype)
        hist = hist[1:] + [x]
    hl_ref[...] = h


def lru_sample(x_t, g_t, st, h0, cw, cb, wa, wi, ba, bi, lam, *, cbw=256):
    seg, n_seq, width = x_t.shape
    blk = wa.shape[1]
    nper = cbw // blk
    vec = lambda c: (0, c)
    return pl.pallas_call(
        functools.partial(_lru_sample_body, seg=seg),
        grid=(width // cbw,),
        in_specs=[
            pl.BlockSpec((seg, n_seq, cbw), lambda c: (0, 0, c)),
            pl.BlockSpec((seg, n_seq, cbw), lambda c: (0, 0, c)),
            pl.BlockSpec((st.shape[0], n_seq, cbw), lambda c: (0, 0, c)),
            pl.BlockSpec((n_seq, cbw), vec),
            pl.BlockSpec((cw.shape[0], cbw), vec),
            pl.BlockSpec((1, cbw), vec),
            pl.BlockSpec((nper, blk, blk), lambda c: (c, 0, 0)),
            pl.BlockSpec((nper, blk, blk), lambda c: (c, 0, 0)),
            pl.BlockSpec((1, cbw), vec),
            pl.BlockSpec((1, cbw), vec),
            pl.BlockSpec((1, cbw), vec),
        ],
        out_specs=[
            pl.BlockSpec((seg, n_seq, cbw), lambda c: (0, 0, c)),
            pl.BlockSpec((n_seq, cbw), vec),
        ],
        out_shape=[
            jax.ShapeDtypeStruct((seg, n_seq, width), BF16),
            jax.ShapeDtypeStruct((n_seq, width), F32),
        ],
        compiler_params=_cparams("parallel"),
        name="lru_sample",
    )(x_t, g_t, st, h0, cw, cb, wa, wi, ba, bi, lam)


def _sconv_prompt_body(bg_ref, cg_ref, hx_ref, halo_ref, cw_ref, y_ref, tail_ref, *, seq):
    Lc = SCAN_CHUNK
    cw = cw_ref[...]
    width = cw.shape[0]

    def chunk(ci, carry):
        r0 = pl.multiple_of(ci * Lc, Lc)
        p = cg_ref[pl.ds(r0, Lc), :] * hx_ref[pl.ds(r0, Lc), :]
        rp = pl.multiple_of(jnp.maximum(r0 - V7X_SUBLANES, 0), V7X_SUBLANES)
        prev = cg_ref[pl.ds(rp, V7X_SUBLANES), :] * hx_ref[pl.ds(rp, V7X_SUBLANES), :]
        prev = jnp.where(ci == 0, halo_ref[0], prev)
        pfull = jnp.concatenate([prev, p], axis=0)
        z = None
        for j in range(width - 1, 0, -1):
            term = pltpu.roll(pfull, j, 0)[V7X_SUBLANES:] * cw[width - 1 - j:width - j]
            z = term if z is None else z + term
        z = z + p * cw[width - 1:width]
        y_ref[pl.ds(r0, Lc), :] = (bg_ref[pl.ds(r0, Lc), :] * z).astype(y_ref.dtype)
        return carry

    lax.fori_loop(0, seq // Lc, chunk, 0)
    rt = seq - V7X_SUBLANES
    tail_ref[0] = cg_ref[pl.ds(rt, V7X_SUBLANES), :] * hx_ref[pl.ds(rt, V7X_SUBLANES), :]


def sconv_prompt(proj, halo, cw, *, batch, seq, width, cbw=512):
    nb = width // cbw
    return pl.pallas_call(
        functools.partial(_sconv_prompt_body, seq=seq),
        grid=(batch, nb),
        in_specs=[
            pl.BlockSpec((seq, cbw), lambda b, c: (b, c)),
            pl.BlockSpec((seq, cbw), lambda b, c: (b, nb + c)),
            pl.BlockSpec((seq, cbw), lambda b, c: (b, 2 * nb + c)),
            pl.BlockSpec((1, V7X_SUBLANES, cbw), lambda b, c: (b, 0, c)),
            pl.BlockSpec((cw.shape[0], cbw), lambda b, c: (0, c)),
        ],
        out_specs=[
            pl.BlockSpec((seq, cbw), lambda b, c: (b, c)),
            pl.BlockSpec((1, V7X_SUBLANES, cbw), lambda b, c: (b, 0, c)),
        ],
        out_shape=[
            jax.ShapeDtypeStruct((proj.shape[0], width), BF16),
            jax.ShapeDtypeStruct((batch, V7X_SUBLANES, width), F32),
        ],
        compiler_params=_cparams("parallel", "parallel"),
        name="sconv_prompt",
    )(proj, proj, proj, halo, cw)


def _sconv_sample_body(bg_ref, cg_ref, hx_ref, st_ref, cw_ref, y_ref, ps_ref, *, seg):
    cw = cw_ref[...]
    width = cw.shape[0]
    hist = [st_ref[j] for j in range(width - 1)]
    for t in range(seg):
        p = cg_ref[t] * hx_ref[t]
        ps_ref[t] = p
        taps = hist + [p]
        z = taps[0] * cw[0:1]
        for j in range(1, width):
            z = z + taps[j] * cw[j:j + 1]
        y_ref[t] = (bg_ref[t] * z).astype(y_ref.dtype)
        hist = hist[1:] + [p]


def sconv_sample(proj_t, st, cw, *, width, cbw=512):
    seg, n_seq, _ = proj_t.shape
    nb = width // cbw
    return pl.pallas_call(
        functools.partial(_sconv_sample_body, seg=seg),
        grid=(nb,),
        in_specs=[
            pl.BlockSpec((seg, n_seq, cbw), lambda c: (0, 0, c)),
            pl.BlockSpec((seg, n_seq, cbw), lambda c: (0, 0, nb + c)),
            pl.BlockSpec((seg, n_seq, cbw), lambda c: (0, 0, 2 * nb + c)),
            pl.BlockSpec((st.shape[0], n_seq, cbw), lambda c: (0, 0, c)),
            pl.BlockSpec((cw.shape[0], cbw), lambda c: (0, c)),
        ],
        out_specs=[
            pl.BlockSpec((seg, n_seq, cbw), lambda c: (0, 0, c)),
            pl.BlockSpec((seg, n_seq, cbw), lambda c: (0, 0, c)),
        ],
        out_shape=[
            jax.ShapeDtypeStruct((seg, n_seq, width), BF16),
            jax.ShapeDtypeStruct((seg, n_seq, width), F32),
        ],
        compiler_params=_cparams("parallel"),
        name="sconv_sample",
    )(proj_t, proj_t, proj_t, st, cw)


def _softmax_rows(s):
    e = jnp.exp(s - jnp.max(s, axis=-1, keepdims=True))
    return e / jnp.sum(e, axis=-1, keepdims=True)


def _xattn_prompt_body(q_ref, k_ref, v_ref, o_ref, *, scale):
    s = lax.dot_general(q_ref[...], k_ref[...].astype(BF16), (((1,), (1,)), ((), ())), preferred_element_type=F32) * scale
    p = _softmax_rows(s).astype(BF16)
    o_ref[...] = jnp.dot(p, v_ref[...].astype(BF16), preferred_element_type=F32).astype(o_ref.dtype)


def xattn_prompt(q, mk, mv, *, batch, seq, n_mem, n_heads, hd, tq=2048):
    nq = seq // tq
    return pl.pallas_call(
        functools.partial(_xattn_prompt_body, scale=hd ** -0.5),
        grid=(batch, n_heads, nq),
        in_specs=[
            pl.BlockSpec((tq, hd), lambda b, h, i: (b * nq + i, h)),
            pl.BlockSpec((n_mem, hd), lambda b, h, i: (b, h)),
            pl.BlockSpec((n_mem, hd), lambda b, h, i: (b, h)),
        ],
        out_specs=pl.BlockSpec((tq, hd), lambda b, h, i: (b * nq + i, h)),
        out_shape=jax.ShapeDtypeStruct((q.shape[0], n_heads * hd), BF16),
        compiler_params=_cparams("parallel", "parallel", "parallel"),
        name="xattn_prompt",
    )(q, mk, mv)


def _xattn_sample_body(q_ref, k_ref, v_ref, o_ref, *, n_heads, scale):
    n_mem, hd = k_ref.shape[2], k_ref.shape[4]
    k2 = k_ref[0, 0].reshape(n_mem * n_heads, hd).astype(BF16)
    v2 = v_ref[0, 0].reshape(n_mem * n_heads, hd).astype(BF16)
    s = lax.dot_general(q_ref[0], k2, (((1,), (1,)), ((), ())), preferred_element_type=F32) * scale
    row = lax.broadcasted_iota(jnp.int32, s.shape, 0)
    col = lax.broadcasted_iota(jnp.int32, s.shape, 1)
    s = jnp.where((row % n_heads) == (col % n_heads), s, -jnp.inf)
    p = _softmax_rows(s).astype(BF16)
    o_ref[0] = jnp.dot(p, v2, preferred_element_type=F32).astype(o_ref.dtype)


def xattn_sample(q3, cache_k, cache_v, layer):
    n_seq, rows, hd = q3.shape
    _, _, n_mem, n_heads, _ = cache_k.shape
    kv_spec = pl.BlockSpec((1, 1, n_mem, n_heads, hd), lambda b: (layer, b, 0, 0, 0))
    return pl.pallas_call(
        functools.partial(_xattn_sample_body, n_heads=n_heads, scale=hd ** -0.5),
        grid=(n_seq,),
        in_specs=[pl.BlockSpec((1, rows, hd), lambda b: (b, 0, 0)), kv_spec, kv_spec],
        out_specs=pl.BlockSpec((1, rows, hd), lambda b: (b, 0, 0)),
        out_shape=jax.ShapeDtypeStruct((n_seq, rows, hd), BF16),
        compiler_params=_cparams("parallel"),
        name="xattn_sample",
    )(q3, cache_k, cache_v)


def kernel(x_prompt, x_sample, mem_prompt, cache_mem_k, cache_mem_v, state_mlstm_c, state_mlstm_n, state_mlstm_m, state_lru_h, state_lru_conv, state_sconv, norm_mix, norm_x, norm_ff, norm_final, w_in_e, b_if, mh_norm, conv_b_w, conv_b_b, lru_wa, lru_ba, lru_wi, lru_bi, lru_lam, w_out_e, w_in_o, conv_c_w, w_out_o, w_xq, w_xk, w_xv, w_xo, w_ff1, w_ff2):
    bp, sp, d = x_prompt.shape
    bs, ss, _ = x_sample.shape
    n_mem = mem_prompt.shape[1]
    depth, _, _, xh, xhd = cache_mem_k.shape
    _, _, ah, dk, dv = state_mlstm_c.shape
    bw = state_lru_h.shape[-1]
    bconv = state_lru_conv.shape[2] + 1
    cconv = state_sconv.shape[2] + 1
    cwid = state_sconv.shape[-1]
    aw = ah * dv
    aqk = ah * dk
    tp = bp * sp
    ts = bs * ss
    t = tp + ts
    TM = 1088
    TN = 1024
    TK2 = 4096
    TM_RES = 544
    TM_MEM = 512
    assert t % TM == 0 and tp % 512 == 0 and ts == 512

    x = jnp.concatenate([x_prompt.reshape(tp, d), x_sample.reshape(ts, d)], axis=0)
    mem_bf = mem_prompt.reshape(bp * n_mem, d).astype(BF16)

    outs = {}
    ie = io = 0
    p_mem_k, p_mem_v = [], []
    for l in range(depth):
        u = rmsnorm(x, norm_mix[l])
        if l % 2 == 0:
            wt = jnp.swapaxes(w_in_e[ie], 0, 1)
            o_qkvo = 2 * aqk + 2 * aw
            o_if = o_qkvo + 2 * ah
            w_if = jnp.pad(wt[o_qkvo:o_if].T, ((0, 0), (0, V7X_LANES - 2 * ah))).astype(BF16)
            bias_if = jnp.pad(b_if[ie], (0, V7X_LANES - 2 * ah)).reshape(1, V7X_LANES)
            col_q, col_k, col_v, col_o = 0, aqk, 2 * aqk, 2 * aqk + aw
            col_x, col_g = 0, bw
            proj = matmul([u], wt, n_out=o_qkvo, w_t=True, tm=TM, tn=TN, name="in_proj_even_qkvo")
            proj_b = matmul([u], wt[o_if:], w_t=True, tm=TM, tn=TN, name="in_proj_even_lru")
            gc, gr = mlstm_gates(u, w_if, bias_if, n_prompt_rows=tp, seg_sample=ss)
            gr = gr.reshape(16, 1, t)
            mh = mh_norm[ie].reshape(1, aw)
            mkw = dict(n_heads=ah, dk=dk, dv=dv, col_q=col_q, col_k=col_k, col_v=col_v, col_o=col_o)
            ya_p, pc, pn, pm = mlstm_prompt(proj, gc, gr, mh, batch=bp, seq=sp, **mkw)
            n0_tok = jnp.repeat(jnp.transpose(state_mlstm_n[ie], (1, 0, 2)), ss, axis=1)
            m0_tok = jnp.broadcast_to(jnp.repeat(state_mlstm_m[ie].T, ss, axis=1)[:, :, None], (ah, ts, MLSTM_CHUNK))
            ya_s, sc_, sn_tok, sm_tok = mlstm_sample(proj, gc, gr, mh, state_mlstm_c[ie:ie + 1], n0_tok, m0_tok,
                                                     row0=tp, n_seq=bs, seg=ss, **mkw)
            outs.setdefault("pc", []).append(pc.reshape(bp, ah, dk, dv))
            outs.setdefault("pn", []).append(pn.reshape(bp, ah, dk))
            outs.setdefault("pm", []).append(pm.reshape(bp, ah))
            outs.setdefault("sc", []).append(sc_[0])
            outs.setdefault("sn", []).append(jnp.transpose(sn_tok[:, ss - 1::ss, :], (1, 0, 2)))
            outs.setdefault("sm", []).append(sm_tok[:, ss - 1::ss, 0].T)

            cw = conv_b_w[ie]
            cbias = conv_b_b[ie].reshape(1, bw)
            wa = lru_wa[ie].astype(BF16)
            wi = lru_wi[ie].astype(BF16)
            ba = lru_ba[ie].reshape(1, bw)
            bi = lru_bi[ie].reshape(1, bw)
            lam = lru_lam[ie].reshape(1, bw)
            halo_p = jnp.zeros((bp, V7X_SUBLANES, bw), F32)
            h0_p = jnp.zeros((bp, 1, bw), F32)
            yb_p, ph, xtail = lru_prompt(proj_b, halo_p, h0_p, cw, cbias, wa, wi, ba, bi, lam,
                                  batch=bp, seq=sp, width=bw, col_x=col_x, col_g=col_g)
            st_s = jnp.transpose(state_lru_conv[ie], (1, 0, 2))
            xb_s = proj_b[tp:, col_x:col_x + bw].reshape(bs, ss, bw)
            gb_s = proj_b[tp:, col_g:col_g + bw].reshape(bs, ss, bw)
            yb_s, sh = lru_sample(jnp.transpose(xb_s, (1, 0, 2)), jnp.transpose(gb_s, (1, 0, 2)), st_s,
                                  state_lru_h[ie], cw, cbias, wa, wi, ba, bi, lam)
            yb_s = jnp.transpose(yb_s, (1, 0, 2)).reshape(ts, bw)
            outs.setdefault("ph", []).append(ph.reshape(bp, bw))
            outs.setdefault("pcb", []).append(xtail[:, V7X_SUBLANES - (bconv - 1):])
            outs.setdefault("sh", []).append(sh)
            outs.setdefault("scb", []).append(jnp.concatenate([state_lru_conv[ie], xb_s], axis=1)[:, ss:])

            ya = place_rows(ya_p, ya_s, tp)
            yb = place_rows(yb_p, yb_s, tp)
            x = matmul([ya, yb], w_out_e, layer=ie, res=x, tm=TM_RES, tn=TN, name="out_proj_even")
            ie += 1
        else:
            proj = matmul([u], w_in_o, layer=io, tm=TM, tn=TN, name="in_proj_odd")
            cw = conv_c_w[io]
            halo_p = jnp.zeros((bp, V7X_SUBLANES, cwid), F32)
            y_p, tail = sconv_prompt(proj, halo_p, cw, batch=bp, seq=sp, width=cwid)
            st_s = jnp.transpose(state_sconv[io], (1, 0, 2))
            proj_t = jnp.transpose(proj[tp:].reshape(bs, ss, 3 * cwid), (1, 0, 2))
            y_s, ps = sconv_sample(proj_t, st_s, cw, width=cwid)
            y_s = jnp.transpose(y_s, (1, 0, 2)).reshape(ts, cwid)
            outs.setdefault("psb", []).append(tail[:, V7X_SUBLANES - (cconv - 1):])
            ps_b = jnp.transpose(ps, (1, 0, 2))
            outs.setdefault("ssb", []).append(jnp.concatenate([state_sconv[io], ps_b], axis=1)[:, ss:])
            y = place_rows(y_p, y_s, tp)
            x = matmul([y], w_out_o, layer=io, res=x, tm=TM_RES, tn=TN, name="out_proj_odd")
            io += 1

        u = rmsnorm(x, norm_x[l])
        q = matmul([u], w_xq, layer=l, out_dtype=BF16, tm=TM, tn=TN, name="xattn_q")
        mk = matmul([mem_bf], w_xk, layer=l, tm=TM_MEM, tn=TN, name="mem_k")
        mv = matmul([mem_bf], w_xv, layer=l, tm=TM_MEM, tn=TN, name="mem_v")
        p_mem_k.append(mk.reshape(bp, n_mem, xh, xhd))
        p_mem_v.append(mv.reshape(bp, n_mem, xh, xhd))
        o_p = xattn_prompt(q, mk, mv, batch=bp, seq=sp, n_mem=n_mem, n_heads=xh, hd=xhd)
        o_s = xattn_sample(q[tp:].reshape(bs, ss * xh, xhd), cache_mem_k, cache_mem_v, l)
        o = place_rows(o_p, o_s.reshape(ts, d), tp)
        x = matmul([o], w_xo, layer=l, res=x, tm=TM_RES, tn=TN, name="xattn_o")

        u = rmsnorm(x, norm_ff[l])
        hmid, w_ff2_bf = matmul([u], w_ff1, layer=l, act="relu2", side=(w_ff2, l), out_dtype=BF16, tm=TM, tn=TN,
                                name="ff1")
        x = matmul_ksplit(hmid, w_ff2_bf, x, tm=TM, tn=TN, tk=TK2, name="ff2")

    y_p = rmsnorm(x, norm_final, out_dtype=F32, row0=0, nrows=tp)
    y_s = rmsnorm(x, norm_final, out_dtype=F32, row0=tp, nrows=ts)
    dt = x_prompt.dtype
    st = lambda name: jnp.stack(outs[name]).astype(dt)
    return (y_p.reshape(bp, sp, d), y_s.reshape(bs, ss, d),
            st("pc"), st("pn"), st("pm"), st("ph"), st("pcb"), st("psb"),
            jnp.stack(p_mem_k), jnp.stack(p_mem_v),
            st("sc"), st("sn"), st("sm"), st("sh"), st("scb"), st("ssb"))
```

```python
import functools

import jax
import jax.numpy as jnp
from jax import lax
from jax.experimental import pallas as pl
from jax.experimental.pallas import tpu as pltpu

F32 = jnp.float32
BF16 = jnp.bfloat16
EPS = 1e-6
LRU_C = 8.0

V7X_LANES = 128
V7X_SUBLANES = 8
V7X_VMEM_BUDGET_BYTES = 60 * 1024 * 1024

MLSTM_CHUNK = 128
SCAN_CHUNK = 256


def _cparams(*semantics):
    return pltpu.CompilerParams(dimension_semantics=semantics, vmem_limit_bytes=V7X_VMEM_BUDGET_BYTES)


def _log_sigmoid(x):
    return jnp.minimum(x, 0.0) - jnp.log1p(jnp.exp(-jnp.abs(x)))


def _sigmoid(x):
    return 0.5 * (jnp.tanh(0.5 * x) + 1.0)


def _gelu_tanh(x):
    c = 0.7978845608028654
    return 0.5 * x * (1.0 + jnp.tanh(c * (x + 0.044715 * (x * x * x))))


def _expm1(x):
    return jnp.tanh(0.5 * x) * (jnp.exp(x) + 1.0)


def _split3(x):
    hi = x.astype(BF16)
    r1 = x - hi.astype(F32)
    mid = r1.astype(BF16)
    lo = (r1 - mid.astype(F32)).astype(BF16)
    return hi, mid, lo


def _dot01(mask_bf16, x):
    hi, mid, lo = _split3(x)
    acc = jnp.dot(mask_bf16, lo, preferred_element_type=F32)
    acc = acc + jnp.dot(mask_bf16, mid, preferred_element_type=F32)
    return acc + jnp.dot(mask_bf16, hi, preferred_element_type=F32)


def _rmsnorm_body(x_ref, g_ref, o_ref):
    x = x_ref[...]
    inv = lax.rsqrt(jnp.mean(x * x, axis=-1, keepdims=True) + EPS)
    o_ref[...] = ((x * inv) * g_ref[...]).astype(o_ref.dtype)


def rmsnorm(x, g, out_dtype=BF16, rows=256, row0=0, nrows=None):
    d = x.shape[1]
    nrows = x.shape[0] if nrows is None else nrows
    b0 = row0 // rows
    return pl.pallas_call(
        _rmsnorm_body,
        grid=(nrows // rows,),
        in_specs=[pl.BlockSpec((rows, d), lambda i: (b0 + i, 0)), pl.BlockSpec((1, d), lambda i: (0, 0))],
        out_specs=pl.BlockSpec((rows, d), lambda i: (i, 0)),
        out_shape=jax.ShapeDtypeStruct((nrows, d), out_dtype),
        compiler_params=_cparams("parallel"),
        name="rmsnorm",
    )(x, g.reshape(1, d))


def _place_rows_body(buf_ref, rows_ref, o_ref):
    del buf_ref
    o_ref[...] = rows_ref[...]


def place_rows(buf, rows, row0):
    r, c = rows.shape
    assert row0 % r == 0 and buf.shape[1] == c and buf.dtype == rows.dtype
    return pl.pallas_call(
        _place_rows_body,
        grid=(1,),
        in_specs=[pl.BlockSpec(memory_space=pl.ANY), pl.BlockSpec((r, c), lambda i: (0, 0))],
        out_specs=pl.BlockSpec((r, c), lambda i: (row0 // r, 0)),
        out_shape=jax.ShapeDtypeStruct(buf.shape, buf.dtype),
        input_output_aliases={0: 0},
        compiler_params=_cparams("arbitrary"),
        name="place_rows",
    )(buf, rows)


def _mm_body(*refs, n_in, has_res, has_side, act, kc, nj, w_t):
    x_refs = refs[:n_in]
    w_ref = refs[n_in]
    res_ref = refs[n_in + 1] if has_res else None
    n_io = n_in + 1 + (1 if has_res else 0)
    side_ref = refs[n_io] if has_side else None
    n_io += 1 if has_side else 0
    o_ref = refs[n_io]
    side_o_ref = refs[n_io + 1] if has_side else None
    wbf = refs[-1]
    p, i = pl.program_id(0), pl.program_id(1)

    @pl.when(p > 0)
    def _():
        if has_side:
            side_o_ref[...] = side_ref[...].astype(BF16)
        slot = (p - 1) % 2
        k0 = 0
        acc = None
        for x_ref in x_refs:
            kw = x_ref.shape[1]
            if w_t:
                part = lax.dot_general(x_ref[...], wbf[slot, :, k0:k0 + kw], (((1,), (1,)), ((), ())),
                                       preferred_element_type=F32)
            else:
                part = jnp.dot(x_ref[...], wbf[slot, k0:k0 + kw, :], preferred_element_type=F32)
            acc = part if acc is None else acc + part
            k0 += kw
        if act == "relu2":
            r = jnp.maximum(acc, 0.0)
            acc = r * r
        if has_res:
            acc = res_ref[...] + acc
        o_ref[...] = acc.astype(o_ref.dtype)

    @pl.when(p < nj)
    def _():
        chunk = pl.ds(pl.multiple_of(i * kc, kc), kc)
        if w_t:
            wbf[p % 2, :, chunk] = w_ref[...].astype(BF16)
        else:
            wbf[p % 2, chunk, :] = w_ref[...].astype(BF16)


def matmul(xs, w, *, layer=None, n_out=None, w_t=False, res=None, act=None, side=None, out_dtype=F32, tm, tn, name):
    m = xs[0].shape[0]
    kk = sum(x.shape[1] for x in xs)
    w_k, w_n = (w.shape[-1], w.shape[-2]) if w_t else (w.shape[-2], w.shape[-1])
    n = w_n if n_out is None else n_out
    ni, nj = m // tm, n // tn
    kc = kk // ni
    assert w_k == kk and kk % ni == 0 and kc % V7X_LANES == 0 and n % tn == 0 and m % tm == 0
    n_in = len(xs)

    def row(p, i):
        return jnp.where(p > 0, i, 0)

    def w_idx(p, i):
        blk = (jnp.where(p < nj, i, ni - 1), jnp.minimum(p, nj - 1))
        blk = blk[::-1] if w_t else blk
        return blk if layer is None else (layer,) + blk

    w_blk = (tn, kc) if w_t else (kc, tn)
    in_specs = [pl.BlockSpec((tm, x.shape[1]), lambda p, i: (row(p, i), 0)) for x in xs]
    in_specs.append(pl.BlockSpec(w_blk if layer is None else (None,) + w_blk, w_idx))
    args = list(xs) + [w]
    out_idx = lambda p, i: (row(p, i), jnp.maximum(p - 1, 0))
    if res is not None:
        in_specs.append(pl.BlockSpec((tm, tn), out_idx))
        args.append(res)
    out_specs = pl.BlockSpec((tm, tn), out_idx)
    out_shape = jax.ShapeDtypeStruct((m, n), out_dtype)
    if side is not None:
        s_arr, s_layer = side
        s_rows, s_cols = s_arr.shape[-2:]
        rb = s_rows // (nj * ni)
        assert s_rows % (nj * ni) == 0 and rb % 16 == 0
        slab = lambda p, i: jnp.maximum(p - 1, 0) * ni + row(p, i)
        in_specs.append(pl.BlockSpec((None, rb, s_cols), lambda p, i: (s_layer, slab(p, i), 0)))
        args.append(s_arr)
        out_specs = [out_specs, pl.BlockSpec((rb, s_cols), lambda p, i: (slab(p, i), 0))]
        out_shape = [out_shape, jax.ShapeDtypeStruct((s_rows, s_cols), BF16)]
    return pl.pallas_call(
        functools.partial(_mm_body, n_in=n_in, has_res=res is not None, has_side=side is not None, act=act, kc=kc,
                          nj=nj, w_t=w_t),
        grid=(nj + 1, ni),
        in_specs=in_specs,
        out_specs=out_specs,
        out_shape=out_shape,
        scratch_shapes=[pltpu.VMEM((2, tn, kk) if w_t else (2, kk, tn), BF16)],
        compiler_params=_cparams("arbitrary", "arbitrary"),
        name=name,
    )(*args)


def _mm_acc_body(x_ref, w_ref, res_ref, o_ref):
    k = pl.program_id(2)

    @pl.when(k == 0)
    def _():
        o_ref[...] = res_ref[...] + jnp.dot(x_ref[...], w_ref[...], preferred_element_type=F32)

    @pl.when(k != 0)
    def _():
        o_ref[...] += jnp.dot(x_ref[...], w_ref[...], preferred_element_type=F32)


def matmul_ksplit(x, w, res, *, tm, tn, tk, name):
    m, kk = x.shape
    n = w.shape[-1]
    return pl.pallas_call(
        _mm_acc_body,
        grid=(m // tm, n // tn, kk // tk),
        in_specs=[
            pl.BlockSpec((tm, tk), lambda i, j, k: (i, k)),
            pl.BlockSpec((tk, tn), lambda i, j, k: (k, j)),
            pl.BlockSpec((tm, tn), lambda i, j, k: (i, j)),
        ],
        out_specs=pl.BlockSpec((tm, tn), lambda i, j, k: (i, j)),
        out_shape=jax.ShapeDtypeStruct((m, n), F32),
        compiler_params=_cparams("parallel", "parallel", "arbitrary"),
        name=name,
    )(x, w, res)


def _gates_body(u_ref, w_ref, b_ref, gc_ref, gr_ref, *, tm, n_prompt_tiles, seg_sample):
    L = MLSTM_CHUNK
    g = jnp.dot(u_ref[...], w_ref[...], preferred_element_type=F32) + b_ref[...]
    lane = lax.broadcasted_iota(jnp.int32, (tm, V7X_LANES), 1)
    val = jnp.where(lane < 8, g, _log_sigmoid(g))
    row = lax.broadcasted_iota(jnp.int32, (L, L), 0)
    col = lax.broadcasted_iota(jnp.int32, (L, L), 1)
    is_sample = pl.program_id(0) >= n_prompt_tiles
    tril = jnp.where(col <= row, 1.0, 0.0)
    tril_seg = jnp.where((col <= row) & ((row // seg_sample) == (col // seg_sample)), 1.0, 0.0)
    mask = jnp.where(is_sample, tril_seg, tril).astype(BF16)
    lane_l = lax.broadcasted_iota(jnp.int32, (L, V7X_LANES), 1)
    for s in range(tm // L):
        x = val[s * L:(s + 1) * L]
        out = jnp.where(lane_l < 8, x, _dot01(mask, x))
        gr_ref[:, s * L:(s + 1) * L] = out.T[:16]
        for c in range(16):
            gc_ref[c, s * L:(s + 1) * L, :] = jnp.broadcast_to(out[:, c:c + 1], (L, V7X_LANES))


def mlstm_gates(u, w_if, b_if, *, n_prompt_rows, seg_sample, tm=512):
    t, d = u.shape
    return pl.pallas_call(
        functools.partial(_gates_body, tm=tm, n_prompt_tiles=n_prompt_rows // tm, seg_sample=seg_sample),
        grid=(t // tm,),
        in_specs=[
            pl.BlockSpec((tm, d), lambda i: (i, 0)),
            pl.BlockSpec((d, V7X_LANES), lambda i: (0, 0)),
            pl.BlockSpec((1, V7X_LANES), lambda i: (0, 0)),
        ],
        out_specs=[
            pl.BlockSpec((16, tm, V7X_LANES), lambda i: (0, i, 0)),
            pl.BlockSpec((16, tm), lambda i: (0, i)),
        ],
        out_shape=[
            jax.ShapeDtypeStruct((16, t, V7X_LANES), F32),
            jax.ShapeDtypeStruct((16, t), F32),
        ],
        compiler_params=_cparams("parallel"),
        name="mlstm_gates",
    )(u, w_if, b_if)


def _head_out(hh, o, mh):
    hn = hh * lax.rsqrt(jnp.mean(hh * hh, axis=-1, keepdims=True) + EPS)
    return _sigmoid(o) * (hn * mh)


def _mlstm_prompt_body(q_ref, k_ref, v_ref, o_ref, ic_ref, bc_ref, ir_ref, br_ref, mh_ref,
                       ya_ref, c_ref, n_ref, m_ref, c_scr, n_scr, m_scr, *, n_heads, seq, scale):
    L = MLSTM_CHUNK
    c_scr[...] = jnp.zeros_like(c_scr)
    n_scr[...] = jnp.zeros_like(n_scr)
    m_scr[...] = jnp.zeros_like(m_scr)
    row = lax.broadcasted_iota(jnp.int32, (L, L), 0)
    col = lax.broadcasted_iota(jnp.int32, (L, L), 1)
    causal = col <= row

    def chunk(ci, carry):
        r0 = pl.multiple_of(ci * L, L)
        q = q_ref[pl.ds(r0, L), :] * scale
        k = k_ref[pl.ds(r0, L), :]
        v = v_ref[pl.ds(r0, L), :]
        bc = bc_ref[0, pl.ds(r0, L), :]
        ic = ic_ref[0, pl.ds(r0, L), :]
        i_row = ir_ref[0, :, pl.ds(r0, L)]
        b_row = br_ref[0, :, pl.ds(r0, L)]
        m_prev = m_scr[...]
        c_prev = c_scr[...]
        n_prev = n_scr[...]
        dmat = jnp.where(causal, bc - b_row + i_row, -jnp.inf)
        g = bc + m_prev
        mt = jnp.maximum(g, jnp.max(dmat, axis=-1, keepdims=True))
        w_inter = jnp.exp(g - mt)
        qb = q.astype(BF16)
        kb = k.astype(BF16)
        vb = v.astype(BF16)
        sc = lax.dot_general(qb, kb, (((1,), (1,)), ((), ())), preferred_element_type=F32) * jnp.exp(dmat - mt)
        inter = jnp.dot(qb, c_prev.astype(BF16), preferred_element_type=F32)
        num = jnp.concatenate([w_inter, w_inter], axis=1) * inter + jnp.dot(sc.astype(BF16), vb, preferred_element_type=F32)
        qn = jnp.sum(q * n_prev, axis=-1, keepdims=True)
        den = w_inter[:, :1] * qn + jnp.sum(sc, axis=-1, keepdims=True)
        hh = num / jnp.maximum(jnp.abs(den), jnp.exp(-mt[:, :1]))
        ya_ref[pl.ds(r0, L), :] = _head_out(hh, o_ref[pl.ds(r0, L), :], mh_ref[...]).astype(ya_ref.dtype)
        m_new = mt[L - 1:L, :1]
        b_last = b_row[:, L - 1:L]
        decay = jnp.exp(b_last + m_prev - m_new)
        kw = k * jnp.exp(b_last - bc + ic - m_new)
        c_scr[...] = decay * c_prev + lax.dot_general(kw.astype(BF16), vb, (((0,), (0,)), ((), ())), preferred_element_type=F32)
        n_scr[...] = decay * n_prev + jnp.sum(kw, axis=0, keepdims=True)
        m_scr[...] = m_new
        return carry

    lax.fori_loop(0, seq // L, chunk, 0, unroll=2)
    c_ref[0, 0] = c_scr[...]
    n_ref[0, 0] = n_scr[...]
    m_ref[0, 0] = m_scr[...]


def mlstm_prompt(proj, gc, gr, mh_norm, *, batch, seq, n_heads, dk, dv, col_q, col_k, col_v, col_o):
    L = MLSTM_CHUNK
    assert dk == L and dv == 2 * L
    rows = proj.shape[0]
    return pl.pallas_call(
        functools.partial(_mlstm_prompt_body, n_heads=n_heads, seq=seq, scale=dk ** -0.5),
        grid=(batch, n_heads),
        in_specs=[
            pl.BlockSpec((seq, dk), lambda b, h: (b, col_q // dk + h)),
            pl.BlockSpec((seq, dk), lambda b, h: (b, col_k // dk + h)),
            pl.BlockSpec((seq, dv), lambda b, h: (b, col_v // dv + h)),
            pl.BlockSpec((seq, dv), lambda b, h: (b, col_o // dv + h)),
            pl.BlockSpec((1, seq, V7X_LANES), lambda b, h: (h, b, 0)),
            pl.BlockSpec((1, seq, V7X_LANES), lambda b, h: (h + n_heads, b, 0)),
            pl.BlockSpec((1, 1, seq), lambda b, h: (h, 0, b)),
            pl.BlockSpec((1, 1, seq), lambda b, h: (h + n_heads, 0, b)),
            pl.BlockSpec((1, dv), lambda b, h: (0, h)),
        ],
        out_specs=[
            pl.BlockSpec((seq, dv), lambda b, h: (b, h)),
            pl.BlockSpec((1, 1, dk, dv), lambda b, h: (b, h, 0, 0)),
            pl.BlockSpec((1, 1, 1, dk), lambda b, h: (b, h, 0, 0)),
            pl.BlockSpec((1, 1, 1, 1), lambda b, h: (b, h, 0, 0)),
        ],
        out_shape=[
            jax.ShapeDtypeStruct((rows, n_heads * dv), BF16),
            jax.ShapeDtypeStruct((batch, n_heads, dk, dv), F32),
            jax.ShapeDtypeStruct((batch, n_heads, 1, dk), F32),
            jax.ShapeDtypeStruct((batch, n_heads, 1, 1), F32),
        ],
        scratch_shapes=[pltpu.VMEM((dk, dv), F32), pltpu.VMEM((1, dk), F32), pltpu.VMEM((1, 1), F32)],
        compiler_params=_cparams("parallel", "parallel"),
        name="mlstm_prompt",
    )(proj, proj, proj, proj, gc, gc, gr, gr, mh_norm)


def _mlstm_sample_body(q_ref, k_ref, v_ref, o_ref, ic_ref, bc_ref, ir_ref, br_ref, mh_ref, c0_ref, n0_ref, m0_ref,
                       ya_ref, c_ref, n_ref, m_ref, inter_scr, *, n_heads, seg, scale):
    L = MLSTM_CHUNK
    nb = L // seg
    row = lax.broadcasted_iota(jnp.int32, (L, L), 0)
    col = lax.broadcasted_iota(jnp.int32, (L, L), 1)
    same = (row // seg) == (col // seg)
    causal = (col <= row) & same
    last = col == (row // seg) * seg + (seg - 1)
    q = q_ref[...] * scale
    k = k_ref[...]
    v = v_ref[...]
    bc = bc_ref[0]
    ic = ic_ref[0]
    i_row = ir_ref[0]
    b_row = br_ref[0]
    m_prev = m0_ref[0]
    n_prev = n0_ref[0]
    dmat = jnp.where(causal, bc - b_row + i_row, -jnp.inf)
    g = bc + m_prev
    mt = jnp.maximum(g, jnp.max(dmat, axis=-1, keepdims=True))
    w_inter = jnp.exp(g - mt)
    qb = q.astype(BF16)
    kb = k.astype(BF16)
    vb = v.astype(BF16)
    sc = lax.dot_general(qb, kb, (((1,), (1,)), ((), ())), preferred_element_type=F32) * jnp.exp(dmat - mt)

    per_group = V7X_SUBLANES // seg
    sub = lax.broadcasted_iota(jnp.int32, (V7X_SUBLANES, 2 * L), 0)
    for gi in range(L // V7X_SUBLANES):
        q8 = qb[gi * V7X_SUBLANES:(gi + 1) * V7X_SUBLANES]
        acc = jnp.zeros((V7X_SUBLANES, 2 * L), F32)
        for j in range(per_group):
            r = jnp.dot(q8, c0_ref[0, gi * per_group + j, 0].astype(BF16), preferred_element_type=F32)
            acc = jnp.where((sub // seg) == j, r, acc)
        inter_scr[gi * V7X_SUBLANES:(gi + 1) * V7X_SUBLANES, :] = acc
    inter = inter_scr[...]

    num = jnp.concatenate([w_inter, w_inter], axis=1) * inter + jnp.dot(sc.astype(BF16), vb, preferred_element_type=F32)
    qn = jnp.sum(q * n_prev, axis=-1, keepdims=True)
    den = w_inter[:, :1] * qn + jnp.sum(sc, axis=-1, keepdims=True)
    hh = num / jnp.maximum(jnp.abs(den), jnp.exp(-mt[:, :1]))
    ya_ref[...] = _head_out(hh, o_ref[...], mh_ref[...]).astype(ya_ref.dtype)

    b_last = jnp.sum(jnp.where(last, jnp.broadcast_to(b_row, (L, L)), 0.0), axis=-1, keepdims=True)
    mt_row = mt.T
    m_new = jnp.sum(jnp.where(last, mt_row, 0.0), axis=-1, keepdims=True)
    decay = jnp.exp(b_last + m_prev - m_new)
    kw = k * jnp.exp(b_last - bc + ic - m_new)
    seg_ones = jnp.where(same, 1.0, 0.0).astype(BF16)
    n_ref[0] = decay * n_prev + _dot01(seg_ones, kw)
    m_ref[0] = jnp.broadcast_to(m_new, (L, L))
    rowk = lax.broadcasted_iota(jnp.int32, (L, L), 0)
    for bi in range(nb):
        kw_b = jnp.where((rowk // seg) == bi, kw, 0.0).astype(BF16)
        upd = lax.dot_general(kw_b, vb, (((0,), (0,)), ((), ())), preferred_element_type=F32)
        d_b = decay[bi * seg:bi * seg + 1, :1]
        c_ref[0, bi, 0] = d_b * c0_ref[0, bi, 0] + upd


def mlstm_sample(proj, gc, gr, mh_norm, c0, n0_tok, m0_tok, *, row0, n_seq, seg, n_heads, dk, dv,
                 col_q, col_k, col_v, col_o):
    L = MLSTM_CHUNK
    assert dk == L and dv == 2 * L and V7X_SUBLANES % seg == 0 and row0 % L == 0
    rows = n_seq * seg
    nb = L // seg
    t0 = row0 // L
    return pl.pallas_call(
        functools.partial(_mlstm_sample_body, n_heads=n_heads, seg=seg, scale=dk ** -0.5),
        grid=(rows // L, n_heads),
        in_specs=[
            pl.BlockSpec((L, dk), lambda j, h: (t0 + j, col_q // dk + h)),
            pl.BlockSpec((L, dk), lambda j, h: (t0 + j, col_k // dk + h)),
            pl.BlockSpec((L, dv), lambda j, h: (t0 + j, col_v // dv + h)),
            pl.BlockSpec((L, dv), lambda j, h: (t0 + j, col_o // dv + h)),
            pl.BlockSpec((1, L, V7X_LANES), lambda j, h: (h, t0 + j, 0)),
            pl.BlockSpec((1, L, V7X_LANES), lambda j, h: (h + n_heads, t0 + j, 0)),
            pl.BlockSpec((1, 1, L), lambda j, h: (h, 0, t0 + j)),
            pl.BlockSpec((1, 1, L), lambda j, h: (h + n_heads, 0, t0 + j)),
            pl.BlockSpec((1, dv), lambda j, h: (0, h)),
            pl.BlockSpec((1, nb, 1, dk, dv), lambda j, h: (0, j, h, 0, 0)),
            pl.BlockSpec((1, L, dk), lambda j, h: (h, j, 0)),
            pl.BlockSpec((1, L, L), lambda j, h: (h, j, 0)),
        ],
        out_specs=[
            pl.BlockSpec((L, dv), lambda j, h: (j, h)),
            pl.BlockSpec((1, nb, 1, dk, dv), lambda j, h: (0, j, h, 0, 0)),
            pl.BlockSpec((1, L, dk), lambda j, h: (h, j, 0)),
            pl.BlockSpec((1, L, L), lambda j, h: (h, j, 0)),
        ],
        out_shape=[
            jax.ShapeDtypeStruct((rows, n_heads * dv), BF16),
            jax.ShapeDtypeStruct((1, n_seq, n_heads, dk, dv), F32),
            jax.ShapeDtypeStruct((n_heads, rows, dk), F32),
            jax.ShapeDtypeStruct((n_heads, rows, L), F32),
        ],
        scratch_shapes=[pltpu.VMEM((L, dv), F32)],
        compiler_params=_cparams("parallel", "parallel"),
        name="mlstm_sample",
    )(proj, proj, proj, proj, gc, gc, gr, gr, mh_norm, c0, n0_tok, m0_tok)


def _lru_gates(xc, wa_ref, wi_ref, ba, bi, lam):
    nblk = wa_ref.shape[0]
    blk = wa_ref.shape[1]
    rs, igs = [], []
    for n in range(nblk):
        xb = xc[:, n * blk:(n + 1) * blk].astype(BF16)
        rs.append(jnp.dot(xb, wa_ref[n], preferred_element_type=F32))
        igs.append(jnp.dot(xb, wi_ref[n], preferred_element_type=F32))
    r = _sigmoid(jnp.concatenate(rs, axis=1) + ba)
    ig = _sigmoid(jnp.concatenate(igs, axis=1) + bi)
    log_a = LRU_C * r * _log_sigmoid(lam)
    a = jnp.exp(log_a)
    u = jnp.sqrt(-_expm1(2.0 * log_a)) * (ig * xc)
    return a, u


def _lru_prompt_body(x_ref, g_ref, halo_ref, h0_ref, cw_ref, cb_ref, wa_ref, wi_ref, ba_ref, bi_ref, lam_ref,
                     y_ref, hl_ref, tail_ref, h_scr, *, seq):
    Lc = SCAN_CHUNK
    cbw = x_ref.shape[1]
    h_scr[...] = h0_ref[0]
    sub = lax.broadcasted_iota(jnp.int32, (Lc, cbw), 0) % V7X_SUBLANES
    cw = cw_ref[...]
    width = cw.shape[0]

    def chunk(ci, carry):
        r0 = pl.multiple_of(ci * Lc, Lc)
        x = x_ref[pl.ds(r0, Lc), :]
        prev = x_ref[pl.ds(pl.multiple_of(jnp.maximum(r0 - V7X_SUBLANES, 0), V7X_SUBLANES), V7X_SUBLANES), :]
        prev = jnp.where(ci == 0, halo_ref[0], prev)
        xfull = jnp.concatenate([prev, x], axis=0)
        xc = None
        for j in range(width - 1, 0, -1):
            term = pltpu.roll(xfull, j, 0)[V7X_SUBLANES:] * cw[width - 1 - j:width - j]
            xc = term if xc is None else xc + term
        xc = xc + x * cw[width - 1:width] + cb_ref[...]
        a, u = _lru_gates(xc, wa_ref, wi_ref, ba_ref[...], bi_ref[...], lam_ref[...])
        d = 1
        while d < V7X_SUBLANES:
            ok = sub >= d
            a_sh = jnp.where(ok, pltpu.roll(a, d, 0), 1.0)
            u_sh = jnp.where(ok, pltpu.roll(u, d, 0), 0.0)
            u = a * u_sh + u
            a = a * a_sh
            d *= 2
        hc = h_scr[...]
        groups = []
        for gi in range(Lc // V7X_SUBLANES):
            rows = slice(gi * V7X_SUBLANES, (gi + 1) * V7X_SUBLANES)
            hg = u[rows] + a[rows] * hc
            groups.append(hg)
            hc = hg[V7X_SUBLANES - 1:V7X_SUBLANES]
        hs = jnp.concatenate(groups, axis=0)
        h_scr[...] = hc
        y_ref[pl.ds(r0, Lc), :] = (hs * _gelu_tanh(g_ref[pl.ds(r0, Lc), :])).astype(y_ref.dtype)
        return carry

    lax.fori_loop(0, seq // Lc, chunk, 0)
    hl_ref[0] = h_scr[...]
    tail_ref[0] = x_ref[pl.ds(seq - V7X_SUBLANES, V7X_SUBLANES), :]


def lru_prompt(proj, halo, h0, cw, cb, wa, wi, ba, bi, lam, *, batch, seq, width, col_x, col_g, cbw=256):
    blk = wa.shape[1]
    nper = cbw // blk
    vec = lambda b, c: (0, c)
    return pl.pallas_call(
        functools.partial(_lru_prompt_body, seq=seq),
        grid=(batch, width // cbw),
        in_specs=[
            pl.BlockSpec((seq, cbw), lambda b, c: (b, col_x // cbw + c)),
            pl.BlockSpec((seq, cbw), lambda b, c: (b, col_g // cbw + c)),
            pl.BlockSpec((1, V7X_SUBLANES, cbw), lambda b, c: (b, 0, c)),
            pl.BlockSpec((1, 1, cbw), lambda b, c: (b, 0, c)),
            pl.BlockSpec((cw.shape[0], cbw), vec),
            pl.BlockSpec((1, cbw), vec),
            pl.BlockSpec((nper, blk, blk), lambda b, c: (c, 0, 0)),
            pl.BlockSpec((nper, blk, blk), lambda b, c: (c, 0, 0)),
            pl.BlockSpec((1, cbw), vec),
            pl.BlockSpec((1, cbw), vec),
            pl.BlockSpec((1, cbw), vec),
        ],
        out_specs=[
            pl.BlockSpec((seq, cbw), lambda b, c: (b, c)),
            pl.BlockSpec((1, 1, cbw), lambda b, c: (b, 0, c)),
            pl.BlockSpec((1, V7X_SUBLANES, cbw), lambda b, c: (b, 0, c)),
        ],
        out_shape=[
            jax.ShapeDtypeStruct((proj.shape[0], width), BF16),
            jax.ShapeDtypeStruct((batch, 1, width), F32),
            jax.ShapeDtypeStruct((batch, V7X_SUBLANES, width), F32),
        ],
        scratch_shapes=[pltpu.VMEM((1, cbw), F32)],
        compiler_params=_cparams("parallel", "parallel"),
        name="lru_prompt",
    )(proj, proj, halo, h0, cw, cb, wa, wi, ba, bi, lam)


def _lru_sample_body(x_ref, g_ref, st_ref, h0_ref, cw_ref, cb_ref, wa_ref, wi_ref, ba_ref, bi_ref, lam_ref,
                     y_ref, hl_ref, *, seg):
    cw = cw_ref[...]
    width = cw.shape[0]
    hist = [st_ref[j] for j in range(width - 1)]
    h = h0_ref[...]
    for t in range(seg):
        x = x_ref[t]
        taps = hist + [x]
        xc = taps[0] * cw[0:1]
        for j in range(1, width):
            xc = xc + taps[j] * cw[j:j + 1]
        xc = xc + cb_ref[...]
        a, u = _lru_gates(xc, wa_ref, wi_ref, ba_ref[...], bi_ref[...], lam_ref[...])
        h = a * h + u
        y_ref[t] = (h * _gelu_tanh(g_ref[t])).astype(y_ref.dtype)
        hist = hist[1:] + [x]
    hl_ref[...] = h


def lru_sample(x_t, g_t, st, h0, cw, cb, wa, wi, ba, bi, lam, *, cbw=256):
    seg, n_seq, width = x_t.shape
    blk = wa.shape[1]
    nper = cbw // blk
    vec = lambda c: (0, c)
    return pl.pallas_call(
        functools.partial(_lru_sample_body, seg=seg),
        grid=(width // cbw,),
        in_specs=[
            pl.BlockSpec((seg, n_seq, cbw), lambda c: (0, 0, c)),
            pl.BlockSpec((seg, n_seq, cbw), lambda c: (0, 0, c)),
            pl.BlockSpec((st.shape[0], n_seq, cbw), lambda c: (0, 0, c)),
            pl.BlockSpec((n_seq, cbw), vec),
            pl.BlockSpec((cw.shape[0], cbw), vec),
            pl.BlockSpec((1, cbw), vec),
            pl.BlockSpec((nper, blk, blk), lambda c: (c, 0, 0)),
            pl.BlockSpec((nper, blk, blk), lambda c: (c, 0, 0)),
            pl.BlockSpec((1, cbw), vec),
            pl.BlockSpec((1, cbw), vec),
            pl.BlockSpec((1, cbw), vec),
        ],
        out_specs=[
            pl.BlockSpec((seg, n_seq, cbw), lambda c: (0, 0, c)),
            pl.BlockSpec((n_seq, cbw), vec),
        ],
        out_shape=[
            jax.ShapeDtypeStruct((seg, n_seq, width), BF16),
            jax.ShapeDtypeStruct((n_seq, width), F32),
        ],
        compiler_params=_cparams("parallel"),
        name="lru_sample",
    )(x_t, g_t, st, h0, cw, cb, wa, wi, ba, bi, lam)


def _sconv_prompt_body(bg_ref, cg_ref, hx_ref, halo_ref, cw_ref, y_ref, tail_ref, *, seq):
    Lc = SCAN_CHUNK
    cw = cw_ref[...]
    width = cw.shape[0]

    def chunk(ci, carry):
        r0 = pl.multiple_of(ci * Lc, Lc)
        p = cg_ref[pl.ds(r0, Lc), :] * hx_ref[pl.ds(r0, Lc), :]
        rp = pl.multiple_of(jnp.maximum(r0 - V7X_SUBLANES, 0), V7X_SUBLANES)
        prev = cg_ref[pl.ds(rp, V7X_SUBLANES), :] * hx_ref[pl.ds(rp, V7X_SUBLANES), :]
        prev = jnp.where(ci == 0, halo_ref[0], prev)
        pfull = jnp.concatenate([prev, p], axis=0)
        z = None
        for j in range(width - 1, 0, -1):
            term = pltpu.roll(pfull, j, 0)[V7X_SUBLANES:] * cw[width - 1 - j:width - j]
            z = term if z is None else z + term
        z = z + p * cw[width - 1:width]
        y_ref[pl.ds(r0, Lc), :] = (bg_ref[pl.ds(r0, Lc), :] * z).astype(y_ref.dtype)
        return carry

    lax.fori_loop(0, seq // Lc, chunk, 0)
    rt = seq - V7X_SUBLANES
    tail_ref[0] = cg_ref[pl.ds(rt, V7X_SUBLANES), :] * hx_ref[pl.ds(rt, V7X_SUBLANES), :]


def sconv_prompt(proj, halo, cw, *, batch, seq, width, cbw=512):
    nb = width // cbw
    return pl.pallas_call(
        functools.partial(_sconv_prompt_body, seq=seq),
        grid=(batch, nb),
        in_specs=[
            pl.BlockSpec((seq, cbw), lambda b, c: (b, c)),
            pl.BlockSpec((seq, cbw), lambda b, c: (b, nb + c)),
            pl.BlockSpec((seq, cbw), lambda b, c: (b, 2 * nb + c)),
            pl.BlockSpec((1, V7X_SUBLANES, cbw), lambda b, c: (b, 0, c)),
            pl.BlockSpec((cw.shape[0], cbw), lambda b, c: (0, c)),
        ],
        out_specs=[
            pl.BlockSpec((seq, cbw), lambda b, c: (b, c)),
            pl.BlockSpec((1, V7X_SUBLANES, cbw), lambda b, c: (b, 0, c)),
        ],
        out_shape=[
            jax.ShapeDtypeStruct((proj.shape[0], width), BF16),
            jax.ShapeDtypeStruct((batch, V7X_SUBLANES, width), F32),
        ],
        compiler_params=_cparams("parallel", "parallel"),
        name="sconv_prompt",
    )(proj, proj, proj, halo, cw)


def _sconv_sample_body(bg_ref, cg_ref, hx_ref, st_ref, cw_ref, y_ref, ps_ref, *, seg):
    cw = cw_ref[...]
    width = cw.shape[0]
    hist = [st_ref[j] for j in range(width - 1)]
    for t in range(seg):
        p = cg_ref[t] * hx_ref[t]
        ps_ref[t] = p
        taps = hist + [p]
        z = taps[0] * cw[0:1]
        for j in range(1, width):
            z = z + taps[j] * cw[j:j + 1]
        y_ref[t] = (bg_ref[t] * z).astype(y_ref.dtype)
        hist = hist[1:] + [p]


def sconv_sample(proj_t, st, cw, *, width, cbw=512):
    seg, n_seq, _ = proj_t.shape
    nb = width // cbw
    return pl.pallas_call(
        functools.partial(_sconv_sample_body, seg=seg),
        grid=(nb,),
        in_specs=[
            pl.BlockSpec((seg, n_seq, cbw), lambda c: (0, 0, c)),
            pl.BlockSpec((seg, n_seq, cbw), lambda c: (0, 0, nb + c)),
            pl.BlockSpec((seg, n_seq, cbw), lambda c: (0, 0, 2 * nb + c)),
            pl.BlockSpec((st.shape[0], n_seq, cbw), lambda c: (0, 0, c)),
            pl.BlockSpec((cw.shape[0], cbw), lambda c: (0, c)),
        ],
        out_specs=[
            pl.BlockSpec((seg, n_seq, cbw), lambda c: (0, 0, c)),
            pl.BlockSpec((seg, n_seq, cbw), lambda c: (0, 0, c)),
        ],
        out_shape=[
            jax.ShapeDtypeStruct((seg, n_seq, width), BF16),
            jax.ShapeDtypeStruct((seg, n_seq, width), F32),
        ],
        compiler_params=_cparams("parallel"),
        name="sconv_sample",
    )(proj_t, proj_t, proj_t, st, cw)


def _softmax_rows(s):
    e = jnp.exp(s - jnp.max(s, axis=-1, keepdims=True))
    return e / jnp.sum(e, axis=-1, keepdims=True)


def _xattn_prompt_body(q_ref, k_ref, v_ref, o_ref, *, scale):
    s = lax.dot_general(q_ref[...], k_ref[...].astype(BF16), (((1,), (1,)), ((), ())), preferred_element_type=F32) * scale
    p = _softmax_rows(s).astype(BF16)
    o_ref[...] = jnp.dot(p, v_ref[...].astype(BF16), preferred_element_type=F32).astype(o_ref.dtype)


def xattn_prompt(q, mk, mv, *, batch, seq, n_mem, n_heads, hd, tq=2048):
    nq = seq // tq
    return pl.pallas_call(
        functools.partial(_xattn_prompt_body, scale=hd ** -0.5),
        grid=(batch, n_heads, nq),
        in_specs=[
            pl.BlockSpec((tq, hd), lambda b, h, i: (b * nq + i, h)),
            pl.BlockSpec((n_mem, hd), lambda b, h, i: (b, h)),
            pl.BlockSpec((n_mem, hd), lambda b, h, i: (b, h)),
        ],
        out_specs=pl.BlockSpec((tq, hd), lambda b, h, i: (b * nq + i, h)),
        out_shape=jax.ShapeDtypeStruct((q.shape[0], n_heads * hd), BF16),
        compiler_params=_cparams("parallel", "parallel", "parallel"),
        name="xattn_prompt",
    )(q, mk, mv)


def _xattn_sample_body(q_ref, k_ref, v_ref, o_ref, *, n_heads, scale):
    n_mem, hd = k_ref.shape[2], k_ref.shape[4]
    k2 = k_ref[0, 0].reshape(n_mem * n_heads, hd).astype(BF16)
    v2 = v_ref[0, 0].reshape(n_mem * n_heads, hd).astype(BF16)
    s = lax.dot_general(q_ref[0], k2, (((1,), (1,)), ((), ())), preferred_element_type=F32) * scale
    row = lax.broadcasted_iota(jnp.int32, s.shape, 0)
    col = lax.broadcasted_iota(jnp.int32, s.shape, 1)
    s = jnp.where((row % n_heads) == (col % n_heads), s, -jnp.inf)
    p = _softmax_rows(s).astype(BF16)
    o_ref[0] = jnp.dot(p, v2, preferred_element_type=F32).astype(o_ref.dtype)


def xattn_sample(q3, cache_k, cache_v, layer):
    n_seq, rows, hd = q3.shape
    _, _, n_mem, n_heads, _ = cache_k.shape
    kv_spec = pl.BlockSpec((1, 1, n_mem, n_heads, hd), lambda b: (layer, b, 0, 0, 0))
    return pl.pallas_call(
        functools.partial(_xattn_sample_body, n_heads=n_heads, scale=hd ** -0.5),
        grid=(n_seq,),
        in_specs=[pl.BlockSpec((1, rows, hd), lambda b: (b, 0, 0)), kv_spec, kv_spec],
        out_specs=pl.BlockSpec((1, rows, hd), lambda b: (b, 0, 0)),
        out_shape=jax.ShapeDtypeStruct((n_seq, rows, hd), BF16),
        compiler_params=_cparams("parallel"),
        name="xattn_sample",
    )(q3, cache_k, cache_v)


def kernel(x_prompt, x_sample, mem_prompt, cache_mem_k, cache_mem_v, state_mlstm_c, state_mlstm_n, state_mlstm_m, state_lru_h, state_lru_conv, state_sconv, norm_mix, norm_x, norm_ff, norm_final, w_in_e, b_if, mh_norm, conv_b_w, conv_b_b, lru_wa, lru_ba, lru_wi, lru_bi, lru_lam, w_out_e, w_in_o, conv_c_w, w_out_o, w_xq, w_xk, w_xv, w_xo, w_ff1, w_ff2):
    bp, sp, d = x_prompt.shape
    bs, ss, _ = x_sample.shape
    n_mem = mem_prompt.shape[1]
    depth, _, _, xh, xhd = cache_mem_k.shape
    _, _, ah, dk, dv = state_mlstm_c.shape
    bw = state_lru_h.shape[-1]
    bconv = state_lru_conv.shape[2] + 1
    cconv = state_sconv.shape[2] + 1
    cwid = state_sconv.shape[-1]
    aw = ah * dv
    aqk = ah * dk
    tp = bp * sp
    ts = bs * ss
    t = tp + ts
    TM = 1088
    TN = 1024
    TK2 = 4096
    TM_RES = 544
    TM_MEM = 512
    assert t % TM == 0 and tp % 512 == 0 and ts == 512

    x = jnp.concatenate([x_prompt.reshape(tp, d), x_sample.reshape(ts, d)], axis=0)
    mem_bf = mem_prompt.reshape(bp * n_mem, d).astype(BF16)

    outs = {}
    ie = io = 0
    p_mem_k, p_mem_v = [], []
    for l in range(depth):
        u = rmsnorm(x, norm_mix[l])
        if l % 2 == 0:
            wt = jnp.swapaxes(w_in_e[ie], 0, 1)
            o_qkvo = 2 * aqk + 2 * aw
            o_if = o_qkvo + 2 * ah
            w_if = jnp.pad(wt[o_qkvo:o_if].T, ((0, 0), (0, V7X_LANES - 2 * ah))).astype(BF16)
            bias_if = jnp.pad(b_if[ie], (0, V7X_LANES - 2 * ah)).reshape(1, V7X_LANES)
            col_q, col_k, col_v, col_o = 0, aqk, 2 * aqk, 2 * aqk + aw
            col_x, col_g = 0, bw
            proj = matmul([u], wt, n_out=o_qkvo, w_t=True, tm=TM, tn=TN, name="in_proj_even_qkvo")
            proj_b = matmul([u], wt[o_if:], w_t=True, tm=TM, tn=TN, name="in_proj_even_lru")
            gc, gr = mlstm_gates(u, w_if, bias_if, n_prompt_rows=tp, seg_sample=ss)
            gr = gr.reshape(16, 1, t)
            mh = mh_norm[ie].reshape(1, aw)
            mkw = dict(n_heads=ah, dk=dk, dv=dv, col_q=col_q, col_k=col_k, col_v=col_v, col_o=col_o)
            ya_p, pc, pn, pm = mlstm_prompt(proj, gc, gr, mh, batch=bp, seq=sp, **mkw)
            n0_tok = jnp.repeat(jnp.transpose(state_mlstm_n[ie], (1, 0, 2)), ss, axis=1)
            m0_tok = jnp.broadcast_to(jnp.repeat(state_mlstm_m[ie].T, ss, axis=1)[:, :, None], (ah, ts, MLSTM_CHUNK))
            ya_s, sc_, sn_tok, sm_tok = mlstm_sample(proj, gc, gr, mh, state_mlstm_c[ie:ie + 1], n0_tok, m0_tok,
                                                     row0=tp, n_seq=bs, seg=ss, **mkw)
            outs.setdefault("pc", []).append(pc.reshape(bp, ah, dk, dv))
            outs.setdefault("pn", []).append(pn.reshape(bp, ah, dk))
            outs.setdefault("pm", []).append(pm.reshape(bp, ah))
            outs.setdefault("sc", []).append(sc_[0])
            outs.setdefault("sn", []).append(jnp.transpose(sn_tok[:, ss - 1::ss, :], (1, 0, 2)))
            outs.setdefault("sm", []).append(sm_tok[:, ss - 1::ss, 0].T)

            cw = conv_b_w[ie]
            cbias = conv_b_b[ie].reshape(1, bw)
            wa = lru_wa[ie].astype(BF16)
            wi = lru_wi[ie].astype(BF16)
            ba = lru_ba[ie].reshape(1, bw)
            bi = lru_bi[ie].reshape(1, bw)
            lam = lru_lam[ie].reshape(1, bw)
            halo_p = jnp.zeros((bp, V7X_SUBLANES, bw), F32)
            h0_p = jnp.zeros((bp, 1, bw), F32)
            yb_p, ph, xtail = lru_prompt(proj_b, halo_p, h0_p, cw, cbias, wa, wi, ba, bi, lam,
                                  batch=bp, seq=sp, width=bw, col_x=col_x, col_g=col_g)
            st_s = jnp.transpose(state_lru_conv[ie], (1, 0, 2))
            xb_s = proj_b[tp:, col_x:col_x + bw].reshape(bs, ss, bw)
            gb_s = proj_b[tp:, col_g:col_g + bw].reshape(bs, ss, bw)
            yb_s, sh = lru_sample(jnp.transpose(xb_s, (1, 0, 2)), jnp.transpose(gb_s, (1, 0, 2)), st_s,
                                  state_lru_h[ie], cw, cbias, wa, wi, ba, bi, lam)
            yb_s = jnp.transpose(yb_s, (1, 0, 2)).reshape(ts, bw)
            outs.setdefault("ph", []).append(ph.reshape(bp, bw))
            outs.setdefault("pcb", []).append(xtail[:, V7X_SUBLANES - (bconv - 1):])
            outs.setdefault("sh", []).append(sh)
            outs.setdefault("scb", []).append(jnp.concatenate([state_lru_conv[ie], xb_s], axis=1)[:, ss:])

            ya = place_rows(ya_p, ya_s, tp)
            yb = place_rows(yb_p, yb_s, tp)
            x = matmul([ya, yb], w_out_e, layer=ie, res=x, tm=TM_RES, tn=TN, name="out_proj_even")
            ie += 1
        else:
            proj = matmul([u], w_in_o, layer=io, tm=TM, tn=TN, name="in_proj_odd")
            cw = conv_c_w[io]
            halo_p = jnp.zeros((bp, V7X_SUBLANES, cwid), F32)
            y_p, tail = sconv_prompt(proj, halo_p, cw, batch=bp, seq=sp, width=cwid)
            st_s = jnp.transpose(state_sconv[io], (1, 0, 2))
            proj_t = jnp.transpose(proj[tp:].reshape(bs, ss, 3 * cwid), (1, 0, 2))
            y_s, ps = sconv_sample(proj_t, st_s, cw, width=cwid)
            y_s = jnp.transpose(y_s, (1, 0, 2)).reshape(ts, cwid)
            outs.setdefault("psb", []).append(tail[:, V7X_SUBLANES - (cconv - 1):])
            ps_b = jnp.transpose(ps, (1, 0, 2))
            outs.setdefault("ssb", []).append(jnp.concatenate([state_sconv[io], ps_b], axis=1)[:, ss:])
            y = place_rows(y_p, y_s, tp)
            x = matmul([y], w_out_o, layer=io, res=x, tm=TM_RES, tn=TN, name="out_proj_odd")
            io += 1

        u = rmsnorm(x, norm_x[l])
        q = matmul([u], w_xq, layer=l, out_dtype=BF16, tm=TM, tn=TN, name="xattn_q")
        mk = matmul([mem_bf], w_xk, layer=l, tm=TM_MEM, tn=TN, name="mem_k")
        mv = matmul([mem_bf], w_xv, layer=l, tm=TM_MEM, tn=TN, name="mem_v")
        p_mem_k.append(mk.reshape(bp, n_mem, xh, xhd))
        p_mem_v.append(mv.reshape(bp, n_mem, xh, xhd))
        o_p = xattn_prompt(q, mk, mv, batch=bp, seq=sp, n_mem=n_mem, n_heads=xh, hd=xhd)
        o_s = xattn_sample(q[tp:].reshape(bs, ss * xh, xhd), cache_mem_k, cache_mem_v, l)
        o = place_rows(o_p, o_s.reshape(ts, d), tp)
        x = matmul([o], w_xo, layer=l, res=x, tm=TM_RES, tn=TN, name="xattn_o")

        u = rmsnorm(x, norm_ff[l])
        hmid, w_ff2_bf = matmul([u], w_ff1, layer=l, act="relu2", side=(w_ff2, l), out_dtype=BF16, tm=TM, tn=TN,
                                name="ff1")
        x = matmul_ksplit(hmid, w_ff2_bf, x, tm=TM, tn=TN, tk=TK2, name="ff2")

    y_p = rmsnorm(x, norm_final, out_dtype=F32, row0=0, nrows=tp)
    y_s = rmsnorm(x, norm_final, out_dtype=F32, row0=tp, nrows=ts)
    dt = x_prompt.dtype
    st = lambda name: jnp.stack(outs[name]).astype(dt)
    return (y_p.reshape(bp, sp, d), y_s.reshape(bs, ss, d),
            st("pc"), st("pn"), st("pm"), st("ph"), st("pcb"), st("psb"),
            jnp.stack(p_mem_k), jnp.stack(p_mem_v),
            st("sc"), st("sn"), st("sm"), st("sh"), st("scb"), st("ssb"))
```

```python
import functools

import jax
import jax.numpy as jnp
from jax import lax
from jax.experimental import pallas as pl
from jax.experimental.pallas import tpu as pltpu

F32 = jnp.float32
BF16 = jnp.bfloat16
EPS = 1e-6
LRU_C = 8.0

V7X_LANES = 128
V7X_SUBLANES = 8
V7X_VMEM_BUDGET_BYTES = 60 * 1024 * 1024

MLSTM_CHUNK = 128
SCAN_CHUNK = 256


def _cparams(*semantics):
    return pltpu.CompilerParams(dimension_semantics=semantics, vmem_limit_bytes=V7X_VMEM_BUDGET_BYTES)


def _log_sigmoid(x):
    return jnp.minimum(x, 0.0) - jnp.log1p(jnp.exp(-jnp.abs(x)))


def _sigmoid(x):
    return 0.5 * (jnp.tanh(0.5 * x) + 1.0)


def _gelu_tanh(x):
    c = 0.7978845608028654
    return 0.5 * x * (1.0 + jnp.tanh(c * (x + 0.044715 * (x * x * x))))


def _expm1(x):
    return jnp.tanh(0.5 * x) * (jnp.exp(x) + 1.0)


def _split3(x):
    hi = x.astype(BF16)
    r1 = x - hi.astype(F32)
    mid = r1.astype(BF16)
    lo = (r1 - mid.astype(F32)).astype(BF16)
    return hi, mid, lo


def _dot01(mask_bf16, x):
    hi, mid, lo = _split3(x)
    acc = jnp.dot(mask_bf16, lo, preferred_element_type=F32)
    acc = acc + jnp.dot(mask_bf16, mid, preferred_element_type=F32)
    return acc + jnp.dot(mask_bf16, hi, preferred_element_type=F32)


def _rmsnorm_body(x_ref, g_ref, o_ref):
    x = x_ref[...]
    inv = lax.rsqrt(jnp.mean(x * x, axis=-1, keepdims=True) + EPS)
    o_ref[...] = ((x * inv) * g_ref[...]).astype(o_ref.dtype)


def rmsnorm(x, g, out_dtype=BF16, rows=512, row0=0, nrows=None):
    d = x.shape[1]
    nrows = x.shape[0] if nrows is None else nrows
    b0 = row0 // rows
    return pl.pallas_call(
        _rmsnorm_body,
        grid=(nrows // rows,),
        in_specs=[pl.BlockSpec((rows, d), lambda i: (b0 + i, 0)), pl.BlockSpec((1, d), lambda i: (0, 0))],
        out_specs=pl.BlockSpec((rows, d), lambda i: (i, 0)),
        out_shape=jax.ShapeDtypeStruct((nrows, d), out_dtype),
        compiler_params=_cparams("parallel"),
        name="rmsnorm",
    )(x, g.reshape(1, d))


def _place_rows_body(buf_ref, rows_ref, o_ref):
    del buf_ref
    o_ref[...] = rows_ref[...]


def place_rows(buf, rows, row0):
    r, c = rows.shape
    assert row0 % r == 0 and buf.shape[1] == c and buf.dtype == rows.dtype
    return pl.pallas_call(
        _place_rows_body,
        grid=(1,),
        in_specs=[pl.BlockSpec(memory_space=pl.ANY), pl.BlockSpec((r, c), lambda i: (0, 0))],
        out_specs=pl.BlockSpec((r, c), lambda i: (row0 // r, 0)),
        out_shape=jax.ShapeDtypeStruct(buf.shape, buf.dtype),
        input_output_aliases={0: 0},
        compiler_params=_cparams("arbitrary"),
        name="place_rows",
    )(buf, rows)


def _mm_body(*refs, n_in, has_res, has_side, act, kc, nj, w_t):
    x_refs = refs[:n_in]
    w_ref = refs[n_in]
    res_ref = refs[n_in + 1] if has_res else None
    n_io = n_in + 1 + (1 if has_res else 0)
    side_ref = refs[n_io] if has_side else None
    n_io += 1 if has_side else 0
    o_ref = refs[n_io]
    side_o_ref = refs[n_io + 1] if has_side else None
    wbf = refs[-1]
    p, i = pl.program_id(0), pl.program_id(1)

    @pl.when(p > 0)
    def _():
        if has_side:
            side_o_ref[...] = side_ref[...].astype(BF16)
        slot = (p - 1) % 2
        k0 = 0
        acc = None
        for x_ref in x_refs:
            kw = x_ref.shape[1]
            if w_t:
                part = lax.dot_general(x_ref[...], wbf[slot, :, k0:k0 + kw], (((1,), (1,)), ((), ())),
                                       preferred_element_type=F32)
            else:
                part = jnp.dot(x_ref[...], wbf[slot, k0:k0 + kw, :], preferred_element_type=F32)
            acc = part if acc is None else acc + part
            k0 += kw
        if act == "relu2":
            r = jnp.maximum(acc, 0.0)
            acc = r * r
        if has_res:
            acc = res_ref[...] + acc
        o_ref[...] = acc.astype(o_ref.dtype)

    @pl.when(p < nj)
    def _():
        chunk = pl.ds(pl.multiple_of(i * kc, kc), kc)
        if w_t:
            wbf[p % 2, :, chunk] = w_ref[...].astype(BF16)
        else:
            wbf[p % 2, chunk, :] = w_ref[...].astype(BF16)


def matmul(xs, w, *, layer=None, n_out=None, w_t=False, res=None, act=None, side=None, out_dtype=F32, tm, tn, name):
    m = xs[0].shape[0]
    kk = sum(x.shape[1] for x in xs)
    w_k, w_n = (w.shape[-1], w.shape[-2]) if w_t else (w.shape[-2], w.shape[-1])
    n = w_n if n_out is None else n_out
    ni, nj = m // tm, n // tn
    kc = kk // ni
    assert w_k == kk and kk % ni == 0 and kc % V7X_LANES == 0 and n % tn == 0 and m % tm == 0
    n_in = len(xs)

    def row(p, i):
        return jnp.where(p > 0, i, 0)

    def w_idx(p, i):
        blk = (jnp.where(p < nj, i, ni - 1), jnp.minimum(p, nj - 1))
        blk = blk[::-1] if w_t else blk
        return blk if layer is None else (layer,) + blk

    w_blk = (tn, kc) if w_t else (kc, tn)
    in_specs = [pl.BlockSpec((tm, x.shape[1]), lambda p, i: (row(p, i), 0)) for x in xs]
    in_specs.append(pl.BlockSpec(w_blk if layer is None else (None,) + w_blk, w_idx))
    args = list(xs) + [w]
    out_idx = lambda p, i: (row(p, i), jnp.maximum(p - 1, 0))
    if res is not None:
        in_specs.append(pl.BlockSpec((tm, tn), out_idx))
        args.append(res)
    out_specs = pl.BlockSpec((tm, tn), out_idx)
    out_shape = jax.ShapeDtypeStruct((m, n), out_dtype)
    if side is not None:
        s_arr, s_layer = side
        s_rows, s_cols = s_arr.shape[-2:]
        rb = s_rows // (nj * ni)
        assert s_rows % (nj * ni) == 0 and rb % 16 == 0
        slab = lambda p, i: jnp.maximum(p - 1, 0) * ni + row(p, i)
        in_specs.append(pl.BlockSpec((None, rb, s_cols), lambda p, i: (s_layer, slab(p, i), 0)))
        args.append(s_arr)
        out_specs = [out_specs, pl.BlockSpec((rb, s_cols), lambda p, i: (slab(p, i), 0))]
        out_shape = [out_shape, jax.ShapeDtypeStruct((s_rows, s_cols), BF16)]
    return pl.pallas_call(
        functools.partial(_mm_body, n_in=n_in, has_res=res is not None, has_side=side is not None, act=act, kc=kc,
                          nj=nj, w_t=w_t),
        grid=(nj + 1, ni),
        in_specs=in_specs,
        out_specs=out_specs,
        out_shape=out_shape,
        scratch_shapes=[pltpu.VMEM((2, tn, kk) if w_t else (2, kk, tn), BF16)],
        compiler_params=_cparams("arbitrary", "arbitrary"),
        name=name,
    )(*args)


def _mm_acc_body(x_ref, w_ref, res_ref, o_ref):
    k = pl.program_id(2)

    @pl.when(k == 0)
    def _():
        o_ref[...] = res_ref[...] + jnp.dot(x_ref[...], w_ref[...], preferred_element_type=F32)

    @pl.when(k != 0)
    def _():
        o_ref[...] += jnp.dot(x_ref[...], w_ref[...], preferred_element_type=F32)


def matmul_ksplit(x, w, res, *, tm, tn, tk, name):
    m, kk = x.shape
    n = w.shape[-1]
    return pl.pallas_call(
        _mm_acc_body,
        grid=(m // tm, n // tn, kk // tk),
        in_specs=[
            pl.BlockSpec((tm, tk), lambda i, j, k: (i, k)),
            pl.BlockSpec((tk, tn), lambda i, j, k: (k, j)),
            pl.BlockSpec((tm, tn), lambda i, j, k: (i, j)),
        ],
        out_specs=pl.BlockSpec((tm, tn), lambda i, j, k: (i, j)),
        out_shape=jax.ShapeDtypeStruct((m, n), F32),
        compiler_params=_cparams("parallel", "parallel", "arbitrary"),
        name=name,
    )(x, w, res)


def _gates_body(u_ref, w_ref, b_ref, gc_ref, gr_ref, *, tm, n_prompt_tiles, seg_sample):
    L = MLSTM_CHUNK
    g = jnp.dot(u_ref[...], w_ref[...], preferred_element_type=F32) + b_ref[...]
    lane = lax.broadcasted_iota(jnp.int32, (tm, V7X_LANES), 1)
    val = jnp.where(lane < 8, g, _log_sigmoid(g))
    row = lax.broadcasted_iota(jnp.int32, (L, L), 0)
    col = lax.broadcasted_iota(jnp.int32, (L, L), 1)
    is_sample = pl.program_id(0) >= n_prompt_tiles
    tril = jnp.where(col <= row, 1.0, 0.0)
    tril_seg = jnp.where((col <= row) & ((row // seg_sample) == (col // seg_sample)), 1.0, 0.0)
    mask = jnp.where(is_sample, tril_seg, tril).astype(BF16)
    lane_l = lax.broadcasted_iota(jnp.int32, (L, V7X_LANES), 1)
    for s in range(tm // L):
        x = val[s * L:(s + 1) * L]
        out = jnp.where(lane_l < 8, x, _dot01(mask, x))
        gr_ref[:, s * L:(s + 1) * L] = out.T[:16]
        for c in range(16):
            gc_ref[c, s * L:(s + 1) * L, :] = jnp.broadcast_to(out[:, c:c + 1], (L, V7X_LANES))


def mlstm_gates(u, w_if, b_if, *, n_prompt_rows, seg_sample, tm=512):
    t, d = u.shape
    return pl.pallas_call(
        functools.partial(_gates_body, tm=tm, n_prompt_tiles=n_prompt_rows // tm, seg_sample=seg_sample),
        grid=(t // tm,),
        in_specs=[
            pl.BlockSpec((tm, d), lambda i: (i, 0)),
            pl.BlockSpec((d, V7X_LANES), lambda i: (0, 0)),
            pl.BlockSpec((1, V7X_LANES), lambda i: (0, 0)),
        ],
        out_specs=[
            pl.BlockSpec((16, tm, V7X_LANES), lambda i: (0, i, 0)),
            pl.BlockSpec((16, tm), lambda i: (0, i)),
        ],
        out_shape=[
            jax.ShapeDtypeStruct((16, t, V7X_LANES), F32),
            jax.ShapeDtypeStruct((16, t), F32),
        ],
        compiler_params=_cparams("parallel"),
        name="mlstm_gates",
    )(u, w_if, b_if)


def _head_out(hh, o, mh):
    hn = hh * lax.rsqrt(jnp.mean(hh * hh, axis=-1, keepdims=True) + EPS)
    return _sigmoid(o) * (hn * mh)


def _mlstm_prompt_body(q_ref, k_ref, v_ref, o_ref, ic_ref, bc_ref, ir_ref, br_ref, mh_ref,
                       ya_ref, c_ref, n_ref, m_ref, c_scr, n_scr, m_scr, *, n_heads, seq, scale):
    L = MLSTM_CHUNK
    c_scr[...] = jnp.zeros_like(c_scr)
    n_scr[...] = jnp.zeros_like(n_scr)
    m_scr[...] = jnp.zeros_like(m_scr)
    row = lax.broadcasted_iota(jnp.int32, (L, L), 0)
    col = lax.broadcasted_iota(jnp.int32, (L, L), 1)
    causal = col <= row

    def chunk(ci, carry):
        r0 = pl.multiple_of(ci * L, L)
        q = q_ref[pl.ds(r0, L), :] * scale
        k = k_ref[pl.ds(r0, L), :]
        v = v_ref[pl.ds(r0, L), :]
        bc = bc_ref[0, pl.ds(r0, L), :]
        ic = ic_ref[0, pl.ds(r0, L), :]
        i_row = ir_ref[0, :, pl.ds(r0, L)]
        b_row = br_ref[0, :, pl.ds(r0, L)]
        m_prev = m_scr[...]
        c_prev = c_scr[...]
        n_prev = n_scr[...]
        dmat = jnp.where(causal, bc - b_row + i_row, -jnp.inf)
        g = bc + m_prev
        mt = jnp.maximum(g, jnp.max(dmat, axis=-1, keepdims=True))
        w_inter = jnp.exp(g - mt)
        qb = q.astype(BF16)
        kb = k.astype(BF16)
        vb = v.astype(BF16)
        sc = lax.dot_general(qb, kb, (((1,), (1,)), ((), ())), preferred_element_type=F32) * jnp.exp(dmat - mt)
        inter = jnp.dot(qb, c_prev.astype(BF16), preferred_element_type=F32)
        num = jnp.concatenate([w_inter, w_inter], axis=1) * inter + jnp.dot(sc.astype(BF16), vb, preferred_element_type=F32)
        qn = jnp.sum(q * n_prev, axis=-1, keepdims=True)
        den = w_inter[:, :1] * qn + jnp.sum(sc, axis=-1, keepdims=True)
        hh = num / jnp.maximum(jnp.abs(den), jnp.exp(-mt[:, :1]))
        ya_ref[pl.ds(r0, L), :] = _head_out(hh, o_ref[pl.ds(r0, L), :], mh_ref[...]).astype(ya_ref.dtype)
        m_new = mt[L - 1:L, :1]
        b_last = b_row[:, L - 1:L]
        decay = jnp.exp(b_last + m_prev - m_new)
        kw = k * jnp.exp(b_last - bc + ic - m_new)
        c_scr[...] = decay * c_prev + lax.dot_general(kw.astype(BF16), vb, (((0,), (0,)), ((), ())), preferred_element_type=F32)
        n_scr[...] = decay * n_prev + jnp.sum(kw, axis=0, keepdims=True)
        m_scr[...] = m_new
        return carry

    lax.fori_loop(0, seq // L, chunk, 0, unroll=2)
    c_ref[0, 0] = c_scr[...]
    n_ref[0, 0] = n_scr[...]
    m_ref[0, 0] = m_scr[...]


def mlstm_prompt(proj, gc, gr, mh_norm, *, batch, seq, n_heads, dk, dv, col_q, col_k, col_v, col_o):
    L = MLSTM_CHUNK
    assert dk == L and dv == 2 * L
    rows = proj.shape[0]
    return pl.pallas_call(
        functools.partial(_mlstm_prompt_body, n_heads=n_heads, seq=seq, scale=dk ** -0.5),
        grid=(batch, n_heads),
        in_specs=[
            pl.BlockSpec((seq, dk), lambda b, h: (b, col_q // dk + h)),
            pl.BlockSpec((seq, dk), lambda b, h: (b, col_k // dk + h)),
            pl.BlockSpec((seq, dv), lambda b, h: (b, col_v // dv + h)),
            pl.BlockSpec((seq, dv), lambda b, h: (b, col_o // dv + h)),
            pl.BlockSpec((1, seq, V7X_LANES), lambda b, h: (h, b, 0)),
            pl.BlockSpec((1, seq, V7X_LANES), lambda b, h: (h + n_heads, b, 0)),
            pl.BlockSpec((1, 1, seq), lambda b, h: (h, 0, b)),
            pl.BlockSpec((1, 1, seq), lambda b, h: (h + n_heads, 0, b)),
            pl.BlockSpec((1, dv), lambda b, h: (0, h)),
        ],
        out_specs=[
            pl.BlockSpec((seq, dv), lambda b, h: (b, h)),
            pl.BlockSpec((1, 1, dk, dv), lambda b, h: (b, h, 0, 0)),
            pl.BlockSpec((1, 1, 1, dk), lambda b, h: (b, h, 0, 0)),
            pl.BlockSpec((1, 1, 1, 1), lambda b, h: (b, h, 0, 0)),
        ],
        out_shape=[
            jax.ShapeDtypeStruct((rows, n_heads * dv), BF16),
            jax.ShapeDtypeStruct((batch, n_heads, dk, dv), F32),
            jax.ShapeDtypeStruct((batch, n_heads, 1, dk), F32),
            jax.ShapeDtypeStruct((batch, n_heads, 1, 1), F32),
        ],
        scratch_shapes=[pltpu.VMEM((dk, dv), F32), pltpu.VMEM((1, dk), F32), pltpu.VMEM((1, 1), F32)],
        compiler_params=_cparams("parallel", "parallel"),
        name="mlstm_prompt",
    )(proj, proj, proj, proj, gc, gc, gr, gr, mh_norm)


def _mlstm_sample_body(q_ref, k_ref, v_ref, o_ref, ic_ref, bc_ref, ir_ref, br_ref, mh_ref, c0_ref, n0_ref, m0_ref,
                       ya_ref, c_ref, n_ref, m_ref, inter_scr, *, n_heads, seg, scale):
    L = MLSTM_CHUNK
    nb = L // seg
    row = lax.broadcasted_iota(jnp.int32, (L, L), 0)
    col = lax.broadcasted_iota(jnp.int32, (L, L), 1)
    same = (row // seg) == (col // seg)
    causal = (col <= row) & same
    last = col == (row // seg) * seg + (seg - 1)
    q = q_ref[...] * scale
    k = k_ref[...]
    v = v_ref[...]
    bc = bc_ref[0]
    ic = ic_ref[0]
    i_row = ir_ref[0]
    b_row = br_ref[0]
    m_prev = m0_ref[0]
    n_prev = n0_ref[0]
    dmat = jnp.where(causal, bc - b_row + i_row, -jnp.inf)
    g = bc + m_prev
    mt = jnp.maximum(g, jnp.max(dmat, axis=-1, keepdims=True))
    w_inter = jnp.exp(g - mt)
    qb = q.astype(BF16)
    kb = k.astype(BF16)
    vb = v.astype(BF16)
    sc = lax.dot_general(qb, kb, (((1,), (1,)), ((), ())), preferred_element_type=F32) * jnp.exp(dmat - mt)

    per_group = V7X_SUBLANES // seg
    sub = lax.broadcasted_iota(jnp.int32, (V7X_SUBLANES, 2 * L), 0)
    for gi in range(L // V7X_SUBLANES):
        q8 = qb[gi * V7X_SUBLANES:(gi + 1) * V7X_SUBLANES]
        acc = jnp.zeros((V7X_SUBLANES, 2 * L), F32)
        for j in range(per_group):
            r = jnp.dot(q8, c0_ref[0, gi * per_group + j, 0].astype(BF16), preferred_element_type=F32)
            acc = jnp.where((sub // seg) == j, r, acc)
        inter_scr[gi * V7X_SUBLANES:(gi + 1) * V7X_SUBLANES, :] = acc
    inter = inter_scr[...]

    num = jnp.concatenate([w_inter, w_inter], axis=1) * inter + jnp.dot(sc.astype(BF16), vb, preferred_element_type=F32)
    qn = jnp.sum(q * n_prev, axis=-1, keepdims=True)
    den = w_inter[:, :1] * qn + jnp.sum(sc, axis=-1, keepdims=True)
    hh = num / jnp.maximum(jnp.abs(den), jnp.exp(-mt[:, :1]))
    ya_ref[...] = _head_out(hh, o_ref[...], mh_ref[...]).astype(ya_ref.dtype)

    b_last = jnp.sum(jnp.where(last, jnp.broadcast_to(b_row, (L, L)), 0.0), axis=-1, keepdims=True)
    mt_row = mt.T
    m_new = jnp.sum(jnp.where(last, mt_row, 0.0), axis=-1, keepdims=True)
    decay = jnp.exp(b_last + m_prev - m_new)
    kw = k * jnp.exp(b_last - bc + ic - m_new)
    seg_ones = jnp.where(same, 1.0, 0.0).astype(BF16)
    n_ref[0] = decay * n_prev + _dot01(seg_ones, kw)
    m_ref[0] = jnp.broadcast_to(m_new, (L, L))
    rowk = lax.broadcasted_iota(jnp.int32, (L, L), 0)
    for bi in range(nb):
        kw_b = jnp.where((rowk // seg) == bi, kw, 0.0).astype(BF16)
        upd = lax.dot_general(kw_b, vb, (((0,), (0,)), ((), ())), preferred_element_type=F32)
        d_b = decay[bi * seg:bi * seg + 1, :1]
        c_ref[0, bi, 0] = d_b * c0_ref[0, bi, 0] + upd


def mlstm_sample(proj, gc, gr, mh_norm, c0, n0_tok, m0_tok, *, row0, n_seq, seg, n_heads, dk, dv,
                 col_q, col_k, col_v, col_o):
    L = MLSTM_CHUNK
    assert dk == L and dv == 2 * L and V7X_SUBLANES % seg == 0 and row0 % L == 0
    rows = n_seq * seg
    nb = L // seg
    t0 = row0 // L
    return pl.pallas_call(
        functools.partial(_mlstm_sample_body, n_heads=n_heads, seg=seg, scale=dk ** -0.5),
        grid=(rows // L, n_heads),
        in_specs=[
            pl.BlockSpec((L, dk), lambda j, h: (t0 + j, col_q // dk + h)),
            pl.BlockSpec((L, dk), lambda j, h: (t0 + j, col_k // dk + h)),
            pl.BlockSpec((L, dv), lambda j, h: (t0 + j, col_v // dv + h)),
            pl.BlockSpec((L, dv), lambda j, h: (t0 + j, col_o // dv + h)),
            pl.BlockSpec((1, L, V7X_LANES), lambda j, h: (h, t0 + j, 0)),
            pl.BlockSpec((1, L, V7X_LANES), lambda j, h: (h + n_heads, t0 + j, 0)),
            pl.BlockSpec((1, 1, L), lambda j, h: (h, 0, t0 + j)),
            pl.BlockSpec((1, 1, L), lambda j, h: (h + n_heads, 0, t0 + j)),
            pl.BlockSpec((1, dv), lambda j, h: (0, h)),
            pl.BlockSpec((1, nb, 1, dk, dv), lambda j, h: (0, j, h, 0, 0)),
            pl.BlockSpec((1, L, dk), lambda j, h: (h, j, 0)),
            pl.BlockSpec((1, L, L), lambda j, h: (h, j, 0)),
        ],
        out_specs=[
            pl.BlockSpec((L, dv), lambda j, h: (j, h)),
            pl.BlockSpec((1, nb, 1, dk, dv), lambda j, h: (0, j, h, 0, 0)),
            pl.BlockSpec((1, L, dk), lambda j, h: (h, j, 0)),
            pl.BlockSpec((1, L, L), lambda j, h: (h, j, 0)),
        ],
        out_shape=[
            jax.ShapeDtypeStruct((rows, n_heads * dv), BF16),
            jax.ShapeDtypeStruct((1, n_seq, n_heads, dk, dv), F32),
            jax.ShapeDtypeStruct((n_heads, rows, dk), F32),
            jax.ShapeDtypeStruct((n_heads, rows, L), F32),
        ],
        scratch_shapes=[pltpu.VMEM((L, dv), F32)],
        compiler_params=_cparams("parallel", "parallel"),
        name="mlstm_sample",
    )(proj, proj, proj, proj, gc, gc, gr, gr, mh_norm, c0, n0_tok, m0_tok)


def _lru_gates(xc, wa_ref, wi_ref, ba, bi, lam):
    nblk = wa_ref.shape[0]
    blk = wa_ref.shape[1]
    rs, igs = [], []
    for n in range(nblk):
        xb = xc[:, n * blk:(n + 1) * blk].astype(BF16)
        rs.append(jnp.dot(xb, wa_ref[n], preferred_element_type=F32))
        igs.append(jnp.dot(xb, wi_ref[n], preferred_element_type=F32))
    r = _sigmoid(jnp.concatenate(rs, axis=1) + ba)
    ig = _sigmoid(jnp.concatenate(igs, axis=1) + bi)
    log_a = LRU_C * r * _log_sigmoid(lam)
    a = jnp.exp(log_a)
    u = jnp.sqrt(-_expm1(2.0 * log_a)) * (ig * xc)
    return a, u


def _lru_prompt_body(x_ref, g_ref, halo_ref, h0_ref, cw_ref, cb_ref, wa_ref, wi_ref, ba_ref, bi_ref, lam_ref,
                     y_ref, hl_ref, tail_ref, h_scr, *, seq):
    Lc = SCAN_CHUNK
    cbw = x_ref.shape[1]
    h_scr[...] = h0_ref[0]
    sub = lax.broadcasted_iota(jnp.int32, (Lc, cbw), 0) % V7X_SUBLANES
    cw = cw_ref[...]
    width = cw.shape[0]

    def chunk(ci, carry):
        r0 = pl.multiple_of(ci * Lc, Lc)
        x = x_ref[pl.ds(r0, Lc), :]
        prev = x_ref[pl.ds(pl.multiple_of(jnp.maximum(r0 - V7X_SUBLANES, 0), V7X_SUBLANES), V7X_SUBLANES), :]
        prev = jnp.where(ci == 0, halo_ref[0], prev)
        xfull = jnp.concatenate([prev, x], axis=0)
        xc = None
        for j in range(width - 1, 0, -1):
            term = pltpu.roll(xfull, j, 0)[V7X_SUBLANES:] * cw[width - 1 - j:width - j]
            xc = term if xc is None else xc + term
        xc = xc + x * cw[width - 1:width] + cb_ref[...]
        a, u = _lru_gates(xc, wa_ref, wi_ref, ba_ref[...], bi_ref[...], lam_ref[...])
        d = 1
        while d < V7X_SUBLANES:
            ok = sub >= d
            a_sh = jnp.where(ok, pltpu.roll(a, d, 0), 1.0)
            u_sh = jnp.where(ok, pltpu.roll(u, d, 0), 0.0)
            u = a * u_sh + u
            a = a * a_sh
            d *= 2
        hc = h_scr[...]
        groups = []
        for gi in range(Lc // V7X_SUBLANES):
            rows = slice(gi * V7X_SUBLANES, (gi + 1) * V7X_SUBLANES)
            hg = u[rows] + a[rows] * hc
            groups.append(hg)
            hc = hg[V7X_SUBLANES - 1:V7X_SUBLANES]
        hs = jnp.concatenate(groups, axis=0)
        h_scr[...] = hc
        y_ref[pl.ds(r0, Lc), :] = (hs * _gelu_tanh(g_ref[pl.ds(r0, Lc), :])).astype(y_ref.dtype)
        return carry

    lax.fori_loop(0, seq // Lc, chunk, 0)
    hl_ref[0] = h_scr[...]
    tail_ref[0] = x_ref[pl.ds(seq - V7X_SUBLANES, V7X_SUBLANES), :]


def lru_prompt(proj, halo, h0, cw, cb, wa, wi, ba, bi, lam, *, batch, seq, width, col_x, col_g, cbw=256):
    blk = wa.shape[1]
    nper = cbw // blk
    vec = lambda b, c: (0, c)
    return pl.pallas_call(
        functools.partial(_lru_prompt_body, seq=seq),
        grid=(batch, width // cbw),
        in_specs=[
            pl.BlockSpec((seq, cbw), lambda b, c: (b, col_x // cbw + c)),
            pl.BlockSpec((seq, cbw), lambda b, c: (b, col_g // cbw + c)),
            pl.BlockSpec((1, V7X_SUBLANES, cbw), lambda b, c: (b, 0, c)),
            pl.BlockSpec((1, 1, cbw), lambda b, c: (b, 0, c)),
            pl.BlockSpec((cw.shape[0], cbw), vec),
            pl.BlockSpec((1, cbw), vec),
            pl.BlockSpec((nper, blk, blk), lambda b, c: (c, 0, 0)),
            pl.BlockSpec((nper, blk, blk), lambda b, c: (c, 0, 0)),
            pl.BlockSpec((1, cbw), vec),
            pl.BlockSpec((1, cbw), vec),
            pl.BlockSpec((1, cbw), vec),
        ],
        out_specs=[
            pl.BlockSpec((seq, cbw), lambda b, c: (b, c)),
            pl.BlockSpec((1, 1, cbw), lambda b, c: (b, 0, c)),
            pl.BlockSpec((1, V7X_SUBLANES, cbw), lambda b, c: (b, 0, c)),
        ],
        out_shape=[
            jax.ShapeDtypeStruct((proj.shape[0], width), BF16),
            jax.ShapeDtypeStruct((batch, 1, width), F32),
            jax.ShapeDtypeStruct((batch, V7X_SUBLANES, width), F32),
        ],
        scratch_shapes=[pltpu.VMEM((1, cbw), F32)],
        compiler_params=_cparams("parallel", "parallel"),
        name="lru_prompt",
    )(proj, proj, halo, h0, cw, cb, wa, wi, ba, bi, lam)


def _lru_sample_body(x_ref, g_ref, st_ref, h0_ref, cw_ref, cb_ref, wa_ref, wi_ref, ba_ref, bi_ref, lam_ref,
                     y_ref, hl_ref, *, seg):
    cw = cw_ref[...]
    width = cw.shape[0]
    hist = [st_ref[j] for j in range(width - 1)]
    h = h0_ref[...]
    for t in range(seg):
        x = x_ref[t]
        taps = hist + [x]
        xc = taps[0] * cw[0:1]
        for j in range(1, width):
            xc = xc + taps[j] * cw[j:j + 1]
        xc = xc + cb_ref[...]
        a, u = _lru_gates(xc, wa_ref, wi_ref, ba_ref[...], bi_ref[...], lam_ref[...])
        h = a * h + u
        y_ref[t] = (h * _gelu_tanh(g_ref[t])).astype(y_ref.dtype)
        hist = hist[1:] + [x]
    hl_ref[...] = h


def lru_sample(x_t, g_t, st, h0, cw, cb, wa, wi, ba, bi, lam, *, cbw=256):
    seg, n_seq, width = x_t.shape
    blk = wa.shape[1]
    nper = cbw // blk
    vec = lambda c: (0, c)
    return pl.pallas_call(
        functools.partial(_lru_sample_body, seg=seg),
        grid=(width // cbw,),
        in_specs=[
            pl.BlockSpec((seg, n_seq, cbw), lambda c: (0, 0, c)),
            pl.BlockSpec((seg, n_seq, cbw), lambda c: (0, 0, c)),
            pl.BlockSpec((st.shape[0], n_seq, cbw), lambda c: (0, 0, c)),
            pl.BlockSpec((n_seq, cbw), vec),
            pl.BlockSpec((cw.shape[0], cbw), vec),
            pl.BlockSpec((1, cbw), vec),
            pl.BlockSpec((nper, blk, blk), lambda c: (c, 0, 0)),
            pl.BlockSpec((nper, blk, blk), lambda c: (c, 0, 0)),
            pl.BlockSpec((1, cbw), vec),
            pl.BlockSpec((1, cbw), vec),
            pl.BlockSpec((1, cbw), vec),
        ],
        out_specs=[
            pl.BlockSpec((seg, n_seq, cbw), lambda c: (0, 0, c)),
            pl.BlockSpec((n_seq, cbw), vec),
        ],
        out_shape=[
            jax.ShapeDtypeStruct((seg, n_seq, width), BF16),
            jax.ShapeDtypeStruct((n_seq, width), F32),
        ],
        compiler_params=_cparams("parallel"),
        name="lru_sample",
    )(x_t, g_t, st, h0, cw, cb, wa, wi, ba, bi, lam)


def _sconv_prompt_body(bg_ref, cg_ref, hx_ref, halo_ref, cw_ref, y_ref, tail_ref, *, seq):
    Lc = SCAN_CHUNK
    cw = cw_ref[...]
    width = cw.shape[0]

    def chunk(ci, carry):
        r0 = pl.multiple_of(ci * Lc, Lc)
        p = cg_ref[pl.ds(r0, Lc), :] * hx_ref[pl.ds(r0, Lc), :]
        rp = pl.multiple_of(jnp.maximum(r0 - V7X_SUBLANES, 0), V7X_SUBLANES)
        prev = cg_ref[pl.ds(rp, V7X_SUBLANES), :] * hx_ref[pl.ds(rp, V7X_SUBLANES), :]
        prev = jnp.where(ci == 0, halo_ref[0], prev)
        pfull = jnp.concatenate([prev, p], axis=0)
        z = None
        for j in range(width - 1, 0, -1):
            term = pltpu.roll(pfull, j, 0)[V7X_SUBLANES:] * cw[width - 1 - j:width - j]
            z = term if z is None else z + term
        z = z + p * cw[width - 1:width]
        y_ref[pl.ds(r0, Lc), :] = (bg_ref[pl.ds(r0, Lc), :] * z).astype(y_ref.dtype)
        return carry

    lax.fori_loop(0, seq // Lc, chunk, 0)
    rt = seq - V7X_SUBLANES
    tail_ref[0] = cg_ref[pl.ds(rt, V7X_SUBLANES), :] * hx_ref[pl.ds(rt, V7X_SUBLANES), :]


def sconv_prompt(proj, halo, cw, *, batch, seq, width, cbw=512):
    nb = width // cbw
    return pl.pallas_call(
        functools.partial(_sconv_prompt_body, seq=seq),
        grid=(batch, nb),
        in_specs=[
            pl.BlockSpec((seq, cbw), lambda b, c: (b, c)),
            pl.BlockSpec((seq, cbw), lambda b, c: (b, nb + c)),
            pl.BlockSpec((seq, cbw), lambda b, c: (b, 2 * nb + c)),
            pl.BlockSpec((1, V7X_SUBLANES, cbw), lambda b, c: (b, 0, c)),
            pl.BlockSpec((cw.shape[0], cbw), lambda b, c: (0, c)),
        ],
        out_specs=[
            pl.BlockSpec((seq, cbw), lambda b, c: (b, c)),
            pl.BlockSpec((1, V7X_SUBLANES, cbw), lambda b, c: (b, 0, c)),
        ],
        out_shape=[
            jax.ShapeDtypeStruct((proj.shape[0], width), BF16),
            jax.ShapeDtypeStruct((batch, V7X_SUBLANES, width), F32),
        ],
        compiler_params=_cparams("parallel", "parallel"),
        name="sconv_prompt",
    )(proj, proj, proj, halo, cw)


def _sconv_sample_body(bg_ref, cg_ref, hx_ref, st_ref, cw_ref, y_ref, ps_ref, *, seg):
    cw = cw_ref[...]
    width = cw.shape[0]
    hist = [st_ref[j] for j in range(width - 1)]
    for t in range(seg):
        p = cg_ref[t] * hx_ref[t]
        ps_ref[t] = p
        taps = hist + [p]
        z = taps[0] * cw[0:1]
        for j in range(1, width):
            z = z + taps[j] * cw[j:j + 1]
        y_ref[t] = (bg_ref[t] * z).astype(y_ref.dtype)
        hist = hist[1:] + [p]


def sconv_sample(proj_t, st, cw, *, width, cbw=512):
    seg, n_seq, _ = proj_t.shape
    nb = width // cbw
    return pl.pallas_call(
        functools.partial(_sconv_sample_body, seg=seg),
        grid=(nb,),
        in_specs=[
            pl.BlockSpec((seg, n_seq, cbw), lambda c: (0, 0, c)),
            pl.BlockSpec((seg, n_seq, cbw), lambda c: (0, 0, nb + c)),
            pl.BlockSpec((seg, n_seq, cbw), lambda c: (0, 0, 2 * nb + c)),
            pl.BlockSpec((st.shape[0], n_seq, cbw), lambda c: (0, 0, c)),
            pl.BlockSpec((cw.shape[0], cbw), lambda c: (0, c)),
        ],
        out_specs=[
            pl.BlockSpec((seg, n_seq, cbw), lambda c: (0, 0, c)),
            pl.BlockSpec((seg, n_seq, cbw), lambda c: (0, 0, c)),
        ],
        out_shape=[
            jax.ShapeDtypeStruct((seg, n_seq, width), BF16),
            jax.ShapeDtypeStruct((seg, n_seq, width), F32),
        ],
        compiler_params=_cparams("parallel"),
        name="sconv_sample",
    )(proj_t, proj_t, proj_t, st, cw)


def _softmax_rows(s):
    e = jnp.exp(s - jnp.max(s, axis=-1, keepdims=True))
    return e / jnp.sum(e, axis=-1, keepdims=True)


def _xattn_prompt_body(q_ref, k_ref, v_ref, o_ref, *, scale):
    s = lax.dot_general(q_ref[...], k_ref[...].astype(BF16), (((1,), (1,)), ((), ())), preferred_element_type=F32) * scale
    p = _softmax_rows(s).astype(BF16)
    o_ref[...] = jnp.dot(p, v_ref[...].astype(BF16), preferred_element_type=F32).astype(o_ref.dtype)


def xattn_prompt(q, mk, mv, *, batch, seq, n_mem, n_heads, hd, tq=2048):
    nq = seq // tq
    return pl.pallas_call(
        functools.partial(_xattn_prompt_body, scale=hd ** -0.5),
        grid=(batch, n_heads, nq),
        in_specs=[
            pl.BlockSpec((tq, hd), lambda b, h, i: (b * nq + i, h)),
            pl.BlockSpec((n_mem, hd), lambda b, h, i: (b, h)),
            pl.BlockSpec((n_mem, hd), lambda b, h, i: (b, h)),
        ],
        out_specs=pl.BlockSpec((tq, hd), lambda b, h, i: (b * nq + i, h)),
        out_shape=jax.ShapeDtypeStruct((q.shape[0], n_heads * hd), BF16),
        compiler_params=_cparams("parallel", "parallel", "parallel"),
        name="xattn_prompt",
    )(q, mk, mv)


def _xattn_sample_body(q_ref, k_ref, v_ref, o_ref, *, n_heads, scale):
    n_mem, hd = k_ref.shape[2], k_ref.shape[4]
    k2 = k_ref[0, 0].reshape(n_mem * n_heads, hd).astype(BF16)
    v2 = v_ref[0, 0].reshape(n_mem * n_heads, hd).astype(BF16)
    s = lax.dot_general(q_ref[0], k2, (((1,), (1,)), ((), ())), preferred_element_type=F32) * scale
    row = lax.broadcasted_iota(jnp.int32, s.shape, 0)
    col = lax.broadcasted_iota(jnp.int32, s.shape, 1)
    s = jnp.where((row % n_heads) == (col % n_heads), s, -jnp.inf)
    p = _softmax_rows(s).astype(BF16)
    o_ref[0] = jnp.dot(p, v2, preferred_element_type=F32).astype(o_ref.dtype)


def xattn_sample(q3, cache_k, cache_v, layer):
    n_seq, rows, hd = q3.shape
    _, _, n_mem, n_heads, _ = cache_k.shape
    kv_spec = pl.BlockSpec((1, 1, n_mem, n_heads, hd), lambda b: (layer, b, 0, 0, 0))
    return pl.pallas_call(
        functools.partial(_xattn_sample_body, n_heads=n_heads, scale=hd ** -0.5),
        grid=(n_seq,),
        in_specs=[pl.BlockSpec((1, rows, hd), lambda b: (b, 0, 0)), kv_spec, kv_spec],
        out_specs=pl.BlockSpec((1, rows, hd), lambda b: (b, 0, 0)),
        out_shape=jax.ShapeDtypeStruct((n_seq, rows, hd), BF16),
        compiler_params=_cparams("parallel"),
        name="xattn_sample",
    )(q3, cache_k, cache_v)


def kernel(x_prompt, x_sample, mem_prompt, cache_mem_k, cache_mem_v, state_mlstm_c, state_mlstm_n, state_mlstm_m, state_lru_h, state_lru_conv, state_sconv, norm_mix, norm_x, norm_ff, norm_final, w_in_e, b_if, mh_norm, conv_b_w, conv_b_b, lru_wa, lru_ba, lru_wi, lru_bi, lru_lam, w_out_e, w_in_o, conv_c_w, w_out_o, w_xq, w_xk, w_xv, w_xo, w_ff1, w_ff2):
    bp, sp, d = x_prompt.shape
    bs, ss, _ = x_sample.shape
    n_mem = mem_prompt.shape[1]
    depth, _, _, xh, xhd = cache_mem_k.shape
    _, _, ah, dk, dv = state_mlstm_c.shape
    bw = state_lru_h.shape[-1]
    bconv = state_lru_conv.shape[2] + 1
    cconv = state_sconv.shape[2] + 1
    cwid = state_sconv.shape[-1]
    aw = ah * dv
    aqk = ah * dk
    tp = bp * sp
    ts = bs * ss
    t = tp + ts
    TM = 1088
    TN = 1024
    TK2 = 4096
    TM_RES = 544
    TM_MEM = 512
    assert t % TM == 0 and tp % 512 == 0 and ts == 512

    x = jnp.concatenate([x_prompt.reshape(tp, d), x_sample.reshape(ts, d)], axis=0)
    mem_bf = mem_prompt.reshape(bp * n_mem, d).astype(BF16)

    outs = {}
    ie = io = 0
    p_mem_k, p_mem_v = [], []
    for l in range(depth):
        u = rmsnorm(x, norm_mix[l])
        if l % 2 == 0:
            wt = jnp.swapaxes(w_in_e[ie], 0, 1)
            o_qkvo = 2 * aqk + 2 * aw
            o_if = o_qkvo + 2 * ah
            w_if = jnp.pad(wt[o_qkvo:o_if].T, ((0, 0), (0, V7X_LANES - 2 * ah))).astype(BF16)
            bias_if = jnp.pad(b_if[ie], (0, V7X_LANES - 2 * ah)).reshape(1, V7X_LANES)
            col_q, col_k, col_v, col_o = 0, aqk, 2 * aqk, 2 * aqk + aw
            col_x, col_g = 0, bw
            proj = matmul([u], wt, n_out=o_qkvo, w_t=True, tm=TM, tn=TN, name="in_proj_even_qkvo")
            proj_b = matmul([u], wt[o_if:], w_t=True, tm=TM, tn=TN, name="in_proj_even_lru")
            gc, gr = mlstm_gates(u, w_if, bias_if, n_prompt_rows=tp, seg_sample=ss)
            gr = gr.reshape(16, 1, t)
            mh = mh_norm[ie].reshape(1, aw)
            mkw = dict(n_heads=ah, dk=dk, dv=dv, col_q=col_q, col_k=col_k, col_v=col_v, col_o=col_o)
            ya_p, pc, pn, pm = mlstm_prompt(proj, gc, gr, mh, batch=bp, seq=sp, **mkw)
            n0_tok = jnp.repeat(jnp.transpose(state_mlstm_n[ie], (1, 0, 2)), ss, axis=1)
            m0_tok = jnp.broadcast_to(jnp.repeat(state_mlstm_m[ie].T, ss, axis=1)[:, :, None], (ah, ts, MLSTM_CHUNK))
            ya_s, sc_, sn_tok, sm_tok = mlstm_sample(proj, gc, gr, mh, state_mlstm_c[ie:ie + 1], n0_tok, m0_tok,
                                                     row0=tp, n_seq=bs, seg=ss, **mkw)
            outs.setdefault("pc", []).append(pc.reshape(bp, ah, dk, dv))
            outs.setdefault("pn", []).append(pn.reshape(bp, ah, dk))
            outs.setdefault("pm", []).append(pm.reshape(bp, ah))
            outs.setdefault("sc", []).append(sc_[0])
            outs.setdefault("sn", []).append(jnp.transpose(sn_tok[:, ss - 1::ss, :], (1, 0, 2)))
            outs.setdefault("sm", []).append(sm_tok[:, ss - 1::ss, 0].T)

            cw = conv_b_w[ie]
            cbias = conv_b_b[ie].reshape(1, bw)
            wa = lru_wa[ie].astype(BF16)
            wi = lru_wi[ie].astype(BF16)
            ba = lru_ba[ie].reshape(1, bw)
            bi = lru_bi[ie].reshape(1, bw)
            lam = lru_lam[ie].reshape(1, bw)
            halo_p = jnp.zeros((bp, V7X_SUBLANES, bw), F32)
            h0_p = jnp.zeros((bp, 1, bw), F32)
            yb_p, ph, xtail = lru_prompt(proj_b, halo_p, h0_p, cw, cbias, wa, wi, ba, bi, lam,
                                  batch=bp, seq=sp, width=bw, col_x=col_x, col_g=col_g)
            st_s = jnp.transpose(state_lru_conv[ie], (1, 0, 2))
            xb_s = proj_b[tp:, col_x:col_x + bw].reshape(bs, ss, bw)
            gb_s = proj_b[tp:, col_g:col_g + bw].reshape(bs, ss, bw)
            yb_s, sh = lru_sample(jnp.transpose(xb_s, (1, 0, 2)), jnp.transpose(gb_s, (1, 0, 2)), st_s,
                                  state_lru_h[ie], cw, cbias, wa, wi, ba, bi, lam)
            yb_s = jnp.transpose(yb_s, (1, 0, 2)).reshape(ts, bw)
            outs.setdefault("ph", []).append(ph.reshape(bp, bw))
            outs.setdefault("pcb", []).append(xtail[:, V7X_SUBLANES - (bconv - 1):])
            outs.setdefault("sh", []).append(sh)
            outs.setdefault("scb", []).append(jnp.concatenate([state_lru_conv[ie], xb_s], axis=1)[:, ss:])

            ya = place_rows(ya_p, ya_s, tp)
            yb = place_rows(yb_p, yb_s, tp)
            x = matmul([ya, yb], w_out_e, layer=ie, res=x, tm=TM_RES, tn=TN, name="out_proj_even")
            ie += 1
        else:
            proj = matmul([u], w_in_o, layer=io, tm=TM, tn=TN, name="in_proj_odd")
            cw = conv_c_w[io]
            halo_p = jnp.zeros((bp, V7X_SUBLANES, cwid), F32)
            y_p, tail = sconv_prompt(proj, halo_p, cw, batch=bp, seq=sp, width=cwid)
            st_s = jnp.transpose(state_sconv[io], (1, 0, 2))
            proj_t = jnp.transpose(proj[tp:].reshape(bs, ss, 3 * cwid), (1, 0, 2))
            y_s, ps = sconv_sample(proj_t, st_s, cw, width=cwid)
            y_s = jnp.transpose(y_s, (1, 0, 2)).reshape(ts, cwid)
            outs.setdefault("psb", []).append(tail[:, V7X_SUBLANES - (cconv - 1):])
            ps_b = jnp.transpose(ps, (1, 0, 2))
            outs.setdefault("ssb", []).append(jnp.concatenate([state_sconv[io], ps_b], axis=1)[:, ss:])
            y = place_rows(y_p, y_s, tp)
            x = matmul([y], w_out_o, layer=io, res=x, tm=TM_RES, tn=TN, name="out_proj_odd")
            io += 1

        u = rmsnorm(x, norm_x[l])
        q = matmul([u], w_xq, layer=l, out_dtype=BF16, tm=TM, tn=TN, name="xattn_q")
        mk = matmul([mem_bf], w_xk, layer=l, tm=TM_MEM, tn=TN, name="mem_k")
        mv = matmul([mem_bf], w_xv, layer=l, tm=TM_MEM, tn=TN, name="mem_v")
        p_mem_k.append(mk.reshape(bp, n_mem, xh, xhd))
        p_mem_v.append(mv.reshape(bp, n_mem, xh, xhd))
        o_p = xattn_prompt(q, mk, mv, batch=bp, seq=sp, n_mem=n_mem, n_heads=xh, hd=xhd)
        o_s = xattn_sample(q[tp:].reshape(bs, ss * xh, xhd), cache_mem_k, cache_mem_v, l)
        o = place_rows(o_p, o_s.reshape(ts, d), tp)
        x = matmul([o], w_xo, layer=l, res=x, tm=TM_RES, tn=TN, name="xattn_o")

        u = rmsnorm(x, norm_ff[l])
        hmid, w_ff2_bf = matmul([u], w_ff1, layer=l, act="relu2", side=(w_ff2, l), out_dtype=BF16, tm=TM, tn=TN,
                                name="ff1")
        x = matmul_ksplit(hmid, w_ff2_bf, x, tm=TM, tn=TN, tk=TK2, name="ff2")

    y_p = rmsnorm(x, norm_final, out_dtype=F32, row0=0, nrows=tp)
    y_s = rmsnorm(x, norm_final, out_dtype=F32, row0=tp, nrows=ts)
    dt = x_prompt.dtype
    st = lambda name: jnp.stack(outs[name]).astype(dt)
    return (y_p.reshape(bp, sp, d), y_s.reshape(bs, ss, d),
            st("pc"), st("pn"), st("pm"), st("ph"), st("pcb"), st("psb"),
            jnp.stack(p_mem_k), jnp.stack(p_mem_v),
            st("sc"), st("sn"), st("sm"), st("sh"), st("scb"), st("ssb"))
```
